```python
import math
import jax, jax.numpy as jnp
from jax import lax
import numpy as np

D_MODEL = 1024
BATCH = 2
SEQ = 8192
DEPTH = 2
DEC_BATCH = 32
DEC_SEQ = 1
PAST_LEN = 16384
PAGE_SIZE = 128

HEAD_DIM = 64
N_GROUPS = 4
GROUP_WIDTH = D_MODEL // N_GROUPS
N_HEADS = GROUP_WIDTH // HEAD_DIM
D_FF = 4 * D_MODEL
CONV_WIDTH = 4
LIN_CHUNK = 64
Q_BLOCK = 128
NORM_EPS = 1e-6
FOX_BIAS_MEAN = 2.0
GW = GROUP_WIDTH
PROJ_SIZES = (GW, GW, GW, GW, GW, GW, GW, N_HEADS, GW, GW, GW, 3 * GW, GW, N_HEADS, N_HEADS)
D_IN = sum(PROJ_SIZES)
SPLIT_POINTS = tuple(int(v) for v in np.cumsum(PROJ_SIZES)[:-1])

kernel_name = 'hymba_style_fox_stickbreak_hgrn2_gdn_step'


def rms_norm(x, g):
    x = x.astype(jnp.float32)
    return x * lax.rsqrt(jnp.mean(x * x, axis=-1, keepdims=True) + NORM_EPS) * g.astype(jnp.float32)


def l2_norm(x):
    x = x.astype(jnp.float32)
    return x * lax.rsqrt(jnp.sum(x * x, axis=-1, keepdims=True) + NORM_EPS)


def heads(a):
    return a.reshape(a.shape[0], a.shape[1], N_HEADS, HEAD_DIM)


def pad_time(a, tp):
    pad = [(0, 0)] * a.ndim
    pad[1] = (0, tp - a.shape[1])
    return jnp.pad(a, pad)


def to_chunks(a, c):
    return jnp.moveaxis(a.reshape(a.shape[0], a.shape[1] // c, c, *a.shape[2:]), 1, 0)


def from_chunks(a, t):
    a = jnp.moveaxis(a, 0, 1)
    return a.reshape(a.shape[0], a.shape[1] * a.shape[2], *a.shape[3:])[:, :t]


def hgrn2_scan(q, k, v, logf, s0):
    t = q.shape[1]
    c = min(LIN_CHUNK, t)
    tp = -(-t // c) * c
    xs = tuple(to_chunks(pad_time(a.astype(jnp.float32), tp), c) for a in (q, k, v, logf))
    incl = jnp.tril(jnp.ones((c, c), bool))

    def step(s, xc):
        qc, kc, vc, lc = xc
        g = jnp.cumsum(lc, axis=1)
        diff = g[:, :, None] - g[:, None, :]
        dec = jnp.exp(jnp.where(incl[None, :, :, None, None], diff, -jnp.inf))
        a = jnp.einsum('bthk,bshk,btshk->bhts', qc, kc, dec)
        o = (jnp.einsum('bthk,bhkv->bthv', qc * jnp.exp(g), s)
             + jnp.einsum('bhts,bshv->bthv', a, vc))
        gl = g[:, -1]
        s = jnp.exp(gl)[..., None] * s + jnp.einsum('bshk,bshv->bhkv', kc * jnp.exp(gl[:, None] - g), vc)
        return s, o

    s, o = lax.scan(step, s0.astype(jnp.float32), xs)
    return from_chunks(o, t), s


def gated_delta_scan(q, k, v, g, beta, s0):
    t = q.shape[1]
    vd = v.shape[-1]
    c = min(LIN_CHUNK, t)
    tp = -(-t // c) * c
    xs = tuple(to_chunks(pad_time(a.astype(jnp.float32), tp), c) for a in (q, k, v, g, beta))
    incl = jnp.tril(jnp.ones((c, c), bool))
    strict = jnp.tril(jnp.ones((c, c), bool), k=-1)
    eye = jnp.eye(c, dtype=jnp.float32)

    def step(s, xc):
        qc, kc, vc, gc, bc = xc
        gcum = jnp.cumsum(gc, axis=1)
        gh = jnp.moveaxis(gcum, 1, 2)
        dec = jnp.exp(jnp.where(incl, gh[..., :, None] - gh[..., None, :], -jnp.inf))
        kb = kc * bc[..., None]
        lmat = jnp.where(strict, jnp.einsum('bthk,bshk->bhts', kb, kc) * dec, 0.0)
        rhs = jnp.concatenate([vc * bc[..., None], kb * jnp.exp(gcum)[..., None]], axis=-1)
        sol = lax.linalg.triangular_solve(lmat + eye, jnp.moveaxis(rhs, 1, 2), left_side=True,
                                          lower=True, unit_diagonal=True)
        u, w = sol[..., :vd], sol[..., vd:]
        v_new = u - jnp.einsum('bhck,bhkv->bhcv', w, s)
        attn = jnp.einsum('bthk,bshk->bhts', qc, kc) * dec
        o = (jnp.einsum('bthk,bhkv->bthv', qc * jnp.exp(gcum)[..., None], s)
             + jnp.einsum('bhts,bhsv->bthv', attn, v_new))
        gl = gcum[:, -1]
        kd = kc * jnp.exp(gl[:, None] - gcum)[..., None]
        s = jnp.exp(gl)[..., None, None] * s + jnp.einsum('bshk,bhsv->bhkv', kd, v_new)
        return s, o

    s, o = lax.scan(step, s0.astype(jnp.float32), xs)
    return from_chunks(o, t), s


def causal_conv(x, buf, w):
    t = x.shape[1]
    xp = jnp.concatenate([buf.astype(jnp.float32), x], axis=1)
    wf = w.astype(jnp.float32)
    y = xp[:, 0:t] * wf[0]
    for j in range(1, CONV_WIDTH):
        y = y + xp[:, j:j + t] * wf[j]
    return jax.nn.silu(y), xp[:, xp.shape[1] - (CONV_WIDTH - 1):]


def fox_block(q, k, v, fq, fk, q_pos, k_pos):
    s = jnp.einsum('bqhd,bkhd->bhqk', q, k) * HEAD_DIM ** -0.5
    s = s + jnp.moveaxis(fq, 1, 2)[..., :, None] - jnp.moveaxis(fk, 1, 2)[..., None, :]
    s = jnp.where(k_pos[None, :] <= q_pos[:, None], s, -jnp.inf)
    p = jax.nn.softmax(s, axis=-1)
    return jnp.einsum('bhqk,bkhd->bqhd', p, v)


def sb_block(q, k, v, q_pos, k_pos):
    z = jnp.einsum('bqhd,bkhd->bhqk', q, k) * HEAD_DIM ** -0.5
    mask = k_pos[None, :] < q_pos[:, None]
    log_stay = jnp.where(mask, jax.nn.log_sigmoid(-z), 0.0)
    after = lax.cumsum(log_stay, axis=3, reverse=True) - log_stay
    w = jnp.exp(jnp.where(mask, jax.nn.log_sigmoid(z) + after, -jnp.inf))
    return jnp.einsum('bhqk,bkhd->bqhd', w, v)


def sweep_query_blocks(fn, q_items, q_pos):
    t = q_pos.shape[0]
    blk = Q_BLOCK if t % Q_BLOCK == 0 else t
    nb = t // blk
    blocks = tuple(jnp.moveaxis(a.reshape(a.shape[0], nb, blk, *a.shape[2:]), 1, 0) for a in q_items)
    out = lax.map(lambda xs: fn(*xs), blocks + (q_pos.reshape(nb, blk),))
    out = jnp.moveaxis(out, 0, 1)
    return out.reshape(out.shape[0], t, *out.shape[3:])


def gather_pages(cache_l, page_table):
    g = cache_l[page_table]
    return g.reshape(g.shape[0], g.shape[1] * g.shape[2], *g.shape[3:]).astype(jnp.float32)


def trunk_layer(x, past, lb, w_in, w_out, ln1_g, ln2_g, fox_f_bias, fox_q_norm, fox_k_norm,
                sb_q_norm, sb_k_norm, hgrn_out_norm, fox_out_norm, sb_out_norm, dn_out_norm,
                dn_conv_w, dn_dt_bias, dn_a_log, w_up, w_down):
    f32 = jnp.float32
    b, t, _ = x.shape
    h = rms_norm(x, ln1_g).astype(x.dtype)
    proj = jnp.einsum('btd,de->bte', h, w_in).astype(f32)
    (hg_q, hg_f, hg_i, hg_g, fx_q, fx_k, fx_v, fx_f,
     sb_q, sb_k, sb_v, dn_qkv, dn_z, dn_b, dn_a) = jnp.split(proj, SPLIT_POINTS, axis=-1)
    if past is None:
        p_len = 0
        hg_s0 = jnp.zeros((b, N_HEADS, HEAD_DIM, HEAD_DIM), f32)
        dn_s0 = jnp.zeros((b, N_HEADS, HEAD_DIM, HEAD_DIM), f32)
        dn_buf = jnp.zeros((b, CONV_WIDTH - 1, 3 * GW), f32)
    else:
        fox_k_p, fox_v_p, fox_lf_p, sb_k_p, sb_v_p, hg_s0, dn_s0, dn_buf = past
        p_len = fox_k_p.shape[1]

    lbf = lb.astype(f32)
    log_f = jnp.logaddexp(jnp.log(lbf), jnp.log1p(-lbf) + jax.nn.log_sigmoid(hg_f))
    k_in = -jnp.expm1(log_f)
    o_hg, hg_s = hgrn2_scan(heads(hg_q), heads(k_in), heads(hg_i), heads(log_f), hg_s0)
    o_hg = rms_norm(o_hg, hgrn_out_norm) * jax.nn.silu(heads(hg_g))

    fq = rms_norm(heads(fx_q), fox_q_norm)
    fk = rms_norm(heads(fx_k), fox_k_norm)
    fv = heads(fx_v)
    f_log = jax.nn.log_sigmoid(fx_f + fox_f_bias.astype(f32))
    sq = rms_norm(heads(sb_q), sb_q_norm)
    sk = rms_norm(heads(sb_k), sb_k_norm)
    sv = heads(sb_v)
    if past is None:
        kf, vf, lf, ks, vs = fk, fv, f_log, sk, sv
    else:
        kf = jnp.concatenate([fox_k_p, fk], axis=1)
        vf = jnp.concatenate([fox_v_p, fv], axis=1)
        lf = jnp.concatenate([fox_lf_p, f_log], axis=1)
        ks = jnp.concatenate([sb_k_p, sk], axis=1)
        vs = jnp.concatenate([sb_v_p, sv], axis=1)
    f_cum = jnp.cumsum(lf, axis=1)
    k_pos = jnp.arange(p_len + t)
    q_pos = p_len + jnp.arange(t)
    o_fox = sweep_query_blocks(lambda qb, fb, pb: fox_block(qb, kf, vf, fb, f_cum, pb, k_pos),
                               (fq, f_cum[:, p_len:]), q_pos)
    o_sb = sweep_query_blocks(lambda qb, pb: sb_block(qb, ks, vs, pb, k_pos), (sq,), q_pos)

    conv_out, dn_buf_new = causal_conv(dn_qkv, dn_buf, dn_conv_w)
    dq, dk, dv = jnp.split(conv_out, 3, axis=-1)
    dq = l2_norm(heads(dq)) * HEAD_DIM ** -0.5
    dk = l2_norm(heads(dk))
    beta = jax.nn.sigmoid(dn_b)
    g_dec = -jnp.exp(dn_a_log.astype(f32)) * jax.nn.softplus(dn_a + dn_dt_bias.astype(f32))
    o_dn, dn_s = gated_delta_scan(dq, dk, heads(dv), g_dec, beta, dn_s0)
    o_dn = rms_norm(o_dn, dn_out_norm) * jax.nn.silu(heads(dn_z))

    mix = jnp.concatenate([o_hg, rms_norm(o_fox, fox_out_norm), rms_norm(o_sb, sb_out_norm), o_dn], axis=2)
    mix = mix.reshape(b, t, D_MODEL).astype(x.dtype)
    x = x + jnp.einsum('bte,ed->btd', mix, w_out).astype(x.dtype)
    h2 = rms_norm(x, ln2_g).astype(x.dtype)
    u = jax.nn.relu(jnp.einsum('btd,df->btf', h2, w_up))
    x = x + jnp.einsum('btf,fd->btd', u * u, w_down).astype(x.dtype)
    return x, (fk, fv, f_log, sk, sv, hg_s, dn_s, dn_buf_new)


def setup_inputs(seed: int = 0) -> dict:
    key = jax.random.key(seed)
    keys = iter(jax.random.split(key, 48))

    def nrm(shape, scale):
        return jax.random.normal(next(keys), shape, jnp.float32) * scale

    n_pages = PAST_LEN // PAGE_SIZE
    n_used = DEC_BATCH * n_pages
    n_phys = n_used + max(1, n_used // 4)
    page_table = jax.random.permutation(next(keys), n_phys)[:n_used].reshape(DEC_BATCH, n_pages).astype(jnp.int32)
    kv_shape = (DEPTH, n_phys, PAGE_SIZE, N_HEADS, HEAD_DIM)
    st_shape = (DEPTH, DEC_BATCH, N_HEADS, HEAD_DIM, HEAD_DIM)
    dt = jnp.exp(jax.random.uniform(next(keys), (DEPTH, N_HEADS), jnp.float32, math.log(1e-3), math.log(1e-1)))
    return {
        'x_prompt': nrm((BATCH, SEQ, D_MODEL), 1.0),
        'x_sample': nrm((DEC_BATCH, DEC_SEQ, D_MODEL), 1.0),
        'cache_fox_k': nrm(kv_shape, 1.0),
        'cache_fox_v': nrm(kv_shape, 1.0),
        'cache_fox_logf': jax.nn.log_sigmoid(FOX_BIAS_MEAN + nrm(kv_shape[:-1], 1.0)),
        'cache_sb_k': nrm(kv_shape, 1.0),
        'cache_sb_v': nrm(kv_shape, 1.0),
        'state_hgrn': nrm(st_shape, 0.5),
        'state_dn': nrm(st_shape, 0.3),
        'state_dn_conv': nrm((DEPTH, DEC_BATCH, CONV_WIDTH - 1, 3 * GW), 1.0),
        'page_table': page_table,
        'hgrn_lb_param': nrm((DEPTH, GW), 1.0),
        'w_in': nrm((DEPTH, D_MODEL, D_IN), D_MODEL ** -0.5),
        'w_out': nrm((DEPTH, D_MODEL, D_MODEL), D_MODEL ** -0.5),
        'ln1_g': 1.0 + nrm((DEPTH, D_MODEL), 0.02),
        'ln2_g': 1.0 + nrm((DEPTH, D_MODEL), 0.02),
        'fox_f_bias': FOX_BIAS_MEAN + nrm((DEPTH, N_HEADS), 0.5),
        'fox_q_norm': 1.0 + nrm((DEPTH, HEAD_DIM), 0.02),
        'fox_k_norm': 1.0 + nrm((DEPTH, HEAD_DIM), 0.02),
        'sb_q_norm': 1.0 + nrm((DEPTH, HEAD_DIM), 0.02),
        'sb_k_norm': 1.0 + nrm((DEPTH, HEAD_DIM), 0.02),
        'hgrn_out_norm': 1.0 + nrm((DEPTH, HEAD_DIM), 0.02),
        'fox_out_norm': 1.0 + nrm((DEPTH, HEAD_DIM), 0.02),
        'sb_out_norm': 1.0 + nrm((DEPTH, HEAD_DIM), 0.02),
        'dn_out_norm': 1.0 + nrm((DEPTH, HEAD_DIM), 0.02),
        'dn_conv_w': nrm((DEPTH, CONV_WIDTH, 3 * GW), CONV_WIDTH ** -0.5),
        'dn_dt_bias': dt + jnp.log(-jnp.expm1(-dt)),
        'dn_a_log': jnp.log(jax.random.uniform(next(keys), (DEPTH, N_HEADS), jnp.float32, 1.0, 16.0)),
        'w_up': nrm((DEPTH, D_MODEL, D_FF), D_MODEL ** -0.5),
        'w_down': nrm((DEPTH, D_FF, D_MODEL), D_FF ** -0.5),
    }


def reference(x_prompt, x_sample, cache_fox_k, cache_fox_v, cache_fox_logf, cache_sb_k, cache_sb_v,
              state_hgrn, state_dn, state_dn_conv, page_table, hgrn_lb_param, w_in, w_out, ln1_g, ln2_g,
              fox_f_bias, fox_q_norm, fox_k_norm, sb_q_norm, sb_k_norm, hgrn_out_norm, fox_out_norm,
              sb_out_norm, dn_out_norm, dn_conv_w, dn_dt_bias, dn_a_log, w_up, w_down):
    lb_all = jnp.cumsum(jax.nn.softmax(hgrn_lb_param.astype(jnp.float32), axis=0), axis=0)
    lb_all = lb_all - lb_all[0:1]
    yp, ys = x_prompt, x_sample
    p_states, s_states = [], []
    for l in range(DEPTH):
        w_l = (lb_all[l], w_in[l], w_out[l], ln1_g[l], ln2_g[l], fox_f_bias[l], fox_q_norm[l],
               fox_k_norm[l], sb_q_norm[l], sb_k_norm[l], hgrn_out_norm[l], fox_out_norm[l],
               sb_out_norm[l], dn_out_norm[l], dn_conv_w[l], dn_dt_bias[l], dn_a_log[l], w_up[l], w_down[l])
        yp, st_p = trunk_layer(yp, None, *w_l)
        past = (gather_pages(cache_fox_k[l], page_table), gather_pages(cache_fox_v[l], page_table),
                gather_pages(cache_fox_logf[l], page_table), gather_pages(cache_sb_k[l], page_table),
                gather_pages(cache_sb_v[l], page_table), state_hgrn[l], state_dn[l], state_dn_conv[l])
        ys, st_s = trunk_layer(ys, past, *w_l)
        p_states.append(st_p)
        s_states.append(st_s)
    p = [jnp.stack([st[i] for st in p_states]) for i in range(8)]
    s = [jnp.stack([st[i] for st in s_states]) for i in range(8)]
    return (yp, ys, p[0], p[1], p[2], p[3], p[4], p[5], p[6], p[7],
            s[0], s[1], s[2], s[3], s[4], s[5], s[6], s[7])
```

```python
import functools

import jax
import jax.numpy as jnp
from jax import lax
from jax.experimental import pallas as pl
from jax.experimental.pallas import tpu as pltpu

F32 = jnp.float32
BF16 = jnp.bfloat16

HEAD_DIM = 64
N_HEADS = 4
GROUP_WIDTH = N_HEADS * HEAD_DIM
N_SEGMENTS = 14
SMALL_WIDTH = 128
CONV_WIDTH = 4
NORM_EPS = 1e-6
NEG_BIG = -1e30
QK_SCALE = HEAD_DIM ** -0.5
VMEM_LIMIT = 56 * 1024 * 1024

LANE_FLOG = 0
LANE_BETA = 4
LANE_GDEC = 8
SMALL_ROWS_T = 16


def _iota(shape, dim):
    return lax.broadcasted_iota(jnp.int32, shape, dim)


def _dot(a, b):
    return jnp.dot(a, b, preferred_element_type=F32)


def _dot_nt(a, b):
    return lax.dot_general(a, b, (((1,), (1,)), ((), ())), preferred_element_type=F32)


def _split3(x):
    hi = x.astype(BF16)
    r = x - hi.astype(F32)
    mid = r.astype(BF16)
    lo = (r - mid.astype(F32)).astype(BF16)
    return hi, mid, lo


def _dot_xc(x, c, parts=3):
    ps = _split3(x)[:parts]
    out = _dot(ps[0], c)
    for p in ps[1:]:
        out = out + _dot(p, c)
    return out


def _dot_cx(c, x, parts=3):
    ps = _split3(x)[:parts]
    out = _dot(c, ps[0])
    for p in ps[1:]:
        out = out + _dot(c, p)
    return out


def _dot_f32(a, b):
    ah = a.astype(BF16)
    al = (a - ah.astype(F32)).astype(BF16)
    bh = b.astype(BF16)
    bl = (b - bh.astype(F32)).astype(BF16)
    return _dot(ah, bh) + _dot(ah, bl) + _dot(al, bh)


def _head_ones(n=GROUP_WIDTH):
    return (_iota((n, n), 0) // HEAD_DIM == _iota((n, n), 1) // HEAD_DIM).astype(BF16)


def _head_sum(x, ones):
    return _dot_xc(x, ones)


def _head_rms(x, ones, gain):
    ms = _head_sum(x * x, ones) * (1.0 / HEAD_DIM)
    return x * lax.rsqrt(ms + NORM_EPS) * gain


def _log_sigmoid(x):
    return jnp.minimum(x, 0.0) - jnp.log1p(jnp.exp(-jnp.abs(x)))


def _softplus(x):
    return jnp.maximum(x, 0.0) + jnp.log1p(jnp.exp(-jnp.abs(x)))


def _sigmoid(x):
    return 1.0 / (1.0 + jnp.exp(-x))


def _silu(x):
    return x * _sigmoid(x)


def _inproj_kernel(x_ref, g1_ref, w_ref, lbp_ref, gains_ref, sp_ref, *refs, layer, depth, tiles_per_seq,
                   with_time):
    (hq_ref, hlf_ref, hk_ref, hi_ref, hg_ref, fq_ref, fk_ref, fv_ref, sq_ref, sk_ref, sv_ref,
     dqkv_ref, dz_ref, small_ref) = refs[:14]
    x = x_ref[...]
    h = (x * lax.rsqrt(jnp.mean(x * x, axis=-1, keepdims=True) + NORM_EPS) * g1_ref[...]).astype(BF16)

    def seg(j, width=GROUP_WIDTH):
        return _dot(h, w_ref[:, j * GROUP_WIDTH:j * GROUP_WIDTH + width])

    ones = _head_ones()

    rows = [lbp_ref[i:i + 1, :] for i in range(depth)]
    mx = functools.reduce(jnp.maximum, rows)
    es = [jnp.exp(r - mx) for r in rows]
    lb = sum(es[1:layer + 1], jnp.zeros_like(mx)) / sum(es)
    hq_ref[...] = seg(0)
    hf = seg(1)
    a = jnp.log(lb)
    b = jnp.log1p(-lb) + _log_sigmoid(hf)
    hi = jnp.maximum(a, b)
    lo = jnp.minimum(a, b)
    hlf_ref[...] = hi + jnp.log1p(jnp.exp(lo - hi))
    hk_ref[...] = (1.0 - lb) * _sigmoid(-hf)
    hi_ref[...] = seg(2)
    hg_ref[...] = _silu(seg(3))

    fq_ref[...] = _head_rms(seg(4), ones, gains_ref[0:1, :]) * QK_SCALE
    sq_ref[...] = _head_rms(seg(7), ones, gains_ref[2:3, :]) * QK_SCALE
    kv = (_head_rms(seg(5), ones, gains_ref[1:2, :]), seg(6), _head_rms(seg(8), ones, gains_ref[3:4, :]), seg(9))
    for ref, val in zip((fk_ref, fv_ref, sk_ref, sv_ref), kv):
        if with_time:
            ref[0] = val.T
        else:
            ref[...] = val

    dqkv_ref[...] = seg(10, 3 * GROUP_WIDTH)
    dz_ref[...] = _silu(seg(13))

    s = _dot(h, w_ref[:, N_SEGMENTS * GROUP_WIDTH:])
    lane = _iota(s.shape, 1)
    f_log = _log_sigmoid(s + sp_ref[0:1, :])
    beta = _sigmoid(s)
    g_dec = -jnp.exp(sp_ref[2:3, :]) * _softplus(s + sp_ref[1:2, :])
    small = jnp.where(lane < LANE_BETA, f_log, jnp.where(lane < LANE_GDEC, beta, g_dec))
    small_ref[...] = small

    if with_time:
        cum_ref, small_t_ref, cum_t_ref, carry_ref = refs[14:]
        tm = s.shape[0]

        @pl.when(pl.program_id(0) % tiles_per_seq == 0)
        def _():
            carry_ref[...] = jnp.zeros_like(carry_ref)

        tril = (_iota((tm, tm), 1) <= _iota((tm, tm), 0)).astype(BF16)
        cum = _dot_cx(tril, small) + carry_ref[0:1, :]
        cum_ref[...] = cum
        carry_ref[...] = jnp.broadcast_to(cum[tm - 1:tm, :], carry_ref.shape)
        small_t_ref[0] = small.T[:SMALL_ROWS_T, :]
        cum_t_ref[0] = cum.T[:SMALL_ROWS_T, :]


def _inproj(x2, g1, w_re, lbp, gains, sp, *, layer, tm, rows_per_seq, with_time):
    m, d = x2.shape
    n_w = w_re.shape[1]
    depth = lbp.shape[0]
    grid = (m // tm,)
    row = lambda i: (i, 0)
    const = lambda i: (0, 0)
    tps = rows_per_seq // tm
    nseq = m // rows_per_seq
    seg_shape = jax.ShapeDtypeStruct((m, GROUP_WIDTH), F32)
    seg_spec = pl.BlockSpec((tm, GROUP_WIDTH), row)
    out_shape = [seg_shape] * 11 + [jax.ShapeDtypeStruct((m, 3 * GROUP_WIDTH), F32), seg_shape,
                                    jax.ShapeDtypeStruct((m, SMALL_WIDTH), F32)]
    out_specs = [seg_spec] * 11 + [pl.BlockSpec((tm, 3 * GROUP_WIDTH), row), seg_spec,
                                   pl.BlockSpec((tm, SMALL_WIDTH), row)]
    scratch = []
    if with_time:
        def by_time(rows):
            return (jax.ShapeDtypeStruct((nseq, rows, rows_per_seq), F32),
                    pl.BlockSpec((1, rows, tm), lambda i: (i // tps, 0, i % tps)))
        for idx in (6, 7, 9, 10):
            out_shape[idx], out_specs[idx] = by_time(GROUP_WIDTH)
        out_shape += [jax.ShapeDtypeStruct((m, SMALL_WIDTH), F32)]
        out_specs += [pl.BlockSpec((tm, SMALL_WIDTH), row)]
        for _ in range(2):
            sh, sp_ = by_time(SMALL_ROWS_T)
            out_shape.append(sh)
            out_specs.append(sp_)
        scratch = [pltpu.VMEM((8, SMALL_WIDTH), F32)]
    kern = functools.partial(_inproj_kernel, layer=layer, depth=depth, tiles_per_seq=tps,
                             with_time=with_time)
    return pl.pallas_call(
        kern, grid=grid,
        in_specs=[pl.BlockSpec((tm, d), row), pl.BlockSpec((1, d), const), pl.BlockSpec((d, n_w), const),
                  pl.BlockSpec(lbp.shape, const), pl.BlockSpec(gains.shape, const), pl.BlockSpec(sp.shape, const)],
        out_specs=out_specs, out_shape=out_shape, scratch_shapes=scratch,
        compiler_params=pltpu.CompilerParams(dimension_semantics=("arbitrary",), vmem_limit_bytes=VMEM_LIMIT),
        name="inproj_time" if with_time else "inproj_step",
    )(x2, g1, w_re, lbp, gains, sp)


HGRN_SUB = 16


def _hgrn_kernel(q_ref, lf_ref, k_ref, v_ref, o_ref, st_ref, s_ref, oi_ref, *, tt):
    t = pl.program_id(1)

    @pl.when(t == 0)
    def _():
        s_ref[...] = jnp.zeros_like(s_ref)

    q = q_ref[0]
    lf = lf_ref[0]
    kin = k_ref[0]
    v = v_ref[0]
    r = _iota((tt, tt), 0)
    c = _iota((tt, tt), 1)
    same = (r // HGRN_SUB) == (c // HGRN_SUB)
    g = _dot_cx(jnp.where(same & (c <= r), 1.0, 0.0).astype(BF16), lf)
    gl = _dot_cx(jnp.where(same, 1.0, 0.0).astype(BF16), lf)
    qg = q * jnp.exp(g)
    kg = kin * jnp.exp(gl - g)

    ones = _head_ones()
    rowmod = _iota((tt, GROUP_WIDTH), 0) % HGRN_SUB
    o = jnp.zeros((tt, GROUP_WIDTH), F32)
    for d in range(HGRN_SUB):
        if d == 0:
            kd, gd, vd = kin, g, v
        else:
            kd = pltpu.roll(kin, d, 0)
            gd = pltpu.roll(g, d, 0)
            vd = pltpu.roll(v, d, 0)
        e = jnp.where(rowmod >= d, g - gd, NEG_BIG)
        p = q * kd * jnp.exp(e)
        o = o + _dot_xc(p, ones, parts=2) * vd

    v_t = v.T
    for i in range(tt // HGRN_SUB):
        rs = slice(i * HGRN_SUB, (i + 1) * HGRN_SUB)
        for h in range(N_HEADS):
            cs = slice(h * HEAD_DIM, (h + 1) * HEAD_DIM)
            s = s_ref[h]
            oi_ref[rs, cs] = _dot_nt(qg[rs, cs].astype(BF16), s.astype(BF16))
            dec = jnp.exp(gl[i * HGRN_SUB:i * HGRN_SUB + 1, cs])
            s_ref[h] = dec * s + _dot(v_t[cs, rs].astype(BF16), kg[rs, cs].astype(BF16))
    o_ref[0] = o + oi_ref[...]

    @pl.when(t == pl.num_programs(1) - 1)
    def _():
        st_ref[0] = s_ref[...]


def _hgrn_prompt(hq, hlf, hk, hi, *, tt):
    b, t, _ = hq.shape
    blk = pl.BlockSpec((1, tt, GROUP_WIDTH), lambda i, j: (i, j, 0))
    return pl.pallas_call(
        functools.partial(_hgrn_kernel, tt=tt), grid=(b, t // tt),
        in_specs=[blk] * 4,
        out_specs=[blk, pl.BlockSpec((1, N_HEADS, HEAD_DIM, HEAD_DIM), lambda i, j: (i, 0, 0, 0))],
        out_shape=[jax.ShapeDtypeStruct((b, t, GROUP_WIDTH), F32),
                   jax.ShapeDtypeStruct((b, N_HEADS, HEAD_DIM, HEAD_DIM), F32)],
        scratch_shapes=[pltpu.VMEM((N_HEADS, HEAD_DIM, HEAD_DIM), F32), pltpu.VMEM((tt, GROUP_WIDTH), F32)],
        compiler_params=pltpu.CompilerParams(dimension_semantics=("arbitrary", "arbitrary"),
                                             vmem_limit_bytes=VMEM_LIMIT),
        name="hgrn_prompt",
    )(hq, hlf, hk, hi)


def _fox_kernel(q_ref, k_ref, v_ref, fq_ref, fk_ref, o_ref, m_ref, l_ref, acc_ref, *, tq):
    qi = pl.program_id(1)
    j = pl.program_id(2)

    @pl.when(j == 0)
    def _():
        m_ref[...] = jnp.full_like(m_ref, NEG_BIG)
        l_ref[...] = jnp.zeros_like(l_ref)
        acc_ref[...] = jnp.zeros_like(acc_ref)

    def step(masked):
        q = q_ref[0].astype(BF16)
        k = k_ref[0].astype(BF16)
        v = v_ref[0].astype(BF16)
        fq = fq_ref[0]
        fk = fk_ref[0]
        if masked:
            keep = _iota((tq, tq), 1) <= _iota((tq, tq), 0)
        for h in range(N_HEADS):
            cs = slice(h * HEAD_DIM, (h + 1) * HEAD_DIM)
            s = _dot(q[:, cs], k[cs, :]) + (fq[:, LANE_FLOG + h:LANE_FLOG + h + 1]
                                            - fk[LANE_FLOG + h:LANE_FLOG + h + 1, :])
            if masked:
                s = jnp.where(keep, s, NEG_BIG)
            m_prev = m_ref[h]
            m_new = jnp.maximum(m_prev, jnp.max(s, axis=-1, keepdims=True))
            alpha = jnp.exp(m_prev - m_new)
            p = jnp.exp(s - m_new[:, 0:1])
            l_ref[h] = alpha * l_ref[h] + jnp.sum(p, axis=-1, keepdims=True)
            m_ref[h] = m_new
            acc_ref[:, cs] = alpha[:, 0:1] * acc_ref[:, cs] + _dot_nt(p.astype(BF16), v[cs, :])

    @pl.when(j < qi)
    def _():
        step(False)

    @pl.when(j == qi)
    def _():
        step(True)
        for h in range(N_HEADS):
            cs = slice(h * HEAD_DIM, (h + 1) * HEAD_DIM)
            o_ref[0, :, cs] = acc_ref[:, cs] / l_ref[h][:, 0:1]


def _fox_prompt(fq, fk, fv, cum, cum_t, *, tq):
    b, t, _ = fq.shape
    n = t // tq
    qspec = pl.BlockSpec((1, tq, GROUP_WIDTH), lambda i, qi, j: (i, qi, 0))
    kspec = pl.BlockSpec((1, GROUP_WIDTH, tq), lambda i, qi, j: (i, 0, jnp.minimum(j, qi)))
    return pl.pallas_call(
        functools.partial(_fox_kernel, tq=tq), grid=(b, n, n),
        in_specs=[qspec, kspec, kspec,
                  pl.BlockSpec((1, tq, SMALL_WIDTH), lambda i, qi, j: (i, qi, 0)),
                  pl.BlockSpec((1, SMALL_ROWS_T, tq), lambda i, qi, j: (i, 0, jnp.minimum(j, qi)))],
        out_specs=qspec,
        out_shape=jax.ShapeDtypeStruct((b, t, GROUP_WIDTH), F32),
        scratch_shapes=[pltpu.VMEM((N_HEADS, tq, 128), F32), pltpu.VMEM((N_HEADS, tq, 128), F32),
                        pltpu.VMEM((tq, GROUP_WIDTH), F32)],
        compiler_params=pltpu.CompilerParams(dimension_semantics=("arbitrary", "arbitrary", "arbitrary"),
                                             vmem_limit_bytes=VMEM_LIMIT),
        name="fox_prompt",
    )(fq, fk, fv, cum, cum_t)


def _sb_kernel(q_ref, k_ref, v_ref, o_ref, carry_ref, acc_ref, *, tq):
    qi = pl.program_id(1)
    j = pl.program_id(2)

    @pl.when(j == 0)
    def _():
        carry_ref[...] = jnp.zeros_like(carry_ref)
        acc_ref[...] = jnp.zeros_like(acc_ref)

    def step(masked):
        q = q_ref[0].astype(BF16)
        k = k_ref[0].astype(BF16)
        v = v_ref[0].astype(BF16)
        r = _iota((tq, tq), 0)
        c = _iota((tq, tq), 1)
        suffix = jnp.where(c <= r, 1.0, 0.0).astype(BF16)
        for h in range(N_HEADS):
            cs = slice(h * HEAD_DIM, (h + 1) * HEAD_DIM)
            z = _dot(q[:, cs], k[cs, :])
            ls = _log_sigmoid(-z)
            if masked:
                ls = jnp.where(c < r, ls, 0.0)
            cum = _dot_xc(ls, suffix, parts=2) + carry_ref[h][:, 0:1]
            e = z + cum
            if masked:
                e = jnp.where(c < r, e, NEG_BIG)
            w = jnp.exp(e)
            acc_ref[:, cs] = acc_ref[:, cs] + _dot_nt(w.astype(BF16), v[cs, :])
            carry_ref[h] = jnp.broadcast_to(cum[:, 0:1], carry_ref.shape[1:])

    @pl.when(j == 0)
    def _():
        step(True)

    @pl.when((j > 0) & (j <= qi))
    def _():
        step(False)

    @pl.when(j == qi)
    def _():
        o_ref[0] = acc_ref[...]


def _sb_prompt(sq, sk, sv, *, tq):
    b, t, _ = sq.shape
    n = t // tq
    qspec = pl.BlockSpec((1, tq, GROUP_WIDTH), lambda i, qi, j: (i, qi, 0))
    kspec = pl.BlockSpec((1, GROUP_WIDTH, tq), lambda i, qi, j: (i, 0, jnp.maximum(qi - j, 0)))
    return pl.pallas_call(
        functools.partial(_sb_kernel, tq=tq), grid=(b, n, n),
        in_specs=[qspec, kspec, kspec],
        out_specs=qspec,
        out_shape=jax.ShapeDtypeStruct((b, t, GROUP_WIDTH), F32),
        scratch_shapes=[pltpu.VMEM((N_HEADS, tq, 128), F32), pltpu.VMEM((tq, GROUP_WIDTH), F32)],
        compiler_params=pltpu.CompilerParams(dimension_semantics=("arbitrary", "arbitrary", "arbitrary"),
                                             vmem_limit_bytes=VMEM_LIMIT),
        name="sb_prompt",
    )(sq, sk, sv)


GDN_CHUNK = 64
GDN_BASE = 16


def _unit_lower_inverse(lmat, ii, jj):
    n = jnp.where(ii // GDN_BASE == jj // GDN_BASE, -lmat, 0.0)
    t = jnp.where(ii == jj, 1.0, 0.0) + n
    p = n
    steps = GDN_BASE.bit_length() - 2
    for _ in range(steps):
        p = _dot_f32(p, p)
        t = t + _dot_f32(t, p)
    b = GDN_BASE
    while b < GDN_CHUNK:
        off = jnp.where((ii // (2 * b) == jj // (2 * b)) & (ii // b != jj // b), lmat, 0.0)
        t = t - _dot_f32(_dot_f32(t, off), t)
        b *= 2
    return t


def _gdn_kernel(x_ref, w_ref, sm_ref, smt_ref, o_ref, st_ref, s_ref, ext_ref, oc_ref, *, tt):
    t = pl.program_id(1)

    @pl.when(t == 0)
    def _():
        s_ref[...] = jnp.zeros_like(s_ref)
        ext_ref[0:8, :] = jnp.zeros((8, 3 * GROUP_WIDTH), F32)

    @pl.when(t > 0)
    def _():
        ext_ref[0:8, :] = ext_ref[tt:tt + 8, :]

    ext_ref[8:8 + tt, :] = x_ref[0]
    y = ext_ref[pl.ds(8 - (CONV_WIDTH - 1), tt), :] * w_ref[0:1, :]
    for jw in range(1, CONV_WIDTH):
        y = y + ext_ref[pl.ds(8 - (CONV_WIDTH - 1) + jw, tt), :] * w_ref[jw:jw + 1, :]
    y = _silu(y)
    ones = _head_ones()
    q = y[:, :GROUP_WIDTH]
    k = y[:, GROUP_WIDTH:2 * GROUP_WIDTH]
    v = y[:, 2 * GROUP_WIDTH:]
    q = q * lax.rsqrt(_head_sum(q * q, ones) + NORM_EPS) * QK_SCALE
    k = k * lax.rsqrt(_head_sum(k * k, ones) + NORM_EPS)
    sm = sm_ref[0]
    smt = smt_ref[0]
    r = _iota((tt, tt), 0)
    c = _iota((tt, tt), 1)
    same = (r // GDN_CHUNK) == (c // GDN_CHUNK)
    gc_col = _dot_cx(jnp.where(same & (c <= r), 1.0, 0.0).astype(BF16), sm)
    gc_row = _dot_xc(smt, jnp.where(same & (r <= c), 1.0, 0.0).astype(BF16))
    k_t = k.T
    ii = _iota((GDN_CHUNK, GDN_CHUNK), 0)
    jj = _iota((GDN_CHUNK, GDN_CHUNK), 1)
    for ci in range(tt // GDN_CHUNK):
        rs = slice(ci * GDN_CHUNK, (ci + 1) * GDN_CHUNK)
        for h in range(N_HEADS):
            cs = slice(h * HEAD_DIM, (h + 1) * HEAD_DIM)
            gcol = gc_col[rs, LANE_GDEC + h:LANE_GDEC + h + 1]
            grow = gc_row[LANE_GDEC + h:LANE_GDEC + h + 1, rs]
            beta = sm[rs, LANE_BETA + h:LANE_BETA + h + 1]
            dec = jnp.exp(jnp.where(ii >= jj, gcol - grow, NEG_BIG))
            qh = q[rs, cs]
            kh = k[rs, cs]
            vh = v[rs, cs]
            kth = k_t[cs, rs].astype(BF16)
            kb = kh * beta
            lmat = jnp.where(ii > jj, _dot(kb.astype(BF16), kth) * dec, 0.0)
            tinv = _unit_lower_inverse(lmat, ii, jj)
            eg = jnp.exp(gcol)
            sol = _dot_f32(tinv, jnp.concatenate([vh * beta, kb * eg], axis=1))
            u = sol[:, :HEAD_DIM]
            w = sol[:, HEAD_DIM:]
            s = s_ref[h]
            sb = s.astype(BF16)
            v_new = u - _dot(w.astype(BF16), sb)
            attn = _dot(qh.astype(BF16), kth) * dec
            oc_ref[rs, cs] = _dot((qh * eg).astype(BF16), sb) + _dot(attn.astype(BF16), v_new.astype(BF16))
            gl = grow[:, GDN_CHUNK - 1:GDN_CHUNK]
            kd_t = k_t[cs, rs] * jnp.exp(gl - grow)
            s_ref[h] = jnp.exp(gl) * s + _dot(kd_t.astype(BF16), v_new.astype(BF16))
    o_ref[0] = oc_ref[...]

    @pl.when(t == pl.num_programs(1) - 1)
    def _():
        st_ref[0] = s_ref[...]


def _gdn_prompt(dqkv, conv_w, small, small_t, *, tt):
    b, t, _ = dqkv.shape
    return pl.pallas_call(
        functools.partial(_gdn_kernel, tt=tt), grid=(b, t // tt),
        in_specs=[pl.BlockSpec((1, tt, 3 * GROUP_WIDTH), lambda i, j: (i, j, 0)),
                  pl.BlockSpec(conv_w.shape, lambda i, j: (0, 0)),
                  pl.BlockSpec((1, tt, SMALL_WIDTH), lambda i, j: (i, j, 0)),
                  pl.BlockSpec((1, SMALL_ROWS_T, tt), lambda i, j: (i, 0, j))],
        out_specs=[pl.BlockSpec((1, tt, GROUP_WIDTH), lambda i, j: (i, j, 0)),
                   pl.BlockSpec((1, N_HEADS, HEAD_DIM, HEAD_DIM), lambda i, j: (i, 0, 0, 0))],
        out_shape=[jax.ShapeDtypeStruct((b, t, GROUP_WIDTH), F32),
                   jax.ShapeDtypeStruct((b, N_HEADS, HEAD_DIM, HEAD_DIM), F32)],
        scratch_shapes=[pltpu.VMEM((N_HEADS, HEAD_DIM, HEAD_DIM), F32),
                        pltpu.VMEM((tt + 8, 3 * GROUP_WIDTH), F32),
                        pltpu.VMEM((tt, GROUP_WIDTH), F32)],
        compiler_params=pltpu.CompilerParams(dimension_semantics=("arbitrary", "arbitrary"),
                                             vmem_limit_bytes=VMEM_LIMIT),
        name="gdn_prompt",
    )(dqkv, conv_w, small, small_t)


def _outproj_kernel(x_ref, ohg_ref, hg_ref, ofx_ref, osb_ref, odn_ref, dz_ref, gains_ref, w_ref, y_ref):
    ones = _head_ones()
    parts = [
        _head_rms(ohg_ref[...], ones, gains_ref[0:1, :]) * hg_ref[...],
        _head_rms(ofx_ref[...], ones, gains_ref[1:2, :]),
        _head_rms(osb_ref[...], ones, gains_ref[2:3, :]),
        _head_rms(odn_ref[...], ones, gains_ref[3:4, :]) * dz_ref[...],
    ]
    y = x_ref[...]
    for gidx, p in enumerate(parts):
        y = y + _dot(p.astype(BF16), w_ref[gidx * GROUP_WIDTH:(gidx + 1) * GROUP_WIDTH, :])
    y_ref[...] = y


def _outproj(x2, ohg, hgate, ofx, osb, odn, dz, gains, w_out, *, tm):
    m, d = x2.shape
    row = lambda i: (i, 0)
    const = lambda i: (0, 0)
    seg = pl.BlockSpec((tm, GROUP_WIDTH), row)
    return pl.pallas_call(
        _outproj_kernel, grid=(m // tm,),
        in_specs=[pl.BlockSpec((tm, d), row)] + [seg] * 6 + [pl.BlockSpec(gains.shape, const),
                                                            pl.BlockSpec(w_out.shape, const)],
        out_specs=pl.BlockSpec((tm, d), row),
        out_shape=jax.ShapeDtypeStruct((m, d), F32),
        compiler_params=pltpu.CompilerParams(dimension_semantics=("arbitrary",), vmem_limit_bytes=VMEM_LIMIT),
        name="outproj",
    )(x2, ohg, hgate, ofx, osb, odn, dz, gains, w_out)


def _mlp_kernel(x_ref, g2_ref, wu_ref, wd_ref, y_ref, h_ref, acc_ref):
    f = pl.program_id(1)

    @pl.when(f == 0)
    def _():
        x = x_ref[...]
        h_ref[...] = (x * lax.rsqrt(jnp.mean(x * x, axis=-1, keepdims=True) + NORM_EPS) * g2_ref[...]).astype(BF16)
        acc_ref[...] = x

    u = jnp.maximum(_dot(h_ref[...], wu_ref[...]), 0.0)
    acc_ref[...] += _dot((u * u).astype(BF16), wd_ref[...])

    @pl.when(f == pl.num_programs(1) - 1)
    def _():
        y_ref[...] = acc_ref[...]


def _mlp(x2, g2, w_up, w_down, *, tm, tf):
    m, d = x2.shape
    dff = w_up.shape[1]
    return pl.pallas_call(
        _mlp_kernel, grid=(m // tm, dff // tf),
        in_specs=[pl.BlockSpec((tm, d), lambda i, f: (i, 0)), pl.BlockSpec((1, d), lambda i, f: (0, 0)),
                  pl.BlockSpec((d, tf), lambda i, f: (0, f)), pl.BlockSpec((tf, d), lambda i, f: (f, 0))],
        out_specs=pl.BlockSpec((tm, d), lambda i, f: (i, 0)),
        out_shape=jax.ShapeDtypeStruct((m, d), F32),
        scratch_shapes=[pltpu.VMEM((tm, d), BF16), pltpu.VMEM((tm, d), F32)],
        compiler_params=pltpu.CompilerParams(dimension_semantics=("arbitrary", "arbitrary"),
                                             vmem_limit_bytes=VMEM_LIMIT),
        name="mlp",
    )(x2, g2, w_up, w_down)


PAGES_PER_STEP = 8


def _head_rows(row):
    x = jnp.broadcast_to(row, (8, GROUP_WIDTH))
    return jnp.where(_iota((8, GROUP_WIDTH), 1) // HEAD_DIM == _iota((8, GROUP_WIDTH), 0), x, 0.0)


def _dec_attn_kernel(pt_ref, q_ref, kn_ref, vn_ref, sn_ref, *refs, fox, pp, page):
    k_refs = refs[:pp]
    v_refs = refs[pp:2 * pp]
    rest = refs[2 * pp:]
    if fox:
        lf_refs = rest[:pp]
        rest = rest[pp:]
    o_ref, m_ref, l_ref, acc_ref, carry_ref = rest
    j = pl.program_id(1)
    qb = _head_rows(q_ref[0]).astype(BF16)
    r = _iota((page, page), 0)
    c = _iota((page, page), 1)
    later = jnp.where(r > c, 1.0, 0.0).astype(BF16)

    @pl.when(j == 0)
    def _():
        if fox:
            kn = jnp.broadcast_to(kn_ref[0], (8, GROUP_WIDTH)).astype(BF16)
            m_ref[...] = jnp.broadcast_to(_dot_nt(qb, kn)[:, 0:1], m_ref.shape)
            l_ref[...] = jnp.ones_like(l_ref)
            acc_ref[...] = jnp.broadcast_to(vn_ref[0], acc_ref.shape).astype(BF16).astype(F32)
            sn = jnp.broadcast_to(sn_ref[0], (8, SMALL_WIDTH))
            lane = _iota((8, SMALL_WIDTH), 1)
            row = _iota((8, SMALL_WIDTH), 0)
            lf_new = jnp.sum(jnp.where(lane == row + LANE_FLOG, sn, 0.0), axis=-1, keepdims=True)
            carry_ref[...] = jnp.broadcast_to(jnp.where(_iota((8, 1), 0) < N_HEADS, lf_new, 0.0), carry_ref.shape)
        else:
            acc_ref[...] = jnp.zeros_like(acc_ref)
            carry_ref[...] = jnp.zeros_like(carry_ref)

    for i in range(pp):
        k = k_refs[i][0].astype(BF16)
        v = v_refs[i][0].astype(BF16)
        z = _dot(qb, k)
        carry = carry_ref[:, 0:1]
        if fox:
            lf = jnp.concatenate([lf_refs[i][0], jnp.zeros((8 - N_HEADS, page), F32)], axis=0)
            s = z + _dot_xc(lf, later) + carry
            carry_ref[...] = jnp.broadcast_to(carry + jnp.sum(lf, axis=-1, keepdims=True), carry_ref.shape)
            m_prev = m_ref[:, 0:1]
            m_new = jnp.maximum(m_prev, jnp.max(s, axis=-1, keepdims=True))
            alpha = jnp.exp(m_prev - m_new)
            p = jnp.exp(s - m_new)
            l_ref[...] = jnp.broadcast_to(alpha * l_ref[:, 0:1] + jnp.sum(p, axis=-1, keepdims=True), l_ref.shape)
            m_ref[...] = jnp.broadcast_to(m_new, m_ref.shape)
            acc_ref[...] = alpha * acc_ref[...] + _dot_nt(p.astype(BF16), v)
        else:
            ls = _log_sigmoid(-z)
            w = jnp.exp(z + ls + _dot_xc(ls, later) + carry)
            carry_ref[...] = jnp.broadcast_to(carry + jnp.sum(ls, axis=-1, keepdims=True), carry_ref.shape)
            acc_ref[...] = acc_ref[...] + _dot_nt(w.astype(BF16), v)

    @pl.when(j == pl.num_programs(1) - 1)
    def _():
        acc = acc_ref[...]
        if fox:
            acc = acc / l_ref[:, 0:1]
        own = _iota((8, GROUP_WIDTH), 1) // HEAD_DIM == _iota((8, GROUP_WIDTH), 0)
        o_ref[0] = jnp.sum(jnp.where(own, acc, 0.0), axis=0, keepdims=True)


def _dec_attn(page_table, q, k_new, v_new, small, cache_k, cache_v, cache_lf_t, *, layer, fox):
    nb = q.shape[0]
    n_pages = page_table.shape[1]
    page = cache_k.shape[3]
    pp = PAGES_PER_STEP
    row3 = lambda a: a.reshape(nb, 1, a.shape[-1])
    rspec = lambda w: pl.BlockSpec((1, 1, w), lambda b, j, pt: (b, 0, 0))

    def page_map(i):
        return lambda b, j, pt: (layer, pt[b, n_pages - 1 - (j * pp + i)], 0, 0)

    in_specs = [rspec(GROUP_WIDTH)] * 3 + [rspec(SMALL_WIDTH)]
    in_specs += [pl.BlockSpec((None, 1, GROUP_WIDTH, page), page_map(i)) for i in range(pp)] * 2
    args = [row3(q), row3(k_new), row3(v_new), row3(small)] + [cache_k] * pp + [cache_v] * pp
    if fox:
        in_specs += [pl.BlockSpec((None, 1, N_HEADS, page), page_map(i)) for i in range(pp)]
        args += [cache_lf_t] * pp
    grid_spec = pltpu.PrefetchScalarGridSpec(
        num_scalar_prefetch=1, grid=(nb, n_pages // pp), in_specs=in_specs,
        out_specs=pl.BlockSpec((1, 1, GROUP_WIDTH), lambda b, j, pt: (b, 0, 0)),
        scratch_shapes=[pltpu.VMEM((8, 128), F32), pltpu.VMEM((8, 128), F32), pltpu.VMEM((8, GROUP_WIDTH), F32),
                        pltpu.VMEM((8, 128), F32)])
    out = pl.pallas_call(
        functools.partial(_dec_attn_kernel, fox=fox, pp=pp, page=page), grid_spec=grid_spec,
        out_shape=jax.ShapeDtypeStruct((nb, 1, GROUP_WIDTH), F32),
        compiler_params=pltpu.CompilerParams(dimension_semantics=("arbitrary", "arbitrary"),
                                             vmem_limit_bytes=VMEM_LIMIT),
        name="fox_step" if fox else "sb_step",
    )(page_table, *args)
    return out.reshape(nb, GROUP_WIDTH)


def _column(row, eye):
    return jnp.sum(eye * row, axis=1, keepdims=True)


def _rec_step_kernel(hq_ref, hlf_ref, hk_ref, hi_ref, dx_ref, sm_ref, w_ref, shg_ref, sdn_ref, buf_ref,
                     ohg_ref, odn_ref, shg_o_ref, sdn_o_ref, buf_o_ref):
    eye = jnp.where(_iota((HEAD_DIM, HEAD_DIM), 0) == _iota((HEAD_DIM, HEAD_DIM), 1), 1.0, 0.0)
    hq = hq_ref[0]
    hlf = hlf_ref[0]
    hk = hk_ref[0]
    hv = hi_ref[0]
    sm = sm_ref[0]
    buf = buf_ref[0]
    x_new = dx_ref[0]
    y = x_new * w_ref[CONV_WIDTH - 1:CONV_WIDTH, :]
    for jw in range(CONV_WIDTH - 1):
        y = y + buf[jw:jw + 1, :] * w_ref[jw:jw + 1, :]
    y = _silu(y)
    buf_o_ref[0] = jnp.concatenate([buf[1:CONV_WIDTH - 1, :], x_new], axis=0)
    for h in range(N_HEADS):
        cs = slice(h * HEAD_DIM, (h + 1) * HEAD_DIM)
        s = shg_ref[0, h]
        s = _column(jnp.exp(hlf[:, cs]), eye) * s + _column(hk[:, cs], eye) * hv[:, cs]
        shg_o_ref[0, h] = s
        ohg_ref[0, :, cs] = jnp.sum(_column(hq[:, cs], eye) * s, axis=0, keepdims=True)
        q = y[:, cs]
        k = y[:, GROUP_WIDTH + h * HEAD_DIM:GROUP_WIDTH + (h + 1) * HEAD_DIM]
        v = y[:, 2 * GROUP_WIDTH + h * HEAD_DIM:2 * GROUP_WIDTH + (h + 1) * HEAD_DIM]
        q = q * lax.rsqrt(jnp.sum(q * q, axis=-1, keepdims=True) + NORM_EPS) * QK_SCALE
        k = k * lax.rsqrt(jnp.sum(k * k, axis=-1, keepdims=True) + NORM_EPS)
        beta = sm[:, LANE_BETA + h:LANE_BETA + h + 1]
        a = jnp.exp(sm[:, LANE_GDEC + h:LANE_GDEC + h + 1])
        s = sdn_ref[0, h]
        kc = _column(k, eye)
        v_new = beta * (v - a * jnp.sum(kc * s, axis=0, keepdims=True))
        s = a * s + kc * v_new
        sdn_o_ref[0, h] = s
        odn_ref[0, :, cs] = jnp.sum(_column(q, eye) * s, axis=0, keepdims=True)


def _rec_step(hq, hlf, hk, hi, dqkv, small, conv_w, s_hg, s_dn, buf):
    nb = hq.shape[0]
    row3 = lambda a: a.reshape(nb, 1, a.shape[-1])
    rspec = lambda w: pl.BlockSpec((1, 1, w), lambda b: (b, 0, 0))
    st_spec = pl.BlockSpec((1, N_HEADS, HEAD_DIM, HEAD_DIM), lambda b: (b, 0, 0, 0))
    buf_spec = pl.BlockSpec((1, CONV_WIDTH - 1, 3 * GROUP_WIDTH), lambda b: (b, 0, 0))
    outs = pl.pallas_call(
        _rec_step_kernel, grid=(nb,),
        in_specs=[rspec(GROUP_WIDTH)] * 4 + [rspec(3 * GROUP_WIDTH), rspec(SMALL_WIDTH),
                                             pl.BlockSpec(conv_w.shape, lambda b: (0, 0)), st_spec, st_spec, buf_spec],
        out_specs=[rspec(GROUP_WIDTH), rspec(GROUP_WIDTH), st_spec, st_spec, buf_spec],
        out_shape=[jax.ShapeDtypeStruct((nb, 1, GROUP_WIDTH), F32)] * 2
        + [jax.ShapeDtypeStruct(s_hg.shape, F32), jax.ShapeDtypeStruct(s_dn.shape, F32),
           jax.ShapeDtypeStruct(buf.shape, F32)],
        compiler_params=pltpu.CompilerParams(dimension_semantics=("arbitrary",), vmem_limit_bytes=VMEM_LIMIT),
        name="recurrent_step",
    )(row3(hq), row3(hlf), row3(hk), row3(hi), row3(dqkv), row3(small), conv_w, s_hg, s_dn, buf)
    ohg, odn, s_hg_new, s_dn_new, buf_new = outs
    return ohg.reshape(nb, GROUP_WIDTH), odn.reshape(nb, GROUP_WIDTH), s_hg_new, s_dn_new, buf_new


def _tile_gain(g):
    return jnp.tile(g.astype(F32), N_HEADS)


def _relayout_w_in(w_in_l):
    gw = GROUP_WIDTH
    a = 7 * gw
    big = jnp.concatenate([w_in_l[:, :a], w_in_l[:, a + N_HEADS:a + N_HEADS + 7 * gw]], axis=1)
    e = a + N_HEADS + 7 * gw
    small = jnp.concatenate([w_in_l[:, a:a + N_HEADS], w_in_l[:, e:e + 2 * N_HEADS]], axis=1)
    small = jnp.pad(small, ((0, 0), (0, SMALL_WIDTH - 3 * N_HEADS)))
    return jnp.concatenate([big, small], axis=1).astype(BF16)


def _small_params(f_bias, dt_bias, a_log):
    sp = jnp.zeros((8, SMALL_WIDTH), F32)
    sp = sp.at[0, LANE_FLOG:LANE_FLOG + N_HEADS].set(f_bias.astype(F32))
    sp = sp.at[1, LANE_GDEC:LANE_GDEC + N_HEADS].set(dt_bias.astype(F32))
    sp = sp.at[2, LANE_GDEC:LANE_GDEC + N_HEADS].set(a_log.astype(F32))
    return sp


def _pick(n, candidates):
    for c in candidates:
        if n % c == 0:
            return c
    return n


def kernel(x_prompt, x_sample, cache_fox_k, cache_fox_v, cache_fox_logf, cache_sb_k, cache_sb_v, state_hgrn, state_dn, state_dn_conv, page_table, hgrn_lb_param, w_in, w_out, ln1_g, ln2_g, fox_f_bias, fox_q_norm, fox_k_norm, sb_q_norm, sb_k_norm, hgrn_out_norm, fox_out_norm, sb_out_norm, dn_out_norm, dn_conv_w, dn_dt_bias, dn_a_log, w_up, w_down):
    depth = w_in.shape[0]
    bsz, seq, d = x_prompt.shape
    nb = x_sample.shape[0]
    n_phys, page = cache_fox_k.shape[1], cache_fox_k.shape[2]
    m = bsz * seq
    tm = _pick(seq, (256, 128, 64, 32, 16, 8))
    tm_mlp = _pick(m, (512, 256, 128, 64, 32, 16, 8))
    tf = _pick(w_up.shape[2], (1024, 512, 256, 128))
    tq = _pick(seq, (512, 256, 128))
    tt = _pick(seq, (256, 128, 64))

    yp = x_prompt.reshape(m, d)
    ys = x_sample.reshape(nb, d)
    lbp = hgrn_lb_param.astype(F32)
    kv_t = lambda a: a.transpose(0, 1, 3, 4, 2).reshape(depth, n_phys, GROUP_WIDTH, page)
    lf_t = jnp.swapaxes(cache_fox_logf.astype(F32), 2, 3)
    p_out = [[] for _ in range(8)]
    s_out = [[] for _ in range(8)]
    for l in range(depth):
        w_re = _relayout_w_in(w_in[l])
        w_o = w_out[l].astype(BF16)
        w_u = w_up[l].astype(BF16)
        w_d = w_down[l].astype(BF16)
        g1 = ln1_g[l].reshape(1, d).astype(F32)
        g2 = ln2_g[l].reshape(1, d).astype(F32)
        qk_gains = jnp.stack([_tile_gain(fox_q_norm[l]), _tile_gain(fox_k_norm[l]),
                              _tile_gain(sb_q_norm[l]), _tile_gain(sb_k_norm[l])])
        out_gains = jnp.stack([_tile_gain(hgrn_out_norm[l]), _tile_gain(fox_out_norm[l]),
                               _tile_gain(sb_out_norm[l]), _tile_gain(dn_out_norm[l])])
        sp = _small_params(fox_f_bias[l], dn_dt_bias[l], dn_a_log[l])
        conv_w = dn_conv_w[l].astype(F32)

        (hq, hlf, hk, hi, hgate, fq, fk, fv, sq, sk, sv, dqkv, dz, small, cum, small_t, cum_t) = _inproj(
            yp, g1, w_re, lbp, qk_gains, sp, layer=l, tm=tm, rows_per_seq=seq, with_time=True)
        b3 = lambda a: a.reshape(bsz, seq, a.shape[-1])
        o_hg, st_hg = _hgrn_prompt(b3(hq), b3(hlf), b3(hk), b3(hi), tt=tt)
        o_fx = _fox_prompt(b3(fq), fk, fv, b3(cum), cum_t, tq=tq)
        o_sb = _sb_prompt(b3(sq), sk, sv, tq=tq)
        o_dn, st_dn = _gdn_prompt(b3(dqkv), conv_w, b3(small), small_t, tt=tt)
        yp = _outproj(yp, o_hg.reshape(m, -1), hgate, o_fx.reshape(m, -1), o_sb.reshape(m, -1),
                      o_dn.reshape(m, -1), dz, out_gains, w_o, tm=tm)
        yp = _mlp(yp, g2, w_u, w_d, tm=tm_mlp, tf=tf)
        by_head = lambda a: a.reshape(bsz, N_HEADS, HEAD_DIM, seq).transpose(0, 3, 1, 2)
        p_out[0].append(by_head(fk))
        p_out[1].append(by_head(fv))
        p_out[2].append(b3(small)[:, :, LANE_FLOG:LANE_FLOG + N_HEADS])
        p_out[3].append(by_head(sk))
        p_out[4].append(by_head(sv))
        p_out[5].append(jnp.swapaxes(st_hg, -1, -2))
        p_out[6].append(st_dn)
        p_out[7].append(b3(dqkv)[:, seq - (CONV_WIDTH - 1):, :])

        (hq, hlf, hk, hi, hgate, fq, fk, fv, sq, sk, sv, dqkv, dz, small) = _inproj(
            ys, g1, w_re, lbp, qk_gains, sp, layer=l, tm=nb, rows_per_seq=nb, with_time=False)
        o_fx = _dec_attn(page_table, fq, fk, fv, small, kv_t(cache_fox_k), kv_t(cache_fox_v), lf_t,
                         layer=l, fox=True)
        o_sb = _dec_attn(page_table, sq, sk, sv, small, kv_t(cache_sb_k), kv_t(cache_sb_v), None,
                         layer=l, fox=False)
        o_hg, o_dn, s_hg_new, s_dn_new, buf_new = _rec_step(
            hq, hlf, hk, hi, dqkv, small, conv_w, state_hgrn[l].astype(F32), state_dn[l].astype(F32),
            state_dn_conv[l].astype(F32))
        ys = _outproj(ys, o_hg, hgate, o_fx, o_sb, o_dn, dz, out_gains, w_o, tm=nb)
        ys = _mlp(ys, g2, w_u, w_d, tm=nb, tf=tf)
        sshape = (nb, 1, N_HEADS, HEAD_DIM)
        s_out[0].append(fk.reshape(sshape))
        s_out[1].append(fv.reshape(sshape))
        s_out[2].append(small[:, LANE_FLOG:LANE_FLOG + N_HEADS].reshape(nb, 1, N_HEADS))
        s_out[3].append(sk.reshape(sshape))
        s_out[4].append(sv.reshape(sshape))
        s_out[5].append(s_hg_new)
        s_out[6].append(s_dn_new)
        s_out[7].append(buf_new)

    p = [jnp.stack(v) for v in p_out]
    s = [jnp.stack(v) for v in s_out]
    return (yp.reshape(bsz, seq, d), ys.reshape(nb, 1, d), *p, *s)
```

```python
import functools

import jax
import jax.numpy as jnp
from jax import lax
from jax.experimental import pallas as pl
from jax.experimental.pallas import tpu as pltpu

F32 = jnp.float32
BF16 = jnp.bfloat16

HEAD_DIM = 64
N_HEADS = 4
GROUP_WIDTH = N_HEADS * HEAD_DIM
N_SEGMENTS = 14
SMALL_WIDTH = 128
CONV_WIDTH = 4
NORM_EPS = 1e-6
NEG_BIG = -1e30
QK_SCALE = HEAD_DIM ** -0.5
LOG2E = 1.4426950408889634
INV_LN2 = LOG2E
VMEM_LIMIT = 56 * 1024 * 1024

LANE_FLOG = 0
LANE_BETA = 4
LANE_GDEC = 8
SMALL_ROWS_T = 16


def _iota(shape, dim):
    return lax.broadcasted_iota(jnp.int32, shape, dim)


def _dot(a, b):
    return jnp.dot(a, b, preferred_element_type=F32)


def _dot_nt(a, b):
    return lax.dot_general(a, b, (((1,), (1,)), ((), ())), preferred_element_type=F32)


def _split3(x):
    hi = x.astype(BF16)
    r = x - hi.astype(F32)
    mid = r.astype(BF16)
    lo = (r - mid.astype(F32)).astype(BF16)
    return hi, mid, lo


def _dot_xc(x, c, parts=3):
    ps = _split3(x)[:parts]
    out = _dot(ps[0], c)
    for p in ps[1:]:
        out = out + _dot(p, c)
    return out


def _dot_cx(c, x, parts=3):
    ps = _split3(x)[:parts]
    out = _dot(c, ps[0])
    for p in ps[1:]:
        out = out + _dot(c, p)
    return out


def _dot_f32(a, b):
    ah = a.astype(BF16)
    al = (a - ah.astype(F32)).astype(BF16)
    bh = b.astype(BF16)
    bl = (b - bh.astype(F32)).astype(BF16)
    return _dot(ah, bh) + _dot(ah, bl) + _dot(al, bh)


def _head_ones(n=GROUP_WIDTH):
    return (_iota((n, n), 0) // HEAD_DIM == _iota((n, n), 1) // HEAD_DIM).astype(BF16)


def _head_sum(x, ones):
    return _dot_xc(x, ones)


def _head_rms(x, ones, gain):
    ms = _head_sum(x * x, ones) * (1.0 / HEAD_DIM)
    return x * lax.rsqrt(ms + NORM_EPS) * gain


def _log_sigmoid(x):
    return jnp.minimum(x, 0.0) - jnp.log1p(jnp.exp(-jnp.abs(x)))


def _softplus(x):
    return jnp.maximum(x, 0.0) + jnp.log1p(jnp.exp(-jnp.abs(x)))


def _sigmoid(x):
    return 1.0 / (1.0 + jnp.exp(-x))


def _silu(x):
    return x * _sigmoid(x)


def _inproj_kernel(x_ref, g1_ref, w_ref, lbp_ref, gains_ref, sp_ref, *refs, layer, depth, tiles_per_seq,
                   with_time):
    (hq_ref, hlf_ref, hk_ref, hi_ref, hg_ref, fq_ref, fk_ref, fv_ref, sq_ref, sk_ref, sv_ref,
     dqkv_ref, dz_ref, small_ref) = refs[:14]
    x = x_ref[...]
    h = (x * lax.rsqrt(jnp.mean(x * x, axis=-1, keepdims=True) + NORM_EPS) * g1_ref[...]).astype(BF16)

    def seg(j, width=GROUP_WIDTH):
        return _dot_nt(h, w_ref[j * GROUP_WIDTH:j * GROUP_WIDTH + width, :])

    ones = _head_ones()

    rows = [lbp_ref[i:i + 1, :] for i in range(depth)]
    mx = functools.reduce(jnp.maximum, rows)
    es = [jnp.exp(r - mx) for r in rows]
    lb = sum(es[1:layer + 1], jnp.zeros_like(mx)) / sum(es)
    hq_ref[...] = seg(0)
    hf = seg(1)
    a = jnp.log(lb)
    b = jnp.log1p(-lb) + _log_sigmoid(hf)
    hi = jnp.maximum(a, b)
    lo = jnp.minimum(a, b)
    hlf_ref[...] = hi + jnp.log1p(jnp.exp(lo - hi))
    hk_ref[...] = (1.0 - lb) * _sigmoid(-hf)
    hi_ref[...] = seg(2)
    hg_ref[...] = _silu(seg(3))

    fq_ref[...] = _head_rms(seg(4), ones, gains_ref[0:1, :]) * QK_SCALE
    sq_ref[...] = _head_rms(seg(7), ones, gains_ref[2:3, :]) * QK_SCALE
    kv = (_head_rms(seg(5), ones, gains_ref[1:2, :]), seg(6), _head_rms(seg(8), ones, gains_ref[3:4, :]), seg(9))
    for ref, val in zip((fk_ref, fv_ref, sk_ref, sv_ref), kv):
        if with_time:
            ref[0] = val.T
        else:
            ref[...] = val

    dqkv_ref[...] = seg(10, 3 * GROUP_WIDTH)
    dz_ref[...] = _silu(seg(13))

    s = _dot_nt(h, w_ref[N_SEGMENTS * GROUP_WIDTH:, :])
    lane = _iota(s.shape, 1)
    f_log = _log_sigmoid(s + sp_ref[0:1, :])
    beta = _sigmoid(s)
    g_dec = -jnp.exp(sp_ref[2:3, :]) * _softplus(s + sp_ref[1:2, :])
    small = jnp.where(lane < LANE_BETA, f_log, jnp.where(lane < LANE_GDEC, beta, g_dec))
    small_ref[...] = small

    if with_time:
        cum_ref, small_t_ref, cum_t_ref, carry_ref = refs[14:]
        tm = s.shape[0]

        @pl.when(pl.program_id(0) % tiles_per_seq == 0)
        def _():
            carry_ref[...] = jnp.zeros_like(carry_ref)

        tril = (_iota((tm, tm), 1) <= _iota((tm, tm), 0)).astype(BF16)
        cum = _dot_cx(tril, small) + carry_ref[0:1, :]
        cum_ref[...] = cum
        carry_ref[...] = jnp.broadcast_to(cum[tm - 1:tm, :], carry_ref.shape)
        small_t_ref[0] = small.T[:SMALL_ROWS_T, :]
        cum_t_ref[0] = cum.T[:SMALL_ROWS_T, :]


def _inproj(x2, g1, w_re, lbp, gains, sp, *, layer, tm, rows_per_seq, with_time):
    m, d = x2.shape
    depth = lbp.shape[0]
    grid = (m // tm,)
    row = lambda i: (i, 0)
    const = lambda i: (0, 0)
    tps = rows_per_seq // tm
    nseq = m // rows_per_seq
    seg_shape = jax.ShapeDtypeStruct((m, GROUP_WIDTH), F32)
    seg_spec = pl.BlockSpec((tm, GROUP_WIDTH), row)
    out_shape = [seg_shape] * 11 + [jax.ShapeDtypeStruct((m, 3 * GROUP_WIDTH), F32), seg_shape,
                                    jax.ShapeDtypeStruct((m, SMALL_WIDTH), F32)]
    out_specs = [seg_spec] * 11 + [pl.BlockSpec((tm, 3 * GROUP_WIDTH), row), seg_spec,
                                   pl.BlockSpec((tm, SMALL_WIDTH), row)]
    scratch = []
    if with_time:
        def by_time(rows):
            return (jax.ShapeDtypeStruct((nseq, rows, rows_per_seq), F32),
                    pl.BlockSpec((1, rows, tm), lambda i: (i // tps, 0, i % tps)))
        for idx in (6, 7, 9, 10):
            out_shape[idx], out_specs[idx] = by_time(GROUP_WIDTH)
        out_shape += [jax.ShapeDtypeStruct((m, SMALL_WIDTH), F32)]
        out_specs += [pl.BlockSpec((tm, SMALL_WIDTH), row)]
        for _ in range(2):
            sh, sp_ = by_time(SMALL_ROWS_T)
            out_shape.append(sh)
            out_specs.append(sp_)
        scratch = [pltpu.VMEM((8, SMALL_WIDTH), F32)]
    kern = functools.partial(_inproj_kernel, layer=layer, depth=depth, tiles_per_seq=tps,
                             with_time=with_time)
    return pl.pallas_call(
        kern, grid=grid,
        in_specs=[pl.BlockSpec((tm, d), row), pl.BlockSpec((1, d), const), pl.BlockSpec(w_re.shape, const),
                  pl.BlockSpec(lbp.shape, const), pl.BlockSpec(gains.shape, const), pl.BlockSpec(sp.shape, const)],
        out_specs=out_specs, out_shape=out_shape, scratch_shapes=scratch,
        compiler_params=pltpu.CompilerParams(dimension_semantics=("arbitrary",), vmem_limit_bytes=VMEM_LIMIT),
        name="inproj_time" if with_time else "inproj_step",
    )(x2, g1, w_re, lbp, gains, sp)


HGRN_SUB = 16


def _hgrn_kernel(q_ref, lf_ref, k_ref, v_ref, o_ref, st_ref, s_ref, oi_ref, *, tt):
    t = pl.program_id(1)

    @pl.when(t == 0)
    def _():
        s_ref[...] = jnp.zeros_like(s_ref)

    q = q_ref[0]
    lf = lf_ref[0]
    kin = k_ref[0]
    v = v_ref[0]
    r = _iota((tt, tt), 0)
    c = _iota((tt, tt), 1)
    same = (r // HGRN_SUB) == (c // HGRN_SUB)
    g = _dot_cx(jnp.where(same & (c <= r), 1.0, 0.0).astype(BF16), lf)
    gl = _dot_cx(jnp.where(same, 1.0, 0.0).astype(BF16), lf)
    qg = q * jnp.exp(g)
    kg = kin * jnp.exp(gl - g)

    ones = _head_ones()
    rowmod = _iota((tt, GROUP_WIDTH), 0) % HGRN_SUB
    o = jnp.zeros((tt, GROUP_WIDTH), F32)
    for d in range(HGRN_SUB):
        if d == 0:
            kd, gd, vd = kin, g, v
        else:
            kd = pltpu.roll(kin, d, 0)
            gd = pltpu.roll(g, d, 0)
            vd = pltpu.roll(v, d, 0)
        e = jnp.where(rowmod >= d, g - gd, NEG_BIG)
        p = q * kd * jnp.exp(e)
        o = o + _dot_xc(p, ones, parts=2) * vd

    v_t = v.T
    for i in range(tt // HGRN_SUB):
        rs = slice(i * HGRN_SUB, (i + 1) * HGRN_SUB)
        for h in range(N_HEADS):
            cs = slice(h * HEAD_DIM, (h + 1) * HEAD_DIM)
            s = s_ref[h]
            oi_ref[rs, cs] = _dot_nt(qg[rs, cs].astype(BF16), s.astype(BF16))
            dec = jnp.exp(gl[i * HGRN_SUB:i * HGRN_SUB + 1, cs])
            s_ref[h] = dec * s + _dot(v_t[cs, rs].astype(BF16), kg[rs, cs].astype(BF16))
    o_ref[0] = o + oi_ref[...]

    @pl.when(t == pl.num_programs(1) - 1)
    def _():
        st_ref[0] = s_ref[...]


def _hgrn_prompt(hq, hlf, hk, hi, *, tt):
    b, t, _ = hq.shape
    blk = pl.BlockSpec((1, tt, GROUP_WIDTH), lambda i, j: (i, j, 0))
    return pl.pallas_call(
        functools.partial(_hgrn_kernel, tt=tt), grid=(b, t // tt),
        in_specs=[blk] * 4,
        out_specs=[blk, pl.BlockSpec((1, N_HEADS, HEAD_DIM, HEAD_DIM), lambda i, j: (i, 0, 0, 0))],
        out_shape=[jax.ShapeDtypeStruct((b, t, GROUP_WIDTH), F32),
                   jax.ShapeDtypeStruct((b, N_HEADS, HEAD_DIM, HEAD_DIM), F32)],
        scratch_shapes=[pltpu.VMEM((N_HEADS, HEAD_DIM, HEAD_DIM), F32), pltpu.VMEM((tt, GROUP_WIDTH), F32)],
        compiler_params=pltpu.CompilerParams(dimension_semantics=("arbitrary", "arbitrary"),
                                             vmem_limit_bytes=VMEM_LIMIT),
        name="hgrn_prompt",
    )(hq, hlf, hk, hi)


def _fox_kernel(q_ref, k_ref, v_ref, fq_ref, fk_ref, o_ref, qa_ref, m_ref, acc_ref, *, tq):
    qi = pl.program_id(1)
    j = pl.program_id(2)

    @pl.when(j == 0)
    def _():
        m_ref[...] = jnp.full_like(m_ref, NEG_BIG)
        acc_ref[...] = jnp.zeros_like(acc_ref)
        q = q_ref[0] * LOG2E
        f = fq_ref[0] * LOG2E
        lane = _iota((tq, HEAD_DIM), 1)
        for h in range(N_HEADS):
            hi, mid, lo = [p.astype(F32) for p in _split3(f[:, LANE_FLOG + h:LANE_FLOG + h + 1])]
            ext = jnp.where(lane == 0, hi, jnp.where(lane == 1, mid, jnp.where(lane == 2, lo,
                                                                               jnp.where(lane < 6, 1.0, 0.0))))
            qa_ref[h] = jnp.concatenate([q[:, h * HEAD_DIM:(h + 1) * HEAD_DIM], ext], axis=1).astype(BF16)

    def step(masked):
        kf = k_ref[0]
        vf = v_ref[0]
        fk = fk_ref[0] * LOG2E
        row = _iota((8, tq), 0)
        pad = jnp.zeros((HEAD_DIM - 8, tq), F32)
        v_ext = jnp.concatenate([jnp.where(row == 0, 1.0, 0.0), pad], axis=0)
        if masked:
            keep = _iota((tq, tq), 1) <= _iota((tq, tq), 0)
        heads = range(N_HEADS)
        ss = []
        for h in heads:
            cs = slice(h * HEAD_DIM, (h + 1) * HEAD_DIM)
            hi, mid, lo = [p.astype(F32) for p in _split3(fk[LANE_FLOG + h:LANE_FLOG + h + 1, :])]
            k_ext = jnp.where(row < 3, 1.0, jnp.where(row == 3, -hi, jnp.where(row == 4, -mid,
                                                                               jnp.where(row == 5, -lo, 0.0))))
            ka = jnp.concatenate([kf[cs, :], k_ext, pad], axis=0).astype(BF16)
            ss.append(_dot(qa_ref[h], ka))
        if masked:
            ss = [jnp.where(keep, s, NEG_BIG) for s in ss]
        m_prevs = [m_ref[h] for h in heads]
        m_news = [jnp.maximum(m_prevs[h], jnp.max(ss[h], axis=-1, keepdims=True)) for h in heads]
        ps = [jnp.exp2(ss[h] - jnp.concatenate([m_news[h]] * (tq // 128), axis=1)) for h in heads]
        for h in heads:
            va = jnp.concatenate([vf[h * HEAD_DIM:(h + 1) * HEAD_DIM, :], v_ext], axis=0).astype(BF16)
            m_ref[h] = m_news[h]
            acc_ref[h] = jnp.exp2(m_prevs[h] - m_news[h]) * acc_ref[h] + _dot_nt(ps[h].astype(BF16), va)

    @pl.when(j < qi)
    def _():
        step(False)

    @pl.when(j == qi)
    def _():
        step(True)
        for h in range(N_HEADS):
            acc = acc_ref[h]
            o_ref[0, :, h * HEAD_DIM:(h + 1) * HEAD_DIM] = acc[:, :HEAD_DIM] / acc[:, HEAD_DIM:HEAD_DIM + 1]


def _fox_prompt(fq, fk, fv, cum, cum_t, *, tq):
    b, t, _ = fq.shape
    n = t // tq
    qspec = pl.BlockSpec((1, tq, GROUP_WIDTH), lambda i, qi, j: (i, qi, 0))
    kspec = pl.BlockSpec((1, GROUP_WIDTH, tq), lambda i, qi, j: (i, 0, jnp.minimum(j, qi)))
    return pl.pallas_call(
        functools.partial(_fox_kernel, tq=tq), grid=(b, n, n),
        in_specs=[qspec, kspec, kspec,
                  pl.BlockSpec((1, tq, SMALL_WIDTH), lambda i, qi, j: (i, qi, 0)),
                  pl.BlockSpec((1, SMALL_ROWS_T, tq), lambda i, qi, j: (i, 0, jnp.minimum(j, qi)))],
        out_specs=qspec,
        out_shape=jax.ShapeDtypeStruct((b, t, GROUP_WIDTH), F32),
        scratch_shapes=[pltpu.VMEM((N_HEADS, tq, 2 * HEAD_DIM), BF16), pltpu.VMEM((N_HEADS, tq, 128), F32),
                        pltpu.VMEM((N_HEADS, tq, 2 * HEAD_DIM), F32)],
        compiler_params=pltpu.CompilerParams(dimension_semantics=("arbitrary", "arbitrary", "arbitrary"),
                                             vmem_limit_bytes=VMEM_LIMIT),
        name="fox_prompt",
    )(fq, fk, fv, cum, cum_t)


SB_SUB = 256


def _sb_kernel(q_ref, k_ref, v_ref, o_ref, qb_ref, carry_ref, acc_ref, *, tq):
    qi = pl.program_id(1)
    j = pl.program_id(2)

    @pl.when(j == 0)
    def _():
        carry_ref[...] = jnp.zeros_like(carry_ref)
        acc_ref[...] = jnp.zeros_like(acc_ref)
        qb_ref[...] = (q_ref[0] * LOG2E).astype(BF16)

    def step(masked):
        k = k_ref[0].astype(BF16)
        v = v_ref[0].astype(BF16)
        heads = range(N_HEADS)
        rows = lambda h: slice(h * HEAD_DIM, (h + 1) * HEAD_DIM)
        r = _iota((tq, tq), 0)
        c = _iota((tq, tq), 1)
        sub = min(SB_SUB, tq)
        suffix = jnp.where(_iota((sub, sub), 1) <= _iota((sub, sub), 0), 1.0, 0.0).astype(BF16)
        z2s = [_dot(qb_ref[:, rows(h)], k[rows(h), :]) for h in heads]
        sps = [jnp.maximum(z2, 0.0) + jnp.log(1.0 + jnp.exp2(-jnp.abs(z2))) * INV_LN2 for z2 in z2s]
        if masked:
            sps = [jnp.where(c < r, sp, 0.0) for sp in sps]
        carries = [carry_ref[h] for h in heads]
        cum_parts = [[None] * (tq // sub) for _ in heads]
        for part in reversed(range(tq // sub)):
            ks = slice(part * sub, (part + 1) * sub)
            for h in heads:
                wide = jnp.concatenate([carries[h]] * (sub // 128), axis=1)
                cum = _dot_xc(sps[h][:, ks], suffix, parts=2) + wide
                cum_parts[h][part] = cum
                carries[h] = jnp.broadcast_to(cum[:, 0:1], carries[h].shape)
        es = [z2s[h] - jnp.concatenate(cum_parts[h], axis=1) for h in heads]
        if masked:
            es = [jnp.where(c < r, e, NEG_BIG) for e in es]
        pad = jnp.zeros((HEAD_DIM, tq), BF16)
        for h in heads:
            va = jnp.concatenate([v[rows(h), :], pad], axis=0)
            acc_ref[h] = acc_ref[h] + _dot_nt(jnp.exp2(es[h]).astype(BF16), va)
            carry_ref[h] = carries[h]

    @pl.when(j == 0)
    def _():
        step(True)

    @pl.when((j > 0) & (j <= qi))
    def _():
        step(False)

    @pl.when(j == qi)
    def _():
        for h in range(N_HEADS):
            o_ref[0, :, h * HEAD_DIM:(h + 1) * HEAD_DIM] = acc_ref[h][:, :HEAD_DIM]


def _sb_prompt(sq, sk, sv, *, tq):
    b, t, _ = sq.shape
    n = t // tq
    qspec = pl.BlockSpec((1, tq, GROUP_WIDTH), lambda i, qi, j: (i, qi, 0))
    kspec = pl.BlockSpec((1, GROUP_WIDTH, tq), lambda i, qi, j: (i, 0, jnp.maximum(qi - j, 0)))
    return pl.pallas_call(
        functools.partial(_sb_kernel, tq=tq), grid=(b, n, n),
        in_specs=[qspec, kspec, kspec],
        out_specs=qspec,
        out_shape=jax.ShapeDtypeStruct((b, t, GROUP_WIDTH), F32),
        scratch_shapes=[pltpu.VMEM((tq, GROUP_WIDTH), BF16), pltpu.VMEM((N_HEADS, tq, 128), F32),
                        pltpu.VMEM((N_HEADS, tq, 2 * HEAD_DIM), F32)],
        compiler_params=pltpu.CompilerParams(dimension_semantics=("arbitrary", "arbitrary", "arbitrary"),
                                             vmem_limit_bytes=VMEM_LIMIT),
        name="sb_prompt",
    )(sq, sk, sv)


GDN_CHUNK = 128
GDN_BASE = 16


def _unit_lower_inverses(lmats, ii, jj):
    ns = [jnp.where(ii // GDN_BASE == jj // GDN_BASE, -lm, 0.0) for lm in lmats]
    eye = jnp.where(ii == jj, 1.0, 0.0)
    ts = [eye + n for n in ns]
    ps = ns
    for _ in range(GDN_BASE.bit_length() - 2):
        ps = [_dot_f32(p, p) for p in ps]
        ts = [t + _dot_f32(t, p) for t, p in zip(ts, ps)]
    b = GDN_BASE
    while b < GDN_CHUNK:
        lower_left = (ii // (2 * b) == jj // (2 * b)) & (ii // b != jj // b)
        mids = [_dot_f32(t, jnp.where(lower_left, lm, 0.0)) for t, lm in zip(ts, lmats)]
        ts = [t - _dot_f32(mid, t) for t, mid in zip(ts, mids)]
        b *= 2
    return ts


def _gdn_prep_kernel(x_ref, w_ref, sm_ref, smt_ref, u_ref, wk_ref, qe_ref, attn_ref, kdt_ref, gct_ref, ext_ref, *, tt):
    t = pl.program_id(1)

    @pl.when(t == 0)
    def _():
        ext_ref[0:8, :] = jnp.zeros((8, 3 * GROUP_WIDTH), F32)

    @pl.when(t > 0)
    def _():
        ext_ref[0:8, :] = ext_ref[tt:tt + 8, :]

    ext_ref[8:8 + tt, :] = x_ref[0]
    y = ext_ref[pl.ds(8 - (CONV_WIDTH - 1), tt), :] * w_ref[0:1, :]
    for jw in range(1, CONV_WIDTH):
        y = y + ext_ref[pl.ds(8 - (CONV_WIDTH - 1) + jw, tt), :] * w_ref[jw:jw + 1, :]
    y = _silu(y)
    ones = _head_ones()
    q = y[:, :GROUP_WIDTH]
    k = y[:, GROUP_WIDTH:2 * GROUP_WIDTH]
    v = y[:, 2 * GROUP_WIDTH:]
    q = q * lax.rsqrt(_head_sum(q * q, ones) + NORM_EPS) * QK_SCALE
    k = k * lax.rsqrt(_head_sum(k * k, ones) + NORM_EPS)
    sm = sm_ref[0]
    smt = smt_ref[0]
    r = _iota((tt, tt), 0)
    c = _iota((tt, tt), 1)
    same = (r // GDN_CHUNK) == (c // GDN_CHUNK)
    gc_col = _dot_cx(jnp.where(same & (c <= r), 1.0, 0.0).astype(BF16), sm)
    gc_row = _dot_xc(smt, jnp.where(same & (r <= c), 1.0, 0.0).astype(BF16))
    gct_ref[0] = gc_row
    k_t = k.T
    ii = _iota((GDN_CHUNK, GDN_CHUNK), 0)
    jj = _iota((GDN_CHUNK, GDN_CHUNK), 1)
    n_chunks = tt // GDN_CHUNK
    lmats, rhss = [], []
    for ci in range(n_chunks):
        rs = slice(ci * GDN_CHUNK, (ci + 1) * GDN_CHUNK)
        qes, attns, kdts = [], [], []
        for h in range(N_HEADS):
            cs = slice(h * HEAD_DIM, (h + 1) * HEAD_DIM)
            gcol = gc_col[rs, LANE_GDEC + h:LANE_GDEC + h + 1]
            grow = gc_row[LANE_GDEC + h:LANE_GDEC + h + 1, rs]
            beta = sm[rs, LANE_BETA + h:LANE_BETA + h + 1]
            dec = jnp.exp(jnp.where(ii >= jj, gcol - grow, NEG_BIG))
            qh = q[rs, cs].astype(BF16)
            kth = k_t[cs, rs]
            kb = k[rs, cs] * beta
            eg = jnp.exp(gcol)
            lmats.append(jnp.where(ii > jj, _dot(kb.astype(BF16), kth.astype(BF16)) * dec, 0.0))
            rhss.append(jnp.concatenate([v[rs, cs] * beta, kb * eg], axis=1))
            qes.append(q[rs, cs] * eg)
            attns.append(_dot(qh, kth.astype(BF16)) * dec)
            kdts.append(kth * jnp.exp(grow[:, GDN_CHUNK - 1:GDN_CHUNK] - grow))
        qe_ref[0, rs, :] = jnp.concatenate(qes, axis=1).astype(BF16)
        attn_ref[0, rs, :] = jnp.concatenate(attns, axis=1).astype(BF16)
        kdt_ref[0, :, rs] = jnp.concatenate(kdts, axis=0).astype(BF16)
    tinvs = _unit_lower_inverses(lmats, ii, jj)
    sols = [_dot_f32(ti, rhs) for ti, rhs in zip(tinvs, rhss)]
    for ci in range(n_chunks):
        rs = slice(ci * GDN_CHUNK, (ci + 1) * GDN_CHUNK)
        chunk = sols[ci * N_HEADS:(ci + 1) * N_HEADS]
        u_ref[0, rs, :] = jnp.concatenate([sol[:, :HEAD_DIM] for sol in chunk], axis=1)
        wk_ref[0, rs, :] = jnp.concatenate([sol[:, HEAD_DIM:] for sol in chunk], axis=1).astype(BF16)


def _gdn_scan_kernel(u_ref, wk_ref, qe_ref, attn_ref, kdt_ref, gct_ref, o_ref, st_ref, s_ref, *, tt, nb):
    t = pl.program_id(0)

    @pl.when(t == 0)
    def _():
        s_ref[...] = jnp.zeros_like(s_ref)

    for ci in range(tt // GDN_CHUNK):
        rs = slice(ci * GDN_CHUNK, (ci + 1) * GDN_CHUNK)
        items = [(b, h) for b in range(nb) for h in range(N_HEADS)]
        cs = lambda h: slice(h * HEAD_DIM, (h + 1) * HEAD_DIM)
        us = [u_ref[b, rs, :] for b in range(nb)]
        wks = [wk_ref[b, rs, :] for b in range(nb)]
        qes = [qe_ref[b, rs, :] for b in range(nb)]
        attns = [attn_ref[b, rs, :] for b in range(nb)]
        kdts = [kdt_ref[b, :, rs] for b in range(nb)]
        decays = [jnp.exp(gct_ref[b, :, rs][:, GDN_CHUNK - 1:GDN_CHUNK]) for b in range(nb)]
        ss = [s_ref[b, h] for b, h in items]
        sbs = [s.astype(BF16) for s in ss]
        v_news = [us[b][:, cs(h)] - _dot(wks[b][:, cs(h)], sb) for (b, h), sb in zip(items, sbs)]
        vbs = [vn.astype(BF16) for vn in v_news]
        for (b, h), s, vb in zip(items, ss, vbs):
            a = decays[b][LANE_GDEC + h:LANE_GDEC + h + 1, :]
            s_ref[b, h] = a * s + _dot(kdts[b][cs(h), :], vb)
        for b in range(nb):
            outs = [_dot(qes[b][:, cs(h)], sbs[b * N_HEADS + h])
                    + _dot(attns[b][:, h * GDN_CHUNK:(h + 1) * GDN_CHUNK], vbs[b * N_HEADS + h])
                    for h in range(N_HEADS)]
            o_ref[b, rs, :] = jnp.concatenate(outs, axis=1)

    @pl.when(t == pl.num_programs(0) - 1)
    def _():
        st_ref[...] = s_ref[...]


def _gdn_prompt(dqkv, conv_w, small, small_t, *, tt):
    b, t, _ = dqkv.shape
    n_attn = N_HEADS * GDN_CHUNK
    by_rows = lambda w: pl.BlockSpec((1, tt, w), lambda i, j: (i, j, 0))
    by_cols = lambda r: pl.BlockSpec((1, r, tt), lambda i, j: (i, 0, j))
    u, wk, qe, attn, kdt, gct = pl.pallas_call(
        functools.partial(_gdn_prep_kernel, tt=tt), grid=(b, t // tt),
        in_specs=[by_rows(3 * GROUP_WIDTH), pl.BlockSpec(conv_w.shape, lambda i, j: (0, 0)),
                  by_rows(SMALL_WIDTH), by_cols(SMALL_ROWS_T)],
        out_specs=[by_rows(GROUP_WIDTH), by_rows(GROUP_WIDTH), by_rows(GROUP_WIDTH), by_rows(n_attn),
                   by_cols(GROUP_WIDTH), by_cols(SMALL_ROWS_T)],
        out_shape=[jax.ShapeDtypeStruct((b, t, GROUP_WIDTH), F32), jax.ShapeDtypeStruct((b, t, GROUP_WIDTH), BF16),
                   jax.ShapeDtypeStruct((b, t, GROUP_WIDTH), BF16), jax.ShapeDtypeStruct((b, t, n_attn), BF16),
                   jax.ShapeDtypeStruct((b, GROUP_WIDTH, t), BF16), jax.ShapeDtypeStruct((b, SMALL_ROWS_T, t), F32)],
        scratch_shapes=[pltpu.VMEM((tt + 8, 3 * GROUP_WIDTH), F32)],
        compiler_params=pltpu.CompilerParams(dimension_semantics=("arbitrary", "arbitrary"),
                                             vmem_limit_bytes=VMEM_LIMIT),
        name="gdn_prep",
    )(dqkv, conv_w, small, small_t)
    all_rows = lambda w: pl.BlockSpec((b, tt, w), lambda j: (0, j, 0))
    all_cols = lambda r: pl.BlockSpec((b, r, tt), lambda j: (0, 0, j))
    st_spec = pl.BlockSpec((b, N_HEADS, HEAD_DIM, HEAD_DIM), lambda j: (0, 0, 0, 0))
    return pl.pallas_call(
        functools.partial(_gdn_scan_kernel, tt=tt, nb=b), grid=(t // tt,),
        in_specs=[all_rows(GROUP_WIDTH), all_rows(GROUP_WIDTH), all_rows(GROUP_WIDTH), all_rows(n_attn),
                  all_cols(GROUP_WIDTH), all_cols(SMALL_ROWS_T)],
        out_specs=[all_rows(GROUP_WIDTH), st_spec],
        out_shape=[jax.ShapeDtypeStruct((b, t, GROUP_WIDTH), F32),
                   jax.ShapeDtypeStruct((b, N_HEADS, HEAD_DIM, HEAD_DIM), F32)],
        scratch_shapes=[pltpu.VMEM((b, N_HEADS, HEAD_DIM, HEAD_DIM), F32)],
        compiler_params=pltpu.CompilerParams(dimension_semantics=("arbitrary",), vmem_limit_bytes=VMEM_LIMIT),
        name="gdn_scan",
    )(u, wk, qe, attn, kdt, gct)


def _outproj_kernel(x_ref, ohg_ref, hg_ref, ofx_ref, osb_ref, odn_ref, dz_ref, gains_ref, w_ref, y_ref):
    ones = _head_ones()
    parts = [
        _head_rms(ohg_ref[...], ones, gains_ref[0:1, :]) * hg_ref[...],
        _head_rms(ofx_ref[...], ones, gains_ref[1:2, :]),
        _head_rms(osb_ref[...], ones, gains_ref[2:3, :]),
        _head_rms(odn_ref[...], ones, gains_ref[3:4, :]) * dz_ref[...],
    ]
    y = x_ref[...]
    for gidx, p in enumerate(parts):
        y = y + _dot(p.astype(BF16), w_ref[gidx * GROUP_WIDTH:(gidx + 1) * GROUP_WIDTH, :])
    y_ref[...] = y


def _outproj(x2, ohg, hgate, ofx, osb, odn, dz, gains, w_out, *, tm):
    m, d = x2.shape
    row = lambda i: (i, 0)
    const = lambda i: (0, 0)
    seg = pl.BlockSpec((tm, GROUP_WIDTH), row)
    return pl.pallas_call(
        _outproj_kernel, grid=(m // tm,),
        in_specs=[pl.BlockSpec((tm, d), row)] + [seg] * 6 + [pl.BlockSpec(gains.shape, const),
                                                            pl.BlockSpec(w_out.shape, const)],
        out_specs=pl.BlockSpec((tm, d), row),
        out_shape=jax.ShapeDtypeStruct((m, d), F32),
        compiler_params=pltpu.CompilerParams(dimension_semantics=("arbitrary",), vmem_limit_bytes=VMEM_LIMIT),
        name="outproj",
    )(x2, ohg, hgate, ofx, osb, odn, dz, gains, w_out)


def _mlp_kernel(x_ref, g2_ref, wu_ref, wd_ref, y_ref, h_ref, acc_ref):
    f = pl.program_id(1)

    @pl.when(f == 0)
    def _():
        x = x_ref[...]
        h_ref[...] = (x * lax.rsqrt(jnp.mean(x * x, axis=-1, keepdims=True) + NORM_EPS) * g2_ref[...]).astype(BF16)
        acc_ref[...] = x

    u = jnp.maximum(_dot(h_ref[...], wu_ref[...]), 0.0)
    acc_ref[...] += _dot((u * u).astype(BF16), wd_ref[...])

    @pl.when(f == pl.num_programs(1) - 1)
    def _():
        y_ref[...] = acc_ref[...]


def _mlp(x2, g2, w_up, w_down, *, tm, tf):
    m, d = x2.shape
    dff = w_up.shape[1]
    return pl.pallas_call(
        _mlp_kernel, grid=(m // tm, dff // tf),
        in_specs=[pl.BlockSpec((tm, d), lambda i, f: (i, 0)), pl.BlockSpec((1, d), lambda i, f: (0, 0)),
                  pl.BlockSpec((d, tf), lambda i, f: (0, f)), pl.BlockSpec((tf, d), lambda i, f: (f, 0))],
        out_specs=pl.BlockSpec((tm, d), lambda i, f: (i, 0)),
        out_shape=jax.ShapeDtypeStruct((m, d), F32),
        scratch_shapes=[pltpu.VMEM((tm, d), BF16), pltpu.VMEM((tm, d), F32)],
        compiler_params=pltpu.CompilerParams(dimension_semantics=("arbitrary", "arbitrary"),
                                             vmem_limit_bytes=VMEM_LIMIT),
        name="mlp",
    )(x2, g2, w_up, w_down)


PAGES_PER_STEP = 16


def _head_rows(row):
    x = jnp.broadcast_to(row, (8, GROUP_WIDTH))
    return jnp.where(_iota((8, GROUP_WIDTH), 1) // HEAD_DIM == _iota((8, GROUP_WIDTH), 0), x, 0.0)


def _dec_attn_kernel(pt_ref, q_ref, kn_ref, vn_ref, sn_ref, *refs, fox, pp, page):
    k_refs = refs[:pp]
    v_refs = refs[pp:2 * pp]
    rest = refs[2 * pp:]
    if fox:
        lf_refs = rest[:pp]
        rest = rest[pp:]
    o_ref, m_ref, l_ref, acc_ref, carry_ref = rest
    j = pl.program_id(1)
    qb = _head_rows(q_ref[0]).astype(BF16)
    r = _iota((page, page), 0)
    c = _iota((page, page), 1)
    later = jnp.where(r > c, 1.0, 0.0).astype(BF16)

    @pl.when(j == 0)
    def _():
        if fox:
            kn = jnp.broadcast_to(kn_ref[0], (8, GROUP_WIDTH)).astype(BF16)
            m_ref[...] = jnp.broadcast_to(_dot_nt(qb, kn)[:, 0:1], m_ref.shape)
            l_ref[...] = jnp.ones_like(l_ref)
            acc_ref[...] = jnp.broadcast_to(vn_ref[0], acc_ref.shape).astype(BF16).astype(F32)
            sn = jnp.broadcast_to(sn_ref[0], (8, SMALL_WIDTH))
            lane = _iota((8, SMALL_WIDTH), 1)
            row = _iota((8, SMALL_WIDTH), 0)
            lf_new = jnp.sum(jnp.where(lane == row + LANE_FLOG, sn, 0.0), axis=-1, keepdims=True)
            carry_ref[...] = jnp.broadcast_to(jnp.where(_iota((8, 1), 0) < N_HEADS, lf_new, 0.0), carry_ref.shape)
        else:
            acc_ref[...] = jnp.zeros_like(acc_ref)
            carry_ref[...] = jnp.zeros_like(carry_ref)

    n8 = pp * 8
    tile_rows = lambda a: jnp.concatenate([a] * pp, axis=0)
    z = jnp.concatenate([_dot(qb, k_refs[i][0].astype(BF16)) for i in range(pp)], axis=0)
    if fox:
        pad = jnp.zeros((8 - N_HEADS, page), F32)
        x = jnp.concatenate([a for i in range(pp) for a in (lf_refs[i][0], pad)], axis=0)
    else:
        x = _log_sigmoid(-z)
    ri = _iota((n8, n8), 0)
    ci = _iota((n8, n8), 1)
    before = jnp.where((ri % 8 == ci % 8) & (ci // 8 < ri // 8), 1.0, 0.0).astype(BF16)
    tot = _dot_xc(x, jnp.ones((page, page), BF16))
    upto = _dot_cx(before, tot) + tile_rows(carry_ref[...])
    bias = _dot_xc(x, later) + upto
    carry_ref[...] = (upto + tot)[n8 - 8:, :]
    if fox:
        s = z + bias
        m_prev = m_ref[...]
        m_new = jnp.maximum(m_prev, jnp.max(jnp.max(s.reshape(pp, 8, page), axis=0), axis=-1, keepdims=True))
        alpha = jnp.exp(m_prev - m_new)
        p = jnp.exp(s - tile_rows(m_new))
        l_ref[...] = alpha * l_ref[...] + jnp.sum(jnp.sum(p.reshape(pp, 8, page), axis=0), axis=-1, keepdims=True)
        m_ref[...] = m_new
    else:
        p = jnp.exp(z + x + bias)
    pv = _dot_nt(p[0:8].astype(BF16), v_refs[0][0].astype(BF16))
    for i in range(1, pp):
        pv = pv + _dot_nt(p[8 * i:8 * i + 8].astype(BF16), v_refs[i][0].astype(BF16))
    if fox:
        acc_ref[...] = alpha[:, 0:1] * acc_ref[...] + pv
    else:
        acc_ref[...] = acc_ref[...] + pv

    @pl.when(j == pl.num_programs(1) - 1)
    def _():
        acc = acc_ref[...]
        if fox:
            acc = acc / l_ref[:, 0:1]
        own = _iota((8, GROUP_WIDTH), 1) // HEAD_DIM == _iota((8, GROUP_WIDTH), 0)
        o_ref[0] = jnp.sum(jnp.where(own, acc, 0.0), axis=0, keepdims=True)


def _dec_attn(page_table, q, k_new, v_new, small, cache_k, cache_v, cache_lf_t, *, layer, fox):
    nb = q.shape[0]
    n_pages = page_table.shape[1]
    page = cache_k.shape[3]
    assert page == 128, "per-page statistics are kept one page per vreg row group"
    pp = _pick(n_pages, (PAGES_PER_STEP, 8, 4, 2, 1))
    row3 = lambda a: a.reshape(nb, 1, a.shape[-1])
    rspec = lambda w: pl.BlockSpec((1, 1, w), lambda b, j, pt: (b, 0, 0))

    def page_map(i):
        return lambda b, j, pt: (layer, pt[b, n_pages - 1 - (j * pp + i)], 0, 0)

    in_specs = [rspec(GROUP_WIDTH)] * 3 + [rspec(SMALL_WIDTH)]
    in_specs += [pl.BlockSpec((None, 1, GROUP_WIDTH, page), page_map(i)) for i in range(pp)] * 2
    args = [row3(q), row3(k_new), row3(v_new), row3(small)] + [cache_k] * pp + [cache_v] * pp
    if fox:
        in_specs += [pl.BlockSpec((None, 1, N_HEADS, page), page_map(i)) for i in range(pp)]
        args += [cache_lf_t] * pp
    grid_spec = pltpu.PrefetchScalarGridSpec(
        num_scalar_prefetch=1, grid=(nb, n_pages // pp), in_specs=in_specs,
        out_specs=pl.BlockSpec((1, 1, GROUP_WIDTH), lambda b, j, pt: (b, 0, 0)),
        scratch_shapes=[pltpu.VMEM((8, 128), F32), pltpu.VMEM((8, 128), F32), pltpu.VMEM((8, GROUP_WIDTH), F32),
                        pltpu.VMEM((8, 128), F32)])
    out = pl.pallas_call(
        functools.partial(_dec_attn_kernel, fox=fox, pp=pp, page=page), grid_spec=grid_spec,
        out_shape=jax.ShapeDtypeStruct((nb, 1, GROUP_WIDTH), F32),
        compiler_params=pltpu.CompilerParams(dimension_semantics=("arbitrary", "arbitrary"),
                                             vmem_limit_bytes=VMEM_LIMIT),
        name="fox_step" if fox else "sb_step",
    )(page_table, *args)
    return out.reshape(nb, GROUP_WIDTH)


def _column(row, eye):
    return jnp.sum(eye * row, axis=1, keepdims=True)


def _rec_step_kernel(hq_ref, hlf_ref, hk_ref, hi_ref, dx_ref, sm_ref, w_ref, shg_ref, sdn_ref, buf_ref,
                     ohg_ref, odn_ref, shg_o_ref, sdn_o_ref, buf_o_ref):
    eye = jnp.where(_iota((HEAD_DIM, HEAD_DIM), 0) == _iota((HEAD_DIM, HEAD_DIM), 1), 1.0, 0.0)
    hq = hq_ref[0]
    hlf = hlf_ref[0]
    hk = hk_ref[0]
    hv = hi_ref[0]
    sm = sm_ref[0]
    buf = buf_ref[0]
    x_new = dx_ref[0]
    y = x_new * w_ref[CONV_WIDTH - 1:CONV_WIDTH, :]
    for jw in range(CONV_WIDTH - 1):
        y = y + buf[jw:jw + 1, :] * w_ref[jw:jw + 1, :]
    y = _silu(y)
    buf_o_ref[0] = jnp.concatenate([buf[1:CONV_WIDTH - 1, :], x_new], axis=0)
    for h in range(N_HEADS):
        cs = slice(h * HEAD_DIM, (h + 1) * HEAD_DIM)
        s = shg_ref[0, h]
        s = _column(jnp.exp(hlf[:, cs]), eye) * s + _column(hk[:, cs], eye) * hv[:, cs]
        shg_o_ref[0, h] = s
        ohg_ref[0, :, cs] = jnp.sum(_column(hq[:, cs], eye) * s, axis=0, keepdims=True)
        q = y[:, cs]
        k = y[:, GROUP_WIDTH + h * HEAD_DIM:GROUP_WIDTH + (h + 1) * HEAD_DIM]
        v = y[:, 2 * GROUP_WIDTH + h * HEAD_DIM:2 * GROUP_WIDTH + (h + 1) * HEAD_DIM]
        q = q * lax.rsqrt(jnp.sum(q * q, axis=-1, keepdims=True) + NORM_EPS) * QK_SCALE
        k = k * lax.rsqrt(jnp.sum(k * k, axis=-1, keepdims=True) + NORM_EPS)
        beta = sm[:, LANE_BETA + h:LANE_BETA + h + 1]
        a = jnp.exp(sm[:, LANE_GDEC + h:LANE_GDEC + h + 1])
        s = sdn_ref[0, h]
        kc = _column(k, eye)
        v_new = beta * (v - a * jnp.sum(kc * s, axis=0, keepdims=True))
        s = a * s + kc * v_new
        sdn_o_ref[0, h] = s
        odn_ref[0, :, cs] = jnp.sum(_column(q, eye) * s, axis=0, keepdims=True)


def _rec_step(hq, hlf, hk, hi, dqkv, small, conv_w, s_hg, s_dn, buf):
    nb = hq.shape[0]
    row3 = lambda a: a.reshape(nb, 1, a.shape[-1])
    rspec = lambda w: pl.BlockSpec((1, 1, w), lambda b: (b, 0, 0))
    st_spec = pl.BlockSpec((1, N_HEADS, HEAD_DIM, HEAD_DIM), lambda b: (b, 0, 0, 0))
    buf_spec = pl.BlockSpec((1, CONV_WIDTH - 1, 3 * GROUP_WIDTH), lambda b: (b, 0, 0))
    outs = pl.pallas_call(
        _rec_step_kernel, grid=(nb,),
        in_specs=[rspec(GROUP_WIDTH)] * 4 + [rspec(3 * GROUP_WIDTH), rspec(SMALL_WIDTH),
                                             pl.BlockSpec(conv_w.shape, lambda b: (0, 0)), st_spec, st_spec, buf_spec],
        out_specs=[rspec(GROUP_WIDTH), rspec(GROUP_WIDTH), st_spec, st_spec, buf_spec],
        out_shape=[jax.ShapeDtypeStruct((nb, 1, GROUP_WIDTH), F32)] * 2
        + [jax.ShapeDtypeStruct(s_hg.shape, F32), jax.ShapeDtypeStruct(s_dn.shape, F32),
           jax.ShapeDtypeStruct(buf.shape, F32)],
        compiler_params=pltpu.CompilerParams(dimension_semantics=("arbitrary",), vmem_limit_bytes=VMEM_LIMIT),
        name="recurrent_step",
    )(row3(hq), row3(hlf), row3(hk), row3(hi), row3(dqkv), row3(small), conv_w, s_hg, s_dn, buf)
    ohg, odn, s_hg_new, s_dn_new, buf_new = outs
    return ohg.reshape(nb, GROUP_WIDTH), odn.reshape(nb, GROUP_WIDTH), s_hg_new, s_dn_new, buf_new


def _tile_gain(g):
    return jnp.tile(g.astype(F32), N_HEADS)


def _relayout_w_in(w_in_t_l):
    gw = GROUP_WIDTH
    a = 7 * gw
    e = a + N_HEADS + 7 * gw
    pad = jnp.zeros((SMALL_WIDTH - 3 * N_HEADS, w_in_t_l.shape[1]), w_in_t_l.dtype)
    rows = [w_in_t_l[:a], w_in_t_l[a + N_HEADS:e], w_in_t_l[a:a + N_HEADS], w_in_t_l[e:e + 2 * N_HEADS], pad]
    return jnp.concatenate(rows, axis=0).astype(BF16)


def _small_params(f_bias, dt_bias, a_log):
    sp = jnp.zeros((8, SMALL_WIDTH), F32)
    sp = sp.at[0, LANE_FLOG:LANE_FLOG + N_HEADS].set(f_bias.astype(F32))
    sp = sp.at[1, LANE_GDEC:LANE_GDEC + N_HEADS].set(dt_bias.astype(F32))
    sp = sp.at[2, LANE_GDEC:LANE_GDEC + N_HEADS].set(a_log.astype(F32))
    return sp


def _pick(n, candidates):
    for c in candidates:
        if n % c == 0:
            return c
    return n


def kernel(x_prompt, x_sample, cache_fox_k, cache_fox_v, cache_fox_logf, cache_sb_k, cache_sb_v, state_hgrn, state_dn, state_dn_conv, page_table, hgrn_lb_param, w_in, w_out, ln1_g, ln2_g, fox_f_bias, fox_q_norm, fox_k_norm, sb_q_norm, sb_k_norm, hgrn_out_norm, fox_out_norm, sb_out_norm, dn_out_norm, dn_conv_w, dn_dt_bias, dn_a_log, w_up, w_down):
    depth = w_in.shape[0]
    bsz, seq, d = x_prompt.shape
    nb = x_sample.shape[0]
    n_phys, page = cache_fox_k.shape[1], cache_fox_k.shape[2]
    m = bsz * seq
    tm = _pick(seq, (256, 128, 64, 32, 16, 8))
    tm_mlp = _pick(m, (512, 256, 128, 64, 32, 16, 8))
    tf = _pick(w_up.shape[2], (1024, 512, 256, 128))
    tq = _pick(seq, (512, 256, 128))
    tt = _pick(seq, (256, 128, 64))

    yp = x_prompt.reshape(m, d)
    ys = x_sample.reshape(nb, d)
    lbp = hgrn_lb_param.astype(F32)
    w_in_t = jnp.transpose(w_in, (2, 0, 1))
    kv_t = lambda a: a.transpose(0, 1, 3, 4, 2).reshape(depth, n_phys, GROUP_WIDTH, page)
    lf_t = jnp.swapaxes(cache_fox_logf.astype(F32), 2, 3)
    p_out = [[] for _ in range(8)]
    s_out = [[] for _ in range(8)]
    for l in range(depth):
        w_re = _relayout_w_in(w_in_t[:, l, :])
        w_o = w_out[l].astype(BF16)
        w_u = w_up[l].astype(BF16)
        w_d = w_down[l].astype(BF16)
        g1 = ln1_g[l].reshape(1, d).astype(F32)
        g2 = ln2_g[l].reshape(1, d).astype(F32)
        qk_gains = jnp.stack([_tile_gain(fox_q_norm[l]), _tile_gain(fox_k_norm[l]),
                              _tile_gain(sb_q_norm[l]), _tile_gain(sb_k_norm[l])])
        out_gains = jnp.stack([_tile_gain(hgrn_out_norm[l]), _tile_gain(fox_out_norm[l]),
                               _tile_gain(sb_out_norm[l]), _tile_gain(dn_out_norm[l])])
        sp = _small_params(fox_f_bias[l], dn_dt_bias[l], dn_a_log[l])
        conv_w = dn_conv_w[l].astype(F32)

        (hq, hlf, hk, hi, hgate, fq, fk, fv, sq, sk, sv, dqkv, dz, small, cum, small_t, cum_t) = _inproj(
            yp, g1, w_re, lbp, qk_gains, sp, layer=l, tm=tm, rows_per_seq=seq, with_time=True)
        b3 = lambda a: a.reshape(bsz, seq, a.shape[-1])
        o_hg, st_hg = _hgrn_prompt(b3(hq), b3(hlf), b3(hk), b3(hi), tt=tt)
        o_fx = _fox_prompt(b3(fq), fk, fv, b3(cum), cum_t, tq=tq)
        o_sb = _sb_prompt(b3(sq), sk, sv, tq=tq)
        o_dn, st_dn = _gdn_prompt(b3(dqkv), conv_w, b3(small), small_t, tt=tt)
        yp = _outproj(yp, o_hg.reshape(m, -1), hgate, o_fx.reshape(m, -1), o_sb.reshape(m, -1),
                      o_dn.reshape(m, -1), dz, out_gains, w_o, tm=tm)
        yp = _mlp(yp, g2, w_u, w_d, tm=tm_mlp, tf=tf)
        by_head = lambda a: a.reshape(bsz, N_HEADS, HEAD_DIM, seq).transpose(0, 3, 1, 2)
        p_out[0].append(by_head(fk))
        p_out[1].append(by_head(fv))
        p_out[2].append(b3(small)[:, :, LANE_FLOG:LANE_FLOG + N_HEADS])
        p_out[3].append(by_head(sk))
        p_out[4].append(by_head(sv))
        p_out[5].append(jnp.swapaxes(st_hg, -1, -2))
        p_out[6].append(st_dn)
        p_out[7].append(b3(dqkv)[:, seq - (CONV_WIDTH - 1):, :])

        (hq, hlf, hk, hi, hgate, fq, fk, fv, sq, sk, sv, dqkv, dz, small) = _inproj(
            ys, g1, w_re, lbp, qk_gains, sp, layer=l, tm=nb, rows_per_seq=nb, with_time=False)
        o_fx = _dec_attn(page_table, fq, fk, fv, small, kv_t(cache_fox_k), kv_t(cache_fox_v), lf_t,
                         layer=l, fox=True)
        o_sb = _dec_attn(page_table, sq, sk, sv, small, kv_t(cache_sb_k), kv_t(cache_sb_v), None,
                         layer=l, fox=False)
        o_hg, o_dn, s_hg_new, s_dn_new, buf_new = _rec_step(
            hq, hlf, hk, hi, dqkv, small, conv_w, state_hgrn[l].astype(F32), state_dn[l].astype(F32),
            state_dn_conv[l].astype(F32))
        ys = _outproj(ys, o_hg, hgate, o_fx, o_sb, o_dn, dz, out_gains, w_o, tm=nb)
        ys = _mlp(ys, g2, w_u, w_d, tm=nb, tf=tf)
        sshape = (nb, 1, N_HEADS, HEAD_DIM)
        s_out[0].append(fk.reshape(sshape))
        s_out[1].append(fv.reshape(sshape))
        s_out[2].append(small[:, LANE_FLOG:LANE_FLOG + N_HEADS].reshape(nb, 1, N_HEADS))
        s_out[3].append(sk.reshape(sshape))
        s_out[4].append(sv.reshape(sshape))
        s_out[5].append(s_hg_new)
        s_out[6].append(s_dn_new)
        s_out[7].append(buf_new)

    p = [jnp.stack(v) for v in p_out]
    s = [jnp.stack(v) for v in s_out]
    return (yp.reshape(bsz, seq, d), ys.reshape(nb, 1, d), *p, *s)
```

```python
import functools

import jax
import jax.numpy as jnp
from jax import lax
from jax.experimental import pallas as pl
from jax.experimental.pallas import tpu as pltpu

F32 = jnp.float32
BF16 = jnp.bfloat16

HEAD_DIM = 64
N_HEADS = 4
GROUP_WIDTH = N_HEADS * HEAD_DIM
N_SEGMENTS = 14
SMALL_WIDTH = 128
CONV_WIDTH = 4
NORM_EPS = 1e-6
NEG_BIG = -1e30
QK_SCALE = HEAD_DIM ** -0.5
LOG2E = 1.4426950408889634
INV_LN2 = LOG2E
PRUNE_LOG2 = 160.0
BOUND_SLACK = 1.001
VMEM_LIMIT = 56 * 1024 * 1024

LANE_FLOG = 0
LANE_BETA = 4
LANE_GDEC = 8
SMALL_ROWS_T = 16


def _iota(shape, dim):
    return lax.broadcasted_iota(jnp.int32, shape, dim)


def _dot(a, b):
    return jnp.dot(a, b, preferred_element_type=F32)


def _dot_nt(a, b):
    return lax.dot_general(a, b, (((1,), (1,)), ((), ())), preferred_element_type=F32)


def _split3(x):
    hi = x.astype(BF16)
    r = x - hi.astype(F32)
    mid = r.astype(BF16)
    lo = (r - mid.astype(F32)).astype(BF16)
    return hi, mid, lo


def _dot_xc(x, c, parts=3):
    ps = _split3(x)[:parts]
    out = _dot(ps[0], c)
    for p in ps[1:]:
        out = out + _dot(p, c)
    return out


def _dot_cx(c, x, parts=3):
    ps = _split3(x)[:parts]
    out = _dot(c, ps[0])
    for p in ps[1:]:
        out = out + _dot(c, p)
    return out


def _dot_f32(a, b):
    ah = a.astype(BF16)
    al = (a - ah.astype(F32)).astype(BF16)
    bh = b.astype(BF16)
    bl = (b - bh.astype(F32)).astype(BF16)
    return _dot(ah, bh) + _dot(ah, bl) + _dot(al, bh)


def _head_ones(n=GROUP_WIDTH):
    return (_iota((n, n), 0) // HEAD_DIM == _iota((n, n), 1) // HEAD_DIM).astype(BF16)


def _head_sum(x, ones):
    return _dot_xc(x, ones)


def _head_rms(x, ones, gain):
    ms = _head_sum(x * x, ones) * (1.0 / HEAD_DIM)
    return x * lax.rsqrt(ms + NORM_EPS) * gain


def _log_sigmoid(x):
    return jnp.minimum(x, 0.0) - jnp.log1p(jnp.exp(-jnp.abs(x)))


def _softplus(x):
    return jnp.maximum(x, 0.0) + jnp.log1p(jnp.exp(-jnp.abs(x)))


def _sigmoid(x):
    return 1.0 / (1.0 + jnp.exp(-x))


def _silu(x):
    return x * _sigmoid(x)


def _inproj_kernel(x_ref, g1_ref, w_ref, lbp_ref, gains_ref, sp_ref, *refs, layer, depth, tiles_per_seq,
                   with_time):
    (hq_ref, hlf_ref, hk_ref, hi_ref, hg_ref, fq_ref, fk_ref, fv_ref, sq_ref, sk_ref, sv_ref,
     dqkv_ref, dz_ref, small_ref) = refs[:14]
    x = x_ref[...]
    h = (x * lax.rsqrt(jnp.mean(x * x, axis=-1, keepdims=True) + NORM_EPS) * g1_ref[...]).astype(BF16)

    def seg(j, width=GROUP_WIDTH):
        return _dot_nt(h, w_ref[j * GROUP_WIDTH:j * GROUP_WIDTH + width, :])

    ones = _head_ones()

    rows = [lbp_ref[i:i + 1, :] for i in range(depth)]
    mx = functools.reduce(jnp.maximum, rows)
    es = [jnp.exp(r - mx) for r in rows]
    lb = sum(es[1:layer + 1], jnp.zeros_like(mx)) / sum(es)
    hq_ref[...] = seg(0)
    hf = seg(1)
    a = jnp.log(lb)
    b = jnp.log1p(-lb) + _log_sigmoid(hf)
    hi = jnp.maximum(a, b)
    lo = jnp.minimum(a, b)
    hlf_ref[...] = hi + jnp.log1p(jnp.exp(lo - hi))
    hk_ref[...] = (1.0 - lb) * _sigmoid(-hf)
    hi_ref[...] = seg(2)
    hg_ref[...] = _silu(seg(3))

    fq_ref[...] = _head_rms(seg(4), ones, gains_ref[0:1, :]) * QK_SCALE
    sq_ref[...] = _head_rms(seg(7), ones, gains_ref[2:3, :]) * QK_SCALE
    kv = (_head_rms(seg(5), ones, gains_ref[1:2, :]), seg(6), _head_rms(seg(8), ones, gains_ref[3:4, :]), seg(9))
    for ref, val in zip((fk_ref, fv_ref, sk_ref, sv_ref), kv):
        if with_time:
            ref[0] = val.T
        else:
            ref[...] = val

    dqkv_ref[...] = seg(10, 3 * GROUP_WIDTH)
    dz_ref[...] = _silu(seg(13))

    s = _dot_nt(h, w_ref[N_SEGMENTS * GROUP_WIDTH:, :])
    lane = _iota(s.shape, 1)
    f_log = _log_sigmoid(s + sp_ref[0:1, :])
    beta = _sigmoid(s)
    g_dec = -jnp.exp(sp_ref[2:3, :]) * _softplus(s + sp_ref[1:2, :])
    small = jnp.where(lane < LANE_BETA, f_log, jnp.where(lane < LANE_GDEC, beta, g_dec))
    small_ref[...] = small

    if with_time:
        cum_ref, small_t_ref, cum_t_ref, carry_ref = refs[14:]
        tm = s.shape[0]

        @pl.when(pl.program_id(0) % tiles_per_seq == 0)
        def _():
            carry_ref[...] = jnp.zeros_like(carry_ref)

        tril = (_iota((tm, tm), 1) <= _iota((tm, tm), 0)).astype(BF16)
        cum = _dot_cx(tril, small) + carry_ref[0:1, :]
        cum_ref[...] = cum
        carry_ref[...] = jnp.broadcast_to(cum[tm - 1:tm, :], carry_ref.shape)
        small_t_ref[0] = small.T[:SMALL_ROWS_T, :]
        cum_t_ref[0] = cum.T[:SMALL_ROWS_T, :]


def _inproj(x2, g1, w_re, lbp, gains, sp, *, layer, tm, rows_per_seq, with_time):
    m, d = x2.shape
    depth = lbp.shape[0]
    grid = (m // tm,)
    row = lambda i: (i, 0)
    const = lambda i: (0, 0)
    tps = rows_per_seq // tm
    nseq = m // rows_per_seq
    seg_shape = jax.ShapeDtypeStruct((m, GROUP_WIDTH), F32)
    seg_spec = pl.BlockSpec((tm, GROUP_WIDTH), row)
    out_shape = [seg_shape] * 11 + [jax.ShapeDtypeStruct((m, 3 * GROUP_WIDTH), F32), seg_shape,
                                    jax.ShapeDtypeStruct((m, SMALL_WIDTH), F32)]
    out_specs = [seg_spec] * 11 + [pl.BlockSpec((tm, 3 * GROUP_WIDTH), row), seg_spec,
                                   pl.BlockSpec((tm, SMALL_WIDTH), row)]
    scratch = []
    if with_time:
        def by_time(rows):
            return (jax.ShapeDtypeStruct((nseq, rows, rows_per_seq), F32),
                    pl.BlockSpec((1, rows, tm), lambda i: (i // tps, 0, i % tps)))
        for idx in (6, 7, 9, 10):
            out_shape[idx], out_specs[idx] = by_time(GROUP_WIDTH)
        out_shape += [jax.ShapeDtypeStruct((m, SMALL_WIDTH), F32)]
        out_specs += [pl.BlockSpec((tm, SMALL_WIDTH), row)]
        for _ in range(2):
            sh, sp_ = by_time(SMALL_ROWS_T)
            out_shape.append(sh)
            out_specs.append(sp_)
        scratch = [pltpu.VMEM((8, SMALL_WIDTH), F32)]
    kern = functools.partial(_inproj_kernel, layer=layer, depth=depth, tiles_per_seq=tps,
                             with_time=with_time)
    return pl.pallas_call(
        kern, grid=grid,
        in_specs=[pl.BlockSpec((tm, d), row), pl.BlockSpec((1, d), const), pl.BlockSpec(w_re.shape, const),
                  pl.BlockSpec(lbp.shape, const), pl.BlockSpec(gains.shape, const), pl.BlockSpec(sp.shape, const)],
        out_specs=out_specs, out_shape=out_shape, scratch_shapes=scratch,
        compiler_params=pltpu.CompilerParams(dimension_semantics=("arbitrary",), vmem_limit_bytes=VMEM_LIMIT),
        name="inproj_time" if with_time else "inproj_step",
    )(x2, g1, w_re, lbp, gains, sp)


HGRN_SUB = 16


def _hgrn_kernel(q_ref, lf_ref, k_ref, v_ref, o_ref, st_ref, s_ref, oi_ref, *, tt):
    t = pl.program_id(1)

    @pl.when(t == 0)
    def _():
        s_ref[...] = jnp.zeros_like(s_ref)

    q = q_ref[0]
    lf = lf_ref[0]
    kin = k_ref[0]
    v = v_ref[0]
    r = _iota((tt, tt), 0)
    c = _iota((tt, tt), 1)
    same = (r // HGRN_SUB) == (c // HGRN_SUB)
    g = _dot_cx(jnp.where(same & (c <= r), 1.0, 0.0).astype(BF16), lf)
    gl = _dot_cx(jnp.where(same, 1.0, 0.0).astype(BF16), lf)
    qg = q * jnp.exp(g)
    kg = kin * jnp.exp(gl - g)

    ones = _head_ones()
    rowmod = _iota((tt, GROUP_WIDTH), 0) % HGRN_SUB
    o = jnp.zeros((tt, GROUP_WIDTH), F32)
    for d in range(HGRN_SUB):
        if d == 0:
            kd, gd, vd = kin, g, v
        else:
            kd = pltpu.roll(kin, d, 0)
            gd = pltpu.roll(g, d, 0)
            vd = pltpu.roll(v, d, 0)
        e = jnp.where(rowmod >= d, g - gd, NEG_BIG)
        p = q * kd * jnp.exp(e)
        o = o + _dot_xc(p, ones, parts=2) * vd

    v_t = v.T
    for i in range(tt // HGRN_SUB):
        rs = slice(i * HGRN_SUB, (i + 1) * HGRN_SUB)
        for h in range(N_HEADS):
            cs = slice(h * HEAD_DIM, (h + 1) * HEAD_DIM)
            s = s_ref[h]
            oi_ref[rs, cs] = _dot_nt(qg[rs, cs].astype(BF16), s.astype(BF16))
            dec = jnp.exp(gl[i * HGRN_SUB:i * HGRN_SUB + 1, cs])
            s_ref[h] = dec * s + _dot(v_t[cs, rs].astype(BF16), kg[rs, cs].astype(BF16))
    o_ref[0] = o + oi_ref[...]

    @pl.when(t == pl.num_programs(1) - 1)
    def _():
        st_ref[0] = s_ref[...]


def _hgrn_prompt(hq, hlf, hk, hi, *, tt):
    b, t, _ = hq.shape
    blk = pl.BlockSpec((1, tt, GROUP_WIDTH), lambda i, j: (i, j, 0))
    return pl.pallas_call(
        functools.partial(_hgrn_kernel, tt=tt), grid=(b, t // tt),
        in_specs=[blk] * 4,
        out_specs=[blk, pl.BlockSpec((1, N_HEADS, HEAD_DIM, HEAD_DIM), lambda i, j: (i, 0, 0, 0))],
        out_shape=[jax.ShapeDtypeStruct((b, t, GROUP_WIDTH), F32),
                   jax.ShapeDtypeStruct((b, N_HEADS, HEAD_DIM, HEAD_DIM), F32)],
        scratch_shapes=[pltpu.VMEM((N_HEADS, HEAD_DIM, HEAD_DIM), F32), pltpu.VMEM((tt, GROUP_WIDTH), F32)],
        compiler_params=pltpu.CompilerParams(dimension_semantics=("arbitrary", "arbitrary"),
                                             vmem_limit_bytes=VMEM_LIMIT),
        name="hgrn_prompt",
    )(hq, hlf, hk, hi)


def _fox_kernel(q_ref, k_ref, v_ref, fq_ref, fk_ref, o_ref, qa_ref, qn_ref, m_ref, acc_ref, kn_ref, fk_min_ref,
                fq_max_ref, *, tq):
    qi = pl.program_id(1)
    j = pl.program_id(2)

    @pl.when(j == 0)
    def _():
        m_ref[...] = jnp.full_like(m_ref, NEG_BIG)
        acc_ref[...] = jnp.zeros_like(acc_ref)
        q = q_ref[0] * LOG2E
        f = fq_ref[0] * LOG2E
        qb = q.astype(BF16).astype(F32)
        qn_ref[...] = _head_sum(qb * qb, _head_ones())
        lane = _iota((tq, HEAD_DIM), 1)
        for h in range(N_HEADS):
            hi, mid, lo = [p.astype(F32) for p in _split3(f[:, LANE_FLOG + h:LANE_FLOG + h + 1])]
            ext = jnp.where(lane == 0, hi, jnp.where(lane == 1, mid, jnp.where(lane == 2, lo,
                                                                               jnp.where(lane < 6, 1.0, 0.0))))
            qa_ref[h] = jnp.concatenate([q[:, h * HEAD_DIM:(h + 1) * HEAD_DIM], ext], axis=1).astype(BF16)

        kf = k_ref[0].astype(BF16).astype(F32)
        k2 = kf * kf
        fk = fk_ref[0] * LOG2E
        for h in range(N_HEADS):
            fq_max_ref[h] = jnp.max(f[:, LANE_FLOG + h:LANE_FLOG + h + 1])
            fk_min_ref[qi * N_HEADS + h] = jnp.min(fk[LANE_FLOG + h:LANE_FLOG + h + 1, :])
            kn_ref[qi * N_HEADS + h] = jnp.max(jnp.sum(k2[h * HEAD_DIM:(h + 1) * HEAD_DIM, :], axis=0, keepdims=True))

    def block_matters():
        worst = []
        for h in range(N_HEADS):
            slot = (qi - j) * N_HEADS + h
            z_cap = jnp.sqrt(qn_ref[:, h * HEAD_DIM:h * HEAD_DIM + 1] * kn_ref[slot]) * BOUND_SLACK
            worst.append(jnp.max(z_cap - m_ref[h][:, 0:1]) + (fq_max_ref[h] - fk_min_ref[slot]))
        return functools.reduce(jnp.maximum, worst) > -PRUNE_LOG2

    def step(masked):
        kf = k_ref[0]
        vf = v_ref[0]
        fk = fk_ref[0] * LOG2E
        row = _iota((8, tq), 0)
        pad = jnp.zeros((HEAD_DIM - 8, tq), F32)
        v_ext = jnp.concatenate([jnp.where(row == 0, 1.0, 0.0), pad], axis=0)
        if masked:
            keep = _iota((tq, tq), 1) <= _iota((tq, tq), 0)
        heads = range(N_HEADS)
        ss = []
        for h in heads:
            cs = slice(h * HEAD_DIM, (h + 1) * HEAD_DIM)
            hi, mid, lo = [p.astype(F32) for p in _split3(fk[LANE_FLOG + h:LANE_FLOG + h + 1, :])]
            k_ext = jnp.where(row < 3, 1.0, jnp.where(row == 3, -hi, jnp.where(row == 4, -mid,
                                                                               jnp.where(row == 5, -lo, 0.0))))
            ka = jnp.concatenate([kf[cs, :], k_ext, pad], axis=0).astype(BF16)
            ss.append(_dot(qa_ref[h], ka))
        if masked:
            ss = [jnp.where(keep, s, NEG_BIG) for s in ss]
        m_prevs = [m_ref[h] for h in heads]
        m_news = [jnp.maximum(m_prevs[h], jnp.max(ss[h], axis=-1, keepdims=True)) for h in heads]
        ps = [jnp.exp2(ss[h] - jnp.concatenate([m_news[h]] * (tq // 128), axis=1)) for h in heads]
        for h in heads:
            va = jnp.concatenate([vf[h * HEAD_DIM:(h + 1) * HEAD_DIM, :], v_ext], axis=0).astype(BF16)
            m_ref[h] = m_news[h]
            acc_ref[h] = jnp.exp2(m_prevs[h] - m_news[h]) * acc_ref[h] + _dot_nt(ps[h].astype(BF16), va)

    @pl.when(j == 0)
    def _():
        step(True)

    @pl.when((j > 0) & (j <= qi))
    def _():
        @pl.when(block_matters())
        def _():
            step(False)

    @pl.when(j == qi)
    def _():
        for h in range(N_HEADS):
            acc = acc_ref[h]
            o_ref[0, :, h * HEAD_DIM:(h + 1) * HEAD_DIM] = acc[:, :HEAD_DIM] / acc[:, HEAD_DIM:HEAD_DIM + 1]


def _fox_prompt(fq, fk, fv, cum, cum_t, *, tq):
    b, t, _ = fq.shape
    n = t // tq
    qspec = pl.BlockSpec((1, tq, GROUP_WIDTH), lambda i, qi, j: (i, qi, 0))
    kspec = pl.BlockSpec((1, GROUP_WIDTH, tq), lambda i, qi, j: (i, 0, jnp.maximum(qi - j, 0)))
    return pl.pallas_call(
        functools.partial(_fox_kernel, tq=tq), grid=(b, n, n),
        in_specs=[qspec, kspec, kspec,
                  pl.BlockSpec((1, tq, SMALL_WIDTH), lambda i, qi, j: (i, qi, 0)),
                  pl.BlockSpec((1, SMALL_ROWS_T, tq), lambda i, qi, j: (i, 0, jnp.maximum(qi - j, 0)))],
        out_specs=qspec,
        out_shape=jax.ShapeDtypeStruct((b, t, GROUP_WIDTH), F32),
        scratch_shapes=[pltpu.VMEM((N_HEADS, tq, 2 * HEAD_DIM), BF16), pltpu.VMEM((tq, GROUP_WIDTH), F32),
                        pltpu.VMEM((N_HEADS, tq, 128), F32), pltpu.VMEM((N_HEADS, tq, 2 * HEAD_DIM), F32),
                        pltpu.SMEM((n * N_HEADS,), F32), pltpu.SMEM((n * N_HEADS,), F32),
                        pltpu.SMEM((N_HEADS,), F32)],
        compiler_params=pltpu.CompilerParams(dimension_semantics=("arbitrary", "arbitrary", "arbitrary"),
                                             vmem_limit_bytes=VMEM_LIMIT),
        name="fox_prompt",
    )(fq, fk, fv, cum, cum_t)


SB_SUB = 256


def _sb_kernel(q_ref, k_ref, v_ref, o_ref, qb_ref, qn_ref, carry_ref, acc_ref, kn_ref, *, tq):
    qi = pl.program_id(1)
    j = pl.program_id(2)

    heads = range(N_HEADS)
    rows = lambda h: slice(h * HEAD_DIM, (h + 1) * HEAD_DIM)

    @pl.when(j == 0)
    def _():
        carry_ref[...] = jnp.zeros_like(carry_ref)
        acc_ref[...] = jnp.zeros_like(acc_ref)
        qb = (q_ref[0] * LOG2E).astype(BF16)
        qb_ref[...] = qb
        qn_ref[...] = _head_sum(qb.astype(F32) * qb.astype(F32), _head_ones())
        kf = k_ref[0].astype(BF16).astype(F32)
        k2 = kf * kf
        for h in heads:
            kn_ref[qi * N_HEADS + h] = jnp.max(jnp.sum(k2[rows(h), :], axis=0, keepdims=True))

    def block_matters():
        worst = []
        for h in heads:
            kn2 = kn_ref[(qi - j) * N_HEADS + h]
            z_cap = jnp.sqrt(qn_ref[:, h * HEAD_DIM:h * HEAD_DIM + 1] * kn2) * BOUND_SLACK
            worst.append(jnp.max(z_cap - carry_ref[h][:, 0:1]))
        return functools.reduce(jnp.maximum, worst) > -PRUNE_LOG2

    def step(masked):
        k = k_ref[0].astype(BF16)
        v = v_ref[0].astype(BF16)
        r = _iota((tq, tq), 0)
        c = _iota((tq, tq), 1)
        sub = min(SB_SUB, tq)
        suffix = jnp.where(_iota((sub, sub), 1) <= _iota((sub, sub), 0), 1.0, 0.0).astype(BF16)
        z2s = [_dot(qb_ref[:, rows(h)], k[rows(h), :]) for h in heads]
        sps = [jnp.maximum(z2, 0.0) + jnp.log(1.0 + jnp.exp2(-jnp.abs(z2))) * INV_LN2 for z2 in z2s]
        if masked:
            sps = [jnp.where(c < r, sp, 0.0) for sp in sps]
        carries = [carry_ref[h] for h in heads]
        cum_parts = [[None] * (tq // sub) for _ in heads]
        for part in reversed(range(tq // sub)):
            ks = slice(part * sub, (part + 1) * sub)
            for h in heads:
                wide = jnp.concatenate([carries[h]] * (sub // 128), axis=1)
                cum = _dot_xc(sps[h][:, ks], suffix, parts=2) + wide
                cum_parts[h][part] = cum
                carries[h] = jnp.broadcast_to(cum[:, 0:1], carries[h].shape)
        es = [z2s[h] - jnp.concatenate(cum_parts[h], axis=1) for h in heads]
        if masked:
            es = [jnp.where(c < r, e, NEG_BIG) for e in es]
        pad = jnp.zeros((HEAD_DIM, tq), BF16)
        for h in heads:
            va = jnp.concatenate([v[rows(h), :], pad], axis=0)
            acc_ref[h] = acc_ref[h] + _dot_nt(jnp.exp2(es[h]).astype(BF16), va)
            carry_ref[h] = carries[h]

    @pl.when(j == 0)
    def _():
        step(True)

    @pl.when((j > 0) & (j <= qi))
    def _():
        @pl.when(block_matters())
        def _():
            step(False)

    @pl.when(j == qi)
    def _():
        for h in range(N_HEADS):
            o_ref[0, :, h * HEAD_DIM:(h + 1) * HEAD_DIM] = acc_ref[h][:, :HEAD_DIM]


def _sb_prompt(sq, sk, sv, *, tq):
    b, t, _ = sq.shape
    n = t // tq
    qspec = pl.BlockSpec((1, tq, GROUP_WIDTH), lambda i, qi, j: (i, qi, 0))
    kspec = pl.BlockSpec((1, GROUP_WIDTH, tq), lambda i, qi, j: (i, 0, jnp.maximum(qi - j, 0)))
    return pl.pallas_call(
        functools.partial(_sb_kernel, tq=tq), grid=(b, n, n),
        in_specs=[qspec, kspec, kspec],
        out_specs=qspec,
        out_shape=jax.ShapeDtypeStruct((b, t, GROUP_WIDTH), F32),
        scratch_shapes=[pltpu.VMEM((tq, GROUP_WIDTH), BF16), pltpu.VMEM((tq, GROUP_WIDTH), F32),
                        pltpu.VMEM((N_HEADS, tq, 128), F32), pltpu.VMEM((N_HEADS, tq, 2 * HEAD_DIM), F32),
                        pltpu.SMEM((n * N_HEADS,), F32)],
        compiler_params=pltpu.CompilerParams(dimension_semantics=("arbitrary", "arbitrary", "arbitrary"),
                                             vmem_limit_bytes=VMEM_LIMIT),
        name="sb_prompt",
    )(sq, sk, sv)


GDN_CHUNK = 128
GDN_BASE = 16


def _unit_lower_inverses(lmats, ii, jj):
    ns = [jnp.where(ii // GDN_BASE == jj // GDN_BASE, -lm, 0.0) for lm in lmats]
    eye = jnp.where(ii == jj, 1.0, 0.0)
    ts = [eye + n for n in ns]
    ps = ns
    for _ in range(GDN_BASE.bit_length() - 2):
        ps = [_dot_f32(p, p) for p in ps]
        ts = [t + _dot_f32(t, p) for t, p in zip(ts, ps)]
    b = GDN_BASE
    while b < GDN_CHUNK:
        lower_left = (ii // (2 * b) == jj // (2 * b)) & (ii // b != jj // b)
        mids = [_dot_f32(t, jnp.where(lower_left, lm, 0.0)) for t, lm in zip(ts, lmats)]
        ts = [t - _dot_f32(mid, t) for t, mid in zip(ts, mids)]
        b *= 2
    return ts


def _gdn_prep_kernel(x_ref, w_ref, sm_ref, smt_ref, u_ref, wk_ref, qe_ref, attn_ref, kdt_ref, gct_ref, ext_ref, *, tt):
    t = pl.program_id(1)

    @pl.when(t == 0)
    def _():
        ext_ref[0:8, :] = jnp.zeros((8, 3 * GROUP_WIDTH), F32)

    @pl.when(t > 0)
    def _():
        ext_ref[0:8, :] = ext_ref[tt:tt + 8, :]

    ext_ref[8:8 + tt, :] = x_ref[0]
    y = ext_ref[pl.ds(8 - (CONV_WIDTH - 1), tt), :] * w_ref[0:1, :]
    for jw in range(1, CONV_WIDTH):
        y = y + ext_ref[pl.ds(8 - (CONV_WIDTH - 1) + jw, tt), :] * w_ref[jw:jw + 1, :]
    y = _silu(y)
    ones = _head_ones()
    q = y[:, :GROUP_WIDTH]
    k = y[:, GROUP_WIDTH:2 * GROUP_WIDTH]
    v = y[:, 2 * GROUP_WIDTH:]
    q = q * lax.rsqrt(_head_sum(q * q, ones) + NORM_EPS) * QK_SCALE
    k = k * lax.rsqrt(_head_sum(k * k, ones) + NORM_EPS)
    sm = sm_ref[0]
    smt = smt_ref[0]
    r = _iota((tt, tt), 0)
    c = _iota((tt, tt), 1)
    same = (r // GDN_CHUNK) == (c // GDN_CHUNK)
    gc_col = _dot_cx(jnp.where(same & (c <= r), 1.0, 0.0).astype(BF16), sm)
    gc_row = _dot_xc(smt, jnp.where(same & (r <= c), 1.0, 0.0).astype(BF16))
    gct_ref[0] = gc_row
    k_t = k.T
    ii = _iota((GDN_CHUNK, GDN_CHUNK), 0)
    jj = _iota((GDN_CHUNK, GDN_CHUNK), 1)
    n_chunks = tt // GDN_CHUNK
    lmats, rhss = [], []
    for ci in range(n_chunks):
        rs = slice(ci * GDN_CHUNK, (ci + 1) * GDN_CHUNK)
        qes, attns, kdts = [], [], []
        for h in range(N_HEADS):
            cs = slice(h * HEAD_DIM, (h + 1) * HEAD_DIM)
            gcol = gc_col[rs, LANE_GDEC + h:LANE_GDEC + h + 1]
            grow = gc_row[LANE_GDEC + h:LANE_GDEC + h + 1, rs]
            beta = sm[rs, LANE_BETA + h:LANE_BETA + h + 1]
            dec = jnp.exp(jnp.where(ii >= jj, gcol - grow, NEG_BIG))
            qh = q[rs, cs].astype(BF16)
            kth = k_t[cs, rs]
            kb = k[rs, cs] * beta
            eg = jnp.exp(gcol)
            lmats.append(jnp.where(ii > jj, _dot(kb.astype(BF16), kth.astype(BF16)) * dec, 0.0))
            rhss.append(jnp.concatenate([v[rs, cs] * beta, kb * eg], axis=1))
            qes.append(q[rs, cs] * eg)
            attns.append(_dot(qh, kth.astype(BF16)) * dec)
            kdts.append(kth * jnp.exp(grow[:, GDN_CHUNK - 1:GDN_CHUNK] - grow))
        qe_ref[0, rs, :] = jnp.concatenate(qes, axis=1).astype(BF16)
        attn_ref[0, rs, :] = jnp.concatenate(attns, axis=1).astype(BF16)
        kdt_ref[0, :, rs] = jnp.concatenate(kdts, axis=0).astype(BF16)
    tinvs = _unit_lower_inverses(lmats, ii, jj)
    sols = [_dot_f32(ti, rhs) for ti, rhs in zip(tinvs, rhss)]
    for ci in range(n_chunks):
        rs = slice(ci * GDN_CHUNK, (ci + 1) * GDN_CHUNK)
        chunk = sols[ci * N_HEADS:(ci + 1) * N_HEADS]
        u_ref[0, rs, :] = jnp.concatenate([sol[:, :HEAD_DIM] for sol in chunk], axis=1)
        wk_ref[0, rs, :] = jnp.concatenate([sol[:, HEAD_DIM:] for sol in chunk], axis=1).astype(BF16)


def _gdn_scan_kernel(u_ref, wk_ref, qe_ref, attn_ref, kdt_ref, gct_ref, o_ref, st_ref, s_ref, *, tt, nb):
    t = pl.program_id(0)

    @pl.when(t == 0)
    def _():
        s_ref[...] = jnp.zeros_like(s_ref)

    for ci in range(tt // GDN_CHUNK):
        rs = slice(ci * GDN_CHUNK, (ci + 1) * GDN_CHUNK)
        items = [(b, h) for b in range(nb) for h in range(N_HEADS)]
        cs = lambda h: slice(h * HEAD_DIM, (h + 1) * HEAD_DIM)
        us = [u_ref[b, rs, :] for b in range(nb)]
        wks = [wk_ref[b, rs, :] for b in range(nb)]
        qes = [qe_ref[b, rs, :] for b in range(nb)]
        attns = [attn_ref[b, rs, :] for b in range(nb)]
        kdts = [kdt_ref[b, :, rs] for b in range(nb)]
        decays = [jnp.exp(gct_ref[b, :, rs][:, GDN_CHUNK - 1:GDN_CHUNK]) for b in range(nb)]
        ss = [s_ref[b, h] for b, h in items]
        sbs = [s.astype(BF16) for s in ss]
        v_news = [us[b][:, cs(h)] - _dot(wks[b][:, cs(h)], sb) for (b, h), sb in zip(items, sbs)]
        vbs = [vn.astype(BF16) for vn in v_news]
        for (b, h), s, vb in zip(items, ss, vbs):
            a = decays[b][LANE_GDEC + h:LANE_GDEC + h + 1, :]
            s_ref[b, h] = a * s + _dot(kdts[b][cs(h), :], vb)
        for b in range(nb):
            outs = [_dot(qes[b][:, cs(h)], sbs[b * N_HEADS + h])
                    + _dot(attns[b][:, h * GDN_CHUNK:(h + 1) * GDN_CHUNK], vbs[b * N_HEADS + h])
                    for h in range(N_HEADS)]
            o_ref[b, rs, :] = jnp.concatenate(outs, axis=1)

    @pl.when(t == pl.num_programs(0) - 1)
    def _():
        st_ref[...] = s_ref[...]


def _gdn_prompt(dqkv, conv_w, small, small_t, *, tt):
    b, t, _ = dqkv.shape
    n_attn = N_HEADS * GDN_CHUNK
    by_rows = lambda w: pl.BlockSpec((1, tt, w), lambda i, j: (i, j, 0))
    by_cols = lambda r: pl.BlockSpec((1, r, tt), lambda i, j: (i, 0, j))
    u, wk, qe, attn, kdt, gct = pl.pallas_call(
        functools.partial(_gdn_prep_kernel, tt=tt), grid=(b, t // tt),
        in_specs=[by_rows(3 * GROUP_WIDTH), pl.BlockSpec(conv_w.shape, lambda i, j: (0, 0)),
                  by_rows(SMALL_WIDTH), by_cols(SMALL_ROWS_T)],
        out_specs=[by_rows(GROUP_WIDTH), by_rows(GROUP_WIDTH), by_rows(GROUP_WIDTH), by_rows(n_attn),
                   by_cols(GROUP_WIDTH), by_cols(SMALL_ROWS_T)],
        out_shape=[jax.ShapeDtypeStruct((b, t, GROUP_WIDTH), F32), jax.ShapeDtypeStruct((b, t, GROUP_WIDTH), BF16),
                   jax.ShapeDtypeStruct((b, t, GROUP_WIDTH), BF16), jax.ShapeDtypeStruct((b, t, n_attn), BF16),
                   jax.ShapeDtypeStruct((b, GROUP_WIDTH, t), BF16), jax.ShapeDtypeStruct((b, SMALL_ROWS_T, t), F32)],
        scratch_shapes=[pltpu.VMEM((tt + 8, 3 * GROUP_WIDTH), F32)],
        compiler_params=pltpu.CompilerParams(dimension_semantics=("arbitrary", "arbitrary"),
                                             vmem_limit_bytes=VMEM_LIMIT),
        name="gdn_prep",
    )(dqkv, conv_w, small, small_t)
    all_rows = lambda w: pl.BlockSpec((b, tt, w), lambda j: (0, j, 0))
    all_cols = lambda r: pl.BlockSpec((b, r, tt), lambda j: (0, 0, j))
    st_spec = pl.BlockSpec((b, N_HEADS, HEAD_DIM, HEAD_DIM), lambda j: (0, 0, 0, 0))
    return pl.pallas_call(
        functools.partial(_gdn_scan_kernel, tt=tt, nb=b), grid=(t // tt,),
        in_specs=[all_rows(GROUP_WIDTH), all_rows(GROUP_WIDTH), all_rows(GROUP_WIDTH), all_rows(n_attn),
                  all_cols(GROUP_WIDTH), all_cols(SMALL_ROWS_T)],
        out_specs=[all_rows(GROUP_WIDTH), st_spec],
        out_shape=[jax.ShapeDtypeStruct((b, t, GROUP_WIDTH), F32),
                   jax.ShapeDtypeStruct((b, N_HEADS, HEAD_DIM, HEAD_DIM), F32)],
        scratch_shapes=[pltpu.VMEM((b, N_HEADS, HEAD_DIM, HEAD_DIM), F32)],
        compiler_params=pltpu.CompilerParams(dimension_semantics=("arbitrary",), vmem_limit_bytes=VMEM_LIMIT),
        name="gdn_scan",
    )(u, wk, qe, attn, kdt, gct)


def _outproj_kernel(x_ref, ohg_ref, hg_ref, ofx_ref, osb_ref, odn_ref, dz_ref, gains_ref, w_ref, y_ref):
    ones = _head_ones()
    parts = [
        _head_rms(ohg_ref[...], ones, gains_ref[0:1, :]) * hg_ref[...],
        _head_rms(ofx_ref[...], ones, gains_ref[1:2, :]),
        _head_rms(osb_ref[...], ones, gains_ref[2:3, :]),
        _head_rms(odn_ref[...], ones, gains_ref[3:4, :]) * dz_ref[...],
    ]
    y = x_ref[...]
    for gidx, p in enumerate(parts):
        y = y + _dot(p.astype(BF16), w_ref[gidx * GROUP_WIDTH:(gidx + 1) * GROUP_WIDTH, :])
    y_ref[...] = y


def _outproj(x2, ohg, hgate, ofx, osb, odn, dz, gains, w_out, *, tm):
    m, d = x2.shape
    row = lambda i: (i, 0)
    const = lambda i: (0, 0)
    seg = pl.BlockSpec((tm, GROUP_WIDTH), row)
    return pl.pallas_call(
        _outproj_kernel, grid=(m // tm,),
        in_specs=[pl.BlockSpec((tm, d), row)] + [seg] * 6 + [pl.BlockSpec(gains.shape, const),
                                                            pl.BlockSpec(w_out.shape, const)],
        out_specs=pl.BlockSpec((tm, d), row),
        out_shape=jax.ShapeDtypeStruct((m, d), F32),
        compiler_params=pltpu.CompilerParams(dimension_semantics=("arbitrary",), vmem_limit_bytes=VMEM_LIMIT),
        name="outproj",
    )(x2, ohg, hgate, ofx, osb, odn, dz, gains, w_out)


def _mlp_kernel(x_ref, g2_ref, wu_ref, wd_ref, y_ref, h_ref, acc_ref):
    f = pl.program_id(1)

    @pl.when(f == 0)
    def _():
        x = x_ref[...]
        h_ref[...] = (x * lax.rsqrt(jnp.mean(x * x, axis=-1, keepdims=True) + NORM_EPS) * g2_ref[...]).astype(BF16)
        acc_ref[...] = x

    u = jnp.maximum(_dot(h_ref[...], wu_ref[...]), 0.0)
    acc_ref[...] += _dot((u * u).astype(BF16), wd_ref[...])

    @pl.when(f == pl.num_programs(1) - 1)
    def _():
        y_ref[...] = acc_ref[...]


def _mlp(x2, g2, w_up, w_down, *, tm, tf):
    m, d = x2.shape
    dff = w_up.shape[1]
    return pl.pallas_call(
        _mlp_kernel, grid=(m // tm, dff // tf),
        in_specs=[pl.BlockSpec((tm, d), lambda i, f: (i, 0)), pl.BlockSpec((1, d), lambda i, f: (0, 0)),
                  pl.BlockSpec((d, tf), lambda i, f: (0, f)), pl.BlockSpec((tf, d), lambda i, f: (f, 0))],
        out_specs=pl.BlockSpec((tm, d), lambda i, f: (i, 0)),
        out_shape=jax.ShapeDtypeStruct((m, d), F32),
        scratch_shapes=[pltpu.VMEM((tm, d), BF16), pltpu.VMEM((tm, d), F32)],
        compiler_params=pltpu.CompilerParams(dimension_semantics=("arbitrary", "arbitrary"),
                                             vmem_limit_bytes=VMEM_LIMIT),
        name="mlp",
    )(x2, g2, w_up, w_down)


PAGES_PER_STEP = 16


def _head_rows(row):
    x = jnp.broadcast_to(row, (8, GROUP_WIDTH))
    return jnp.where(_iota((8, GROUP_WIDTH), 1) // HEAD_DIM == _iota((8, GROUP_WIDTH), 0), x, 0.0)


def _dec_attn_kernel(pt_ref, q_ref, kn_ref, vn_ref, sn_ref, *refs, fox, pp, page):
    k_refs = refs[:pp]
    v_refs = refs[pp:2 * pp]
    rest = refs[2 * pp:]
    if fox:
        lf_refs = rest[:pp]
        rest = rest[pp:]
    o_ref, m_ref, l_ref, acc_ref, carry_ref = rest
    j = pl.program_id(1)
    qb = _head_rows(q_ref[0]).astype(BF16)
    r = _iota((page, page), 0)
    c = _iota((page, page), 1)
    later = jnp.where(r > c, 1.0, 0.0).astype(BF16)

    @pl.when(j == 0)
    def _():
        if fox:
            kn = jnp.broadcast_to(kn_ref[0], (8, GROUP_WIDTH)).astype(BF16)
            m_ref[...] = jnp.broadcast_to(_dot_nt(qb, kn)[:, 0:1], m_ref.shape)
            l_ref[...] = jnp.ones_like(l_ref)
            acc_ref[...] = jnp.broadcast_to(vn_ref[0], acc_ref.shape).astype(BF16).astype(F32)
            sn = jnp.broadcast_to(sn_ref[0], (8, SMALL_WIDTH))
            lane = _iota((8, SMALL_WIDTH), 1)
            row = _iota((8, SMALL_WIDTH), 0)
            lf_new = jnp.sum(jnp.where(lane == row + LANE_FLOG, sn, 0.0), axis=-1, keepdims=True)
            carry_ref[...] = jnp.broadcast_to(jnp.where(_iota((8, 1), 0) < N_HEADS, lf_new, 0.0), carry_ref.shape)
        else:
            acc_ref[...] = jnp.zeros_like(acc_ref)
            carry_ref[...] = jnp.zeros_like(carry_ref)

    n8 = pp * 8
    tile_rows = lambda a: jnp.concatenate([a] * pp, axis=0)
    z = jnp.concatenate([_dot(qb, k_refs[i][0].astype(BF16)) for i in range(pp)], axis=0)
    if fox:
        pad = jnp.zeros((8 - N_HEADS, page), F32)
        x = jnp.concatenate([a for i in range(pp) for a in (lf_refs[i][0], pad)], axis=0)
    else:
        x = _log_sigmoid(-z)
    ri = _iota((n8, n8), 0)
    ci = _iota((n8, n8), 1)
    before = jnp.where((ri % 8 == ci % 8) & (ci // 8 < ri // 8), 1.0, 0.0).astype(BF16)
    tot = _dot_xc(x, jnp.ones((page, page), BF16))
    upto = _dot_cx(before, tot) + tile_rows(carry_ref[...])
    bias = _dot_xc(x, later) + upto
    carry_ref[...] = (upto + tot)[n8 - 8:, :]
    if fox:
        s = z + bias
        m_prev = m_ref[...]
        m_new = jnp.maximum(m_prev, jnp.max(jnp.max(s.reshape(pp, 8, page), axis=0), axis=-1, keepdims=True))
        alpha = jnp.exp(m_prev - m_new)
        p = jnp.exp(s - tile_rows(m_new))
        l_ref[...] = alpha * l_ref[...] + jnp.sum(jnp.sum(p.reshape(pp, 8, page), axis=0), axis=-1, keepdims=True)
        m_ref[...] = m_new
    else:
        p = jnp.exp(z + x + bias)
    pv = _dot_nt(p[0:8].astype(BF16), v_refs[0][0].astype(BF16))
    for i in range(1, pp):
        pv = pv + _dot_nt(p[8 * i:8 * i + 8].astype(BF16), v_refs[i][0].astype(BF16))
    if fox:
        acc_ref[...] = alpha[:, 0:1] * acc_ref[...] + pv
    else:
        acc_ref[...] = acc_ref[...] + pv

    @pl.when(j == pl.num_programs(1) - 1)
    def _():
        acc = acc_ref[...]
        if fox:
            acc = acc / l_ref[:, 0:1]
        own = _iota((8, GROUP_WIDTH), 1) // HEAD_DIM == _iota((8, GROUP_WIDTH), 0)
        o_ref[0] = jnp.sum(jnp.where(own, acc, 0.0), axis=0, keepdims=True)


def _dec_attn(page_table, q, k_new, v_new, small, cache_k, cache_v, cache_lf_t, *, layer, fox):
    nb = q.shape[0]
    n_pages = page_table.shape[1]
    page = cache_k.shape[3]
    assert page == 128, "per-page statistics are kept one page per vreg row group"
    pp = _pick(n_pages, (PAGES_PER_STEP, 8, 4, 2, 1))
    row3 = lambda a: a.reshape(nb, 1, a.shape[-1])
    rspec = lambda w: pl.BlockSpec((1, 1, w), lambda b, j, pt: (b, 0, 0))

    def page_map(i):
        return lambda b, j, pt: (layer, pt[b, n_pages - 1 - (j * pp + i)], 0, 0)

    in_specs = [rspec(GROUP_WIDTH)] * 3 + [rspec(SMALL_WIDTH)]
    in_specs += [pl.BlockSpec((None, 1, GROUP_WIDTH, page), page_map(i)) for i in range(pp)] * 2
    args = [row3(q), row3(k_new), row3(v_new), row3(small)] + [cache_k] * pp + [cache_v] * pp
    if fox:
        in_specs += [pl.BlockSpec((None, 1, N_HEADS, page), page_map(i)) for i in range(pp)]
        args += [cache_lf_t] * pp
    grid_spec = pltpu.PrefetchScalarGridSpec(
        num_scalar_prefetch=1, grid=(nb, n_pages // pp), in_specs=in_specs,
        out_specs=pl.BlockSpec((1, 1, GROUP_WIDTH), lambda b, j, pt: (b, 0, 0)),
        scratch_shapes=[pltpu.VMEM((8, 128), F32), pltpu.VMEM((8, 128), F32), pltpu.VMEM((8, GROUP_WIDTH), F32),
                        pltpu.VMEM((8, 128), F32)])
    out = pl.pallas_call(
        functools.partial(_dec_attn_kernel, fox=fox, pp=pp, page=page), grid_spec=grid_spec,
        out_shape=jax.ShapeDtypeStruct((nb, 1, GROUP_WIDTH), F32),
        compiler_params=pltpu.CompilerParams(dimension_semantics=("arbitrary", "arbitrary"),
                                             vmem_limit_bytes=VMEM_LIMIT),
        name="fox_step" if fox else "sb_step",
    )(page_table, *args)
    return out.reshape(nb, GROUP_WIDTH)


def _column(row, eye):
    return jnp.sum(eye * row, axis=1, keepdims=True)


def _rec_step_kernel(hq_ref, hlf_ref, hk_ref, hi_ref, dx_ref, sm_ref, w_ref, shg_ref, sdn_ref, buf_ref,
                     ohg_ref, odn_ref, shg_o_ref, sdn_o_ref, buf_o_ref):
    eye = jnp.where(_iota((HEAD_DIM, HEAD_DIM), 0) == _iota((HEAD_DIM, HEAD_DIM), 1), 1.0, 0.0)
    hq = hq_ref[0]
    hlf = hlf_ref[0]
    hk = hk_ref[0]
    hv = hi_ref[0]
    sm = sm_ref[0]
    buf = buf_ref[0]
    x_new = dx_ref[0]
    y = x_new * w_ref[CONV_WIDTH - 1:CONV_WIDTH, :]
    for jw in range(CONV_WIDTH - 1):
        y = y + buf[jw:jw + 1, :] * w_ref[jw:jw + 1, :]
    y = _silu(y)
    buf_o_ref[0] = jnp.concatenate([buf[1:CONV_WIDTH - 1, :], x_new], axis=0)
    for h in range(N_HEADS):
        cs = slice(h * HEAD_DIM, (h + 1) * HEAD_DIM)
        s = shg_ref[0, h]
        s = _column(jnp.exp(hlf[:, cs]), eye) * s + _column(hk[:, cs], eye) * hv[:, cs]
        shg_o_ref[0, h] = s
        ohg_ref[0, :, cs] = jnp.sum(_column(hq[:, cs], eye) * s, axis=0, keepdims=True)
        q = y[:, cs]
        k = y[:, GROUP_WIDTH + h * HEAD_DIM:GROUP_WIDTH + (h + 1) * HEAD_DIM]
        v = y[:, 2 * GROUP_WIDTH + h * HEAD_DIM:2 * GROUP_WIDTH + (h + 1) * HEAD_DIM]
        q = q * lax.rsqrt(jnp.sum(q * q, axis=-1, keepdims=True) + NORM_EPS) * QK_SCALE
        k = k * lax.rsqrt(jnp.sum(k * k, axis=-1, keepdims=True) + NORM_EPS)
        beta = sm[:, LANE_BETA + h:LANE_BETA + h + 1]
        a = jnp.exp(sm[:, LANE_GDEC + h:LANE_GDEC + h + 1])
        s = sdn_ref[0, h]
        kc = _column(k, eye)
        v_new = beta * (v - a * jnp.sum(kc * s, axis=0, keepdims=True))
        s = a * s + kc * v_new
        sdn_o_ref[0, h] = s
        odn_ref[0, :, cs] = jnp.sum(_column(q, eye) * s, axis=0, keepdims=True)


def _rec_step(hq, hlf, hk, hi, dqkv, small, conv_w, s_hg, s_dn, buf):
    nb = hq.shape[0]
    row3 = lambda a: a.reshape(nb, 1, a.shape[-1])
    rspec = lambda w: pl.BlockSpec((1, 1, w), lambda b: (b, 0, 0))
    st_spec = pl.BlockSpec((1, N_HEADS, HEAD_DIM, HEAD_DIM), lambda b: (b, 0, 0, 0))
    buf_spec = pl.BlockSpec((1, CONV_WIDTH - 1, 3 * GROUP_WIDTH), lambda b: (b, 0, 0))
    outs = pl.pallas_call(
        _rec_step_kernel, grid=(nb,),
        in_specs=[rspec(GROUP_WIDTH)] * 4 + [rspec(3 * GROUP_WIDTH), rspec(SMALL_WIDTH),
                                             pl.BlockSpec(conv_w.shape, lambda b: (0, 0)), st_spec, st_spec, buf_spec],
        out_specs=[rspec(GROUP_WIDTH), rspec(GROUP_WIDTH), st_spec, st_spec, buf_spec],
        out_shape=[jax.ShapeDtypeStruct((nb, 1, GROUP_WIDTH), F32)] * 2
        + [jax.ShapeDtypeStruct(s_hg.shape, F32), jax.ShapeDtypeStruct(s_dn.shape, F32),
           jax.ShapeDtypeStruct(buf.shape, F32)],
        compiler_params=pltpu.CompilerParams(dimension_semantics=("arbitrary",), vmem_limit_bytes=VMEM_LIMIT),
        name="recurrent_step",
    )(row3(hq), row3(hlf), row3(hk), row3(hi), row3(dqkv), row3(small), conv_w, s_hg, s_dn, buf)
    ohg, odn, s_hg_new, s_dn_new, buf_new = outs
    return ohg.reshape(nb, GROUP_WIDTH), odn.reshape(nb, GROUP_WIDTH), s_hg_new, s_dn_new, buf_new


def _tile_gain(g):
    return jnp.tile(g.astype(F32), N_HEADS)


def _relayout_w_in(w_in_t_l):
    gw = GROUP_WIDTH
    a = 7 * gw
    e = a + N_HEADS + 7 * gw
    pad = jnp.zeros((SMALL_WIDTH - 3 * N_HEADS, w_in_t_l.shape[1]), w_in_t_l.dtype)
    rows = [w_in_t_l[:a], w_in_t_l[a + N_HEADS:e], w_in_t_l[a:a + N_HEADS], w_in_t_l[e:e + 2 * N_HEADS], pad]
    return jnp.concatenate(rows, axis=0).astype(BF16)


def _small_params(f_bias, dt_bias, a_log):
    sp = jnp.zeros((8, SMALL_WIDTH), F32)
    sp = sp.at[0, LANE_FLOG:LANE_FLOG + N_HEADS].set(f_bias.astype(F32))
    sp = sp.at[1, LANE_GDEC:LANE_GDEC + N_HEADS].set(dt_bias.astype(F32))
    sp = sp.at[2, LANE_GDEC:LANE_GDEC + N_HEADS].set(a_log.astype(F32))
    return sp


def _pick(n, candidates):
    for c in candidates:
        if n % c == 0:
            return c
    return n


def kernel(x_prompt, x_sample, cache_fox_k, cache_fox_v, cache_fox_logf, cache_sb_k, cache_sb_v, state_hgrn, state_dn, state_dn_conv, page_table, hgrn_lb_param, w_in, w_out, ln1_g, ln2_g, fox_f_bias, fox_q_norm, fox_k_norm, sb_q_norm, sb_k_norm, hgrn_out_norm, fox_out_norm, sb_out_norm, dn_out_norm, dn_conv_w, dn_dt_bias, dn_a_log, w_up, w_down):
    depth = w_in.shape[0]
    bsz, seq, d = x_prompt.shape
    nb = x_sample.shape[0]
    n_phys, page = cache_fox_k.shape[1], cache_fox_k.shape[2]
    m = bsz * seq
    tm = _pick(seq, (256, 128, 64, 32, 16, 8))
    tm_mlp = _pick(m, (512, 256, 128, 64, 32, 16, 8))
    tf = _pick(w_up.shape[2], (1024, 512, 256, 128))
    tq = _pick(seq, (512, 256, 128))
    tt = _pick(seq, (256, 128, 64))

    yp = x_prompt.reshape(m, d)
    ys = x_sample.reshape(nb, d)
    lbp = hgrn_lb_param.astype(F32)
    w_in_t = jnp.transpose(w_in, (2, 0, 1))
    kv_t = lambda a: a.transpose(0, 1, 3, 4, 2).reshape(depth, n_phys, GROUP_WIDTH, page)
    lf_t = jnp.swapaxes(cache_fox_logf.astype(F32), 2, 3)
    p_out = [[] for _ in range(8)]
    s_out = [[] for _ in range(8)]
    for l in range(depth):
        w_re = _relayout_w_in(w_in_t[:, l, :])
        w_o = w_out[l].astype(BF16)
        w_u = w_up[l].astype(BF16)
        w_d = w_down[l].astype(BF16)
        g1 = ln1_g[l].reshape(1, d).astype(F32)
        g2 = ln2_g[l].reshape(1, d).astype(F32)
        qk_gains = jnp.stack([_tile_gain(fox_q_norm[l]), _tile_gain(fox_k_norm[l]),
                              _tile_gain(sb_q_norm[l]), _tile_gain(sb_k_norm[l])])
        out_gains = jnp.stack([_tile_gain(hgrn_out_norm[l]), _tile_gain(fox_out_norm[l]),
                               _tile_gain(sb_out_norm[l]), _tile_gain(dn_out_norm[l])])
        sp = _small_params(fox_f_bias[l], dn_dt_bias[l], dn_a_log[l])
        conv_w = dn_conv_w[l].astype(F32)

        (hq, hlf, hk, hi, hgate, fq, fk, fv, sq, sk, sv, dqkv, dz, small, cum, small_t, cum_t) = _inproj(
            yp, g1, w_re, lbp, qk_gains, sp, layer=l, tm=tm, rows_per_seq=seq, with_time=True)
        b3 = lambda a: a.reshape(bsz, seq, a.shape[-1])
        o_hg, st_hg = _hgrn_prompt(b3(hq), b3(hlf), b3(hk), b3(hi), tt=tt)
        o_fx = _fox_prompt(b3(fq), fk, fv, b3(cum), cum_t, tq=tq)
        o_sb = _sb_prompt(b3(sq), sk, sv, tq=tq)
        o_dn, st_dn = _gdn_prompt(b3(dqkv), conv_w, b3(small), small_t, tt=tt)
        yp = _outproj(yp, o_hg.reshape(m, -1), hgate, o_fx.reshape(m, -1), o_sb.reshape(m, -1),
                      o_dn.reshape(m, -1), dz, out_gains, w_o, tm=tm)
        yp = _mlp(yp, g2, w_u, w_d, tm=tm_mlp, tf=tf)
        by_head = lambda a: a.reshape(bsz, N_HEADS, HEAD_DIM, seq).transpose(0, 3, 1, 2)
        p_out[0].append(by_head(fk))
        p_out[1].append(by_head(fv))
        p_out[2].append(b3(small)[:, :, LANE_FLOG:LANE_FLOG + N_HEADS])
        p_out[3].append(by_head(sk))
        p_out[4].append(by_head(sv))
        p_out[5].append(jnp.swapaxes(st_hg, -1, -2))
        p_out[6].append(st_dn)
        p_out[7].append(b3(dqkv)[:, seq - (CONV_WIDTH - 1):, :])

        (hq, hlf, hk, hi, hgate, fq, fk, fv, sq, sk, sv, dqkv, dz, small) = _inproj(
            ys, g1, w_re, lbp, qk_gains, sp, layer=l, tm=nb, rows_per_seq=nb, with_time=False)
        o_fx = _dec_attn(page_table, fq, fk, fv, small, kv_t(cache_fox_k), kv_t(cache_fox_v), lf_t,
                         layer=l, fox=True)
        o_sb = _dec_attn(page_table, sq, sk, sv, small, kv_t(cache_sb_k), kv_t(cache_sb_v), None,
                         layer=l, fox=False)
        o_hg, o_dn, s_hg_new, s_dn_new, buf_new = _rec_step(
            hq, hlf, hk, hi, dqkv, small, conv_w, state_hgrn[l].astype(F32), state_dn[l].astype(F32),
            state_dn_conv[l].astype(F32))
        ys = _outproj(ys, o_hg, hgate, o_fx, o_sb, o_dn, dz, out_gains, w_o, tm=nb)
        ys = _mlp(ys, g2, w_u, w_d, tm=nb, tf=tf)
        sshape = (nb, 1, N_HEADS, HEAD_DIM)
        s_out[0].append(fk.reshape(sshape))
        s_out[1].append(fv.reshape(sshape))
        s_out[2].append(small[:, LANE_FLOG:LANE_FLOG + N_HEADS].reshape(nb, 1, N_HEADS))
        s_out[3].append(sk.reshape(sshape))
        s_out[4].append(sv.reshape(sshape))
        s_out[5].append(s_hg_new)
        s_out[6].append(s_dn_new)
        s_out[7].append(buf_new)

    p = [jnp.stack(v) for v in p_out]
    s = [jnp.stack(v) for v in s_out]
    return (yp.reshape(bsz, seq, d), ys.reshape(nb, 1, d), *p, *s)
```

```python
import functools

import jax
import jax.numpy as jnp
from jax import lax
from jax.experimental import pallas as pl
from jax.experimental.pallas import tpu as pltpu

F32 = jnp.float32
BF16 = jnp.bfloat16

HEAD_DIM = 64
N_HEADS = 4
GROUP_WIDTH = N_HEADS * HEAD_DIM
N_SEGMENTS = 14
SMALL_WIDTH = 128
CONV_WIDTH = 4
NORM_EPS = 1e-6
NEG_BIG = -1e30
QK_SCALE = HEAD_DIM ** -0.5
LOG2E = 1.4426950408889634
INV_LN2 = LOG2E
PRUNE_LOG2 = 160.0
BOUND_SLACK = 1.001
VMEM_LIMIT = 56 * 1024 * 1024

LANE_FLOG = 0
LANE_BETA = 4
LANE_GDEC = 8
SMALL_ROWS_T = 16


def _iota(shape, dim):
    return lax.broadcasted_iota(jnp.int32, shape, dim)


def _dot(a, b):
    return jnp.dot(a, b, preferred_element_type=F32)


def _dot_nt(a, b):
    return lax.dot_general(a, b, (((1,), (1,)), ((), ())), preferred_element_type=F32)


def _split3(x):
    hi = x.astype(BF16)
    r = x - hi.astype(F32)
    mid = r.astype(BF16)
    lo = (r - mid.astype(F32)).astype(BF16)
    return hi, mid, lo


def _dot_xc(x, c, parts=3):
    ps = _split3(x)[:parts]
    out = _dot(ps[0], c)
    for p in ps[1:]:
        out = out + _dot(p, c)
    return out


def _dot_cx(c, x, parts=3):
    ps = _split3(x)[:parts]
    out = _dot(c, ps[0])
    for p in ps[1:]:
        out = out + _dot(c, p)
    return out


def _dot_f32(a, b):
    ah = a.astype(BF16)
    al = (a - ah.astype(F32)).astype(BF16)
    bh = b.astype(BF16)
    bl = (b - bh.astype(F32)).astype(BF16)
    return _dot(ah, bh) + _dot(ah, bl) + _dot(al, bh)


def _head_ones(n=GROUP_WIDTH):
    return (_iota((n, n), 0) // HEAD_DIM == _iota((n, n), 1) // HEAD_DIM).astype(BF16)


def _head_sum(x, ones):
    return _dot_xc(x, ones)


def _head_rms(x, ones, gain):
    ms = _head_sum(x * x, ones) * (1.0 / HEAD_DIM)
    return x * lax.rsqrt(ms + NORM_EPS) * gain


def _log_sigmoid(x):
    return jnp.minimum(x, 0.0) - jnp.log1p(jnp.exp(-jnp.abs(x)))


def _softplus(x):
    return jnp.maximum(x, 0.0) + jnp.log1p(jnp.exp(-jnp.abs(x)))


def _sigmoid(x):
    return 1.0 / (1.0 + jnp.exp(-x))


def _silu(x):
    return x * _sigmoid(x)


def _max_all(x):
    return jnp.max(jnp.max(x, axis=0, keepdims=True), axis=1, keepdims=True)


def _lane_pack(vals):
    lane = _iota((1, 128), 1)
    out = jnp.zeros((1, 128), F32)
    for i, v in enumerate(vals):
        out = jnp.where(lane == i, v, out)
    return out


def _any_head(mask):
    lane = _iota(mask.shape, 1)
    return jnp.max(jnp.where(mask & (lane < N_HEADS), 1.0, 0.0)) > 0.0


def _inproj_kernel(x_ref, g1_ref, w_ref, lbp_ref, gains_ref, sp_ref, *refs, layer, depth, tiles_per_seq,
                   with_time):
    (hq_ref, hlf_ref, hk_ref, hi_ref, hg_ref, fq_ref, fk_ref, fv_ref, sq_ref, sk_ref, sv_ref,
     dqkv_ref, dz_ref, small_ref) = refs[:14]
    x = x_ref[...]
    h = (x * lax.rsqrt(jnp.mean(x * x, axis=-1, keepdims=True) + NORM_EPS) * g1_ref[...]).astype(BF16)

    def seg(j, width=GROUP_WIDTH):
        return _dot_nt(h, w_ref[j * GROUP_WIDTH:j * GROUP_WIDTH + width, :])

    ones = _head_ones()

    rows = [lbp_ref[i:i + 1, :] for i in range(depth)]
    mx = functools.reduce(jnp.maximum, rows)
    es = [jnp.exp(r - mx) for r in rows]
    lb = sum(es[1:layer + 1], jnp.zeros_like(mx)) / sum(es)
    hq_ref[...] = seg(0)
    hf = seg(1)
    a = jnp.log(lb)
    b = jnp.log1p(-lb) + _log_sigmoid(hf)
    hi = jnp.maximum(a, b)
    lo = jnp.minimum(a, b)
    hlf_ref[...] = hi + jnp.log1p(jnp.exp(lo - hi))
    hk_ref[...] = (1.0 - lb) * _sigmoid(-hf)
    hi_ref[...] = seg(2)
    hg_ref[...] = _silu(seg(3))

    fq_ref[...] = _head_rms(seg(4), ones, gains_ref[0:1, :]) * QK_SCALE
    sq_ref[...] = _head_rms(seg(7), ones, gains_ref[2:3, :]) * QK_SCALE
    kv = (_head_rms(seg(5), ones, gains_ref[1:2, :]), seg(6), _head_rms(seg(8), ones, gains_ref[3:4, :]), seg(9))
    for ref, val in zip((fk_ref, fv_ref, sk_ref, sv_ref), kv):
        if with_time:
            ref[0] = val.T
        else:
            ref[...] = val

    dqkv_ref[...] = seg(10, 3 * GROUP_WIDTH)
    dz_ref[...] = _silu(seg(13))

    s = _dot_nt(h, w_ref[N_SEGMENTS * GROUP_WIDTH:, :])
    lane = _iota(s.shape, 1)
    f_log = _log_sigmoid(s + sp_ref[0:1, :])
    beta = _sigmoid(s)
    g_dec = -jnp.exp(sp_ref[2:3, :]) * _softplus(s + sp_ref[1:2, :])
    small = jnp.where(lane < LANE_BETA, f_log, jnp.where(lane < LANE_GDEC, beta, g_dec))
    small_ref[...] = small

    if with_time:
        cum_ref, small_t_ref, cum_t_ref, carry_ref = refs[14:]
        tm = s.shape[0]

        @pl.when(pl.program_id(0) % tiles_per_seq == 0)
        def _():
            carry_ref[...] = jnp.zeros_like(carry_ref)

        tril = (_iota((tm, tm), 1) <= _iota((tm, tm), 0)).astype(BF16)
        cum = _dot_cx(tril, small) + carry_ref[0:1, :]
        cum_ref[...] = cum
        carry_ref[...] = jnp.broadcast_to(cum[tm - 1:tm, :], carry_ref.shape)
        small_t_ref[0] = small.T[:SMALL_ROWS_T, :]
        cum_t_ref[0] = cum.T[:SMALL_ROWS_T, :]


def _inproj(x2, g1, w_re, lbp, gains, sp, *, layer, tm, rows_per_seq, with_time):
    m, d = x2.shape
    depth = lbp.shape[0]
    grid = (m // tm,)
    row = lambda i: (i, 0)
    const = lambda i: (0, 0)
    tps = rows_per_seq // tm
    nseq = m // rows_per_seq
    seg_shape = jax.ShapeDtypeStruct((m, GROUP_WIDTH), F32)
    seg_spec = pl.BlockSpec((tm, GROUP_WIDTH), row)
    out_shape = [seg_shape] * 11 + [jax.ShapeDtypeStruct((m, 3 * GROUP_WIDTH), F32), seg_shape,
                                    jax.ShapeDtypeStruct((m, SMALL_WIDTH), F32)]
    out_specs = [seg_spec] * 11 + [pl.BlockSpec((tm, 3 * GROUP_WIDTH), row), seg_spec,
                                   pl.BlockSpec((tm, SMALL_WIDTH), row)]
    scratch = []
    if with_time:
        def by_time(rows):
            return (jax.ShapeDtypeStruct((nseq, rows, rows_per_seq), F32),
                    pl.BlockSpec((1, rows, tm), lambda i: (i // tps, 0, i % tps)))
        for idx in (6, 7, 9, 10):
            out_shape[idx], out_specs[idx] = by_time(GROUP_WIDTH)
        out_shape += [jax.ShapeDtypeStruct((m, SMALL_WIDTH), F32)]
        out_specs += [pl.BlockSpec((tm, SMALL_WIDTH), row)]
        for _ in range(2):
            sh, sp_ = by_time(SMALL_ROWS_T)
            out_shape.append(sh)
            out_specs.append(sp_)
        scratch = [pltpu.VMEM((8, SMALL_WIDTH), F32)]
    kern = functools.partial(_inproj_kernel, layer=layer, depth=depth, tiles_per_seq=tps,
                             with_time=with_time)
    return pl.pallas_call(
        kern, grid=grid,
        in_specs=[pl.BlockSpec((tm, d), row), pl.BlockSpec((1, d), const), pl.BlockSpec(w_re.shape, const),
                  pl.BlockSpec(lbp.shape, const), pl.BlockSpec(gains.shape, const), pl.BlockSpec(sp.shape, const)],
        out_specs=out_specs, out_shape=out_shape, scratch_shapes=scratch,
        compiler_params=pltpu.CompilerParams(dimension_semantics=("arbitrary",), vmem_limit_bytes=VMEM_LIMIT),
        name="inproj_time" if with_time else "inproj_step",
    )(x2, g1, w_re, lbp, gains, sp)


HGRN_SUB = 16


def _hgrn_kernel(q_ref, lf_ref, k_ref, v_ref, o_ref, st_ref, s_ref, oi_ref, *, tt):
    t = pl.program_id(1)

    @pl.when(t == 0)
    def _():
        s_ref[...] = jnp.zeros_like(s_ref)

    q = q_ref[0]
    lf = lf_ref[0]
    kin = k_ref[0]
    v = v_ref[0]
    r = _iota((tt, tt), 0)
    c = _iota((tt, tt), 1)
    same = (r // HGRN_SUB) == (c // HGRN_SUB)
    g = _dot_cx(jnp.where(same & (c <= r), 1.0, 0.0).astype(BF16), lf)
    gl = _dot_cx(jnp.where(same, 1.0, 0.0).astype(BF16), lf)
    qg = q * jnp.exp(g)
    kg = kin * jnp.exp(gl - g)

    ones = _head_ones()
    rowmod = _iota((tt, GROUP_WIDTH), 0) % HGRN_SUB
    o = jnp.zeros((tt, GROUP_WIDTH), F32)
    for d in range(HGRN_SUB):
        if d == 0:
            kd, gd, vd = kin, g, v
        else:
            kd = pltpu.roll(kin, d, 0)
            gd = pltpu.roll(g, d, 0)
            vd = pltpu.roll(v, d, 0)
        e = jnp.where(rowmod >= d, g - gd, NEG_BIG)
        p = q * kd * jnp.exp(e)
        o = o + _dot_xc(p, ones, parts=2) * vd

    v_t = v.T
    for i in range(tt // HGRN_SUB):
        rs = slice(i * HGRN_SUB, (i + 1) * HGRN_SUB)
        for h in range(N_HEADS):
            cs = slice(h * HEAD_DIM, (h + 1) * HEAD_DIM)
            s = s_ref[h]
            oi_ref[rs, cs] = _dot_nt(qg[rs, cs].astype(BF16), s.astype(BF16))
            dec = jnp.exp(gl[i * HGRN_SUB:i * HGRN_SUB + 1, cs])
            s_ref[h] = dec * s + _dot(v_t[cs, rs].astype(BF16), kg[rs, cs].astype(BF16))
    o_ref[0] = o + oi_ref[...]

    @pl.when(t == pl.num_programs(1) - 1)
    def _():
        st_ref[0] = s_ref[...]


def _hgrn_prompt(hq, hlf, hk, hi, *, tt):
    b, t, _ = hq.shape
    blk = pl.BlockSpec((1, tt, GROUP_WIDTH), lambda i, j: (i, j, 0))
    return pl.pallas_call(
        functools.partial(_hgrn_kernel, tt=tt), grid=(b, t // tt),
        in_specs=[blk] * 4,
        out_specs=[blk, pl.BlockSpec((1, N_HEADS, HEAD_DIM, HEAD_DIM), lambda i, j: (i, 0, 0, 0))],
        out_shape=[jax.ShapeDtypeStruct((b, t, GROUP_WIDTH), F32),
                   jax.ShapeDtypeStruct((b, N_HEADS, HEAD_DIM, HEAD_DIM), F32)],
        scratch_shapes=[pltpu.VMEM((N_HEADS, HEAD_DIM, HEAD_DIM), F32), pltpu.VMEM((tt, GROUP_WIDTH), F32)],
        compiler_params=pltpu.CompilerParams(dimension_semantics=("arbitrary", "arbitrary"),
                                             vmem_limit_bytes=VMEM_LIMIT),
        name="hgrn_prompt",
    )(hq, hlf, hk, hi)


def _fox_kernel(q_ref, k_ref, v_ref, fq_ref, fk_ref, o_ref, qa_ref, m_ref, acc_ref, qn_ref, fq_max_ref, m_min_ref,
                kn_ref, fk_min_ref, *, tq):
    qi = pl.program_id(1)
    j = pl.program_id(2)

    @pl.when(j == 0)
    def _():
        m_ref[...] = jnp.full_like(m_ref, NEG_BIG)
        acc_ref[...] = jnp.zeros_like(acc_ref)
        q = q_ref[0] * LOG2E
        f = fq_ref[0] * LOG2E
        lane = _iota((tq, HEAD_DIM), 1)
        for h in range(N_HEADS):
            hi, mid, lo = [p.astype(F32) for p in _split3(f[:, LANE_FLOG + h:LANE_FLOG + h + 1])]
            ext = jnp.where(lane == 0, hi, jnp.where(lane == 1, mid, jnp.where(lane == 2, lo,
                                                                               jnp.where(lane < 6, 1.0, 0.0))))
            qa_ref[h] = jnp.concatenate([q[:, h * HEAD_DIM:(h + 1) * HEAD_DIM], ext], axis=1).astype(BF16)

        heads = range(N_HEADS)
        rows = lambda h: slice(h * HEAD_DIM, (h + 1) * HEAD_DIM)
        qb = q.astype(BF16).astype(F32)
        qn2 = _head_sum(qb * qb, _head_ones())
        kf = k_ref[0].astype(BF16).astype(F32)
        k2 = kf * kf
        fk = fk_ref[0] * LOG2E
        qn_ref[...] = _lane_pack([_max_all(qn2[:, rows(h)]) for h in heads])
        fq_max_ref[...] = _lane_pack([_max_all(f[:, LANE_FLOG + h:LANE_FLOG + h + 1]) for h in heads])
        kn_ref[pl.ds(qi, 1), :] = _lane_pack([_max_all(jnp.sum(k2[rows(h), :], axis=0, keepdims=True))
                                              for h in heads])
        fk_min_ref[pl.ds(qi, 1), :] = _lane_pack([-_max_all(-fk[LANE_FLOG + h:LANE_FLOG + h + 1, :])
                                                  for h in heads])

    def block_matters():
        zcap2 = qn_ref[...] * kn_ref[pl.ds(qi - j, 1), :] * (BOUND_SLACK * BOUND_SLACK)
        bias_cap = fq_max_ref[...] - fk_min_ref[pl.ds(qi - j, 1), :]
        room = m_min_ref[...] - bias_cap - PRUNE_LOG2
        return _any_head((room < 0.0) | (zcap2 > room * room))

    def step(masked):
        kf = k_ref[0]
        vf = v_ref[0]
        fk = fk_ref[0] * LOG2E
        row = _iota((8, tq), 0)
        pad = jnp.zeros((HEAD_DIM - 8, tq), F32)
        v_ext = jnp.concatenate([jnp.where(row == 0, 1.0, 0.0), pad], axis=0)
        if masked:
            keep = _iota((tq, tq), 1) <= _iota((tq, tq), 0)
        heads = range(N_HEADS)
        ss = []
        for h in heads:
            cs = slice(h * HEAD_DIM, (h + 1) * HEAD_DIM)
            hi, mid, lo = [p.astype(F32) for p in _split3(fk[LANE_FLOG + h:LANE_FLOG + h + 1, :])]
            k_ext = jnp.where(row < 3, 1.0, jnp.where(row == 3, -hi, jnp.where(row == 4, -mid,
                                                                               jnp.where(row == 5, -lo, 0.0))))
            ka = jnp.concatenate([kf[cs, :], k_ext, pad], axis=0).astype(BF16)
            ss.append(_dot(qa_ref[h], ka))
        if masked:
            ss = [jnp.where(keep, s, NEG_BIG) for s in ss]
        m_prevs = [m_ref[h] for h in heads]
        m_news = [jnp.maximum(m_prevs[h], jnp.max(ss[h], axis=-1, keepdims=True)) for h in heads]
        ps = [jnp.exp2(ss[h] - jnp.concatenate([m_news[h]] * (tq // 128), axis=1)) for h in heads]
        for h in heads:
            va = jnp.concatenate([vf[h * HEAD_DIM:(h + 1) * HEAD_DIM, :], v_ext], axis=0).astype(BF16)
            m_ref[h] = m_news[h]
            acc_ref[h] = jnp.exp2(m_prevs[h] - m_news[h]) * acc_ref[h] + _dot_nt(ps[h].astype(BF16), va)
        m_min_ref[...] = _lane_pack([-_max_all(-m_news[h]) for h in heads])

    @pl.when(j == 0)
    def _():
        step(True)

    @pl.when((j > 0) & (j <= qi))
    def _():
        @pl.when(block_matters())
        def _():
            step(False)

    @pl.when(j == qi)
    def _():
        for h in range(N_HEADS):
            acc = acc_ref[h]
            o_ref[0, :, h * HEAD_DIM:(h + 1) * HEAD_DIM] = acc[:, :HEAD_DIM] / acc[:, HEAD_DIM:HEAD_DIM + 1]


def _fox_prompt(fq, fk, fv, cum, cum_t, *, tq):
    b, t, _ = fq.shape
    n = t // tq
    qspec = pl.BlockSpec((1, tq, GROUP_WIDTH), lambda i, qi, j: (i, qi, 0))
    kspec = pl.BlockSpec((1, GROUP_WIDTH, tq), lambda i, qi, j: (i, 0, jnp.maximum(qi - j, 0)))
    return pl.pallas_call(
        functools.partial(_fox_kernel, tq=tq), grid=(b, n, n),
        in_specs=[qspec, kspec, kspec,
                  pl.BlockSpec((1, tq, SMALL_WIDTH), lambda i, qi, j: (i, qi, 0)),
                  pl.BlockSpec((1, SMALL_ROWS_T, tq), lambda i, qi, j: (i, 0, jnp.maximum(qi - j, 0)))],
        out_specs=qspec,
        out_shape=jax.ShapeDtypeStruct((b, t, GROUP_WIDTH), F32),
        scratch_shapes=[pltpu.VMEM((N_HEADS, tq, 2 * HEAD_DIM), BF16),
                        pltpu.VMEM((N_HEADS, tq, 128), F32), pltpu.VMEM((N_HEADS, tq, 2 * HEAD_DIM), F32),
                        pltpu.VMEM((1, 128), F32), pltpu.VMEM((1, 128), F32), pltpu.VMEM((1, 128), F32),
                        pltpu.VMEM((n, 128), F32), pltpu.VMEM((n, 128), F32)],
        compiler_params=pltpu.CompilerParams(dimension_semantics=("arbitrary", "arbitrary", "arbitrary"),
                                             vmem_limit_bytes=VMEM_LIMIT),
        name="fox_prompt",
    )(fq, fk, fv, cum, cum_t)


SB_SUB = 256


def _sb_kernel(q_ref, k_ref, v_ref, o_ref, qb_ref, carry_ref, acc_ref, qn_ref, kn_ref, cmin_ref, *, tq):
    qi = pl.program_id(1)
    j = pl.program_id(2)

    heads = range(N_HEADS)
    rows = lambda h: slice(h * HEAD_DIM, (h + 1) * HEAD_DIM)

    @pl.when(j == 0)
    def _():
        carry_ref[...] = jnp.zeros_like(carry_ref)
        acc_ref[...] = jnp.zeros_like(acc_ref)
        qb = (q_ref[0] * LOG2E).astype(BF16)
        qb_ref[...] = qb
        qn2 = _head_sum(qb.astype(F32) * qb.astype(F32), _head_ones())
        kf = k_ref[0].astype(BF16).astype(F32)
        k2 = kf * kf
        qn_ref[...] = _lane_pack([_max_all(qn2[:, rows(h)]) for h in heads])
        kn_ref[pl.ds(qi, 1), :] = _lane_pack([_max_all(jnp.sum(k2[rows(h), :], axis=0, keepdims=True))
                                              for h in heads])
        cmin_ref[...] = jnp.zeros_like(cmin_ref)

    def block_matters():
        zcap2 = qn_ref[...] * kn_ref[pl.ds(qi - j, 1), :] * (BOUND_SLACK * BOUND_SLACK)
        room = cmin_ref[...] - PRUNE_LOG2
        return _any_head((room < 0.0) | (zcap2 > room * room))

    def step(masked):
        k = k_ref[0].astype(BF16)
        v = v_ref[0].astype(BF16)
        r = _iota((tq, tq), 0)
        c = _iota((tq, tq), 1)
        sub = min(SB_SUB, tq)
        suffix = jnp.where(_iota((sub, sub), 1) <= _iota((sub, sub), 0), 1.0, 0.0).astype(BF16)
        z2s = [_dot(qb_ref[:, rows(h)], k[rows(h), :]) for h in heads]
        sps = [jnp.maximum(z2, 0.0) + jnp.log(1.0 + jnp.exp2(-jnp.abs(z2))) * INV_LN2 for z2 in z2s]
        if masked:
            sps = [jnp.where(c < r, sp, 0.0) for sp in sps]
        carries = [carry_ref[h] for h in heads]
        cum_parts = [[None] * (tq // sub) for _ in heads]
        for part in reversed(range(tq // sub)):
            ks = slice(part * sub, (part + 1) * sub)
            for h in heads:
                wide = jnp.concatenate([carries[h]] * (sub // 128), axis=1)
                cum = _dot_xc(sps[h][:, ks], suffix, parts=2) + wide
                cum_parts[h][part] = cum
                carries[h] = jnp.broadcast_to(cum[:, 0:1], carries[h].shape)
        es = [z2s[h] - jnp.concatenate(cum_parts[h], axis=1) for h in heads]
        if masked:
            es = [jnp.where(c < r, e, NEG_BIG) for e in es]
        pad = jnp.zeros((HEAD_DIM, tq), BF16)
        for h in heads:
            va = jnp.concatenate([v[rows(h), :], pad], axis=0)
            acc_ref[h] = acc_ref[h] + _dot_nt(jnp.exp2(es[h]).astype(BF16), va)
            carry_ref[h] = carries[h]
        cmin_ref[...] = _lane_pack([-_max_all(-carries[h]) for h in heads])

    @pl.when(j == 0)
    def _():
        step(True)

    @pl.when((j > 0) & (j <= qi))
    def _():
        @pl.when(block_matters())
        def _():
            step(False)

    @pl.when(j == qi)
    def _():
        for h in range(N_HEADS):
            o_ref[0, :, h * HEAD_DIM:(h + 1) * HEAD_DIM] = acc_ref[h][:, :HEAD_DIM]


def _sb_prompt(sq, sk, sv, *, tq):
    b, t, _ = sq.shape
    n = t // tq
    qspec = pl.BlockSpec((1, tq, GROUP_WIDTH), lambda i, qi, j: (i, qi, 0))
    kspec = pl.BlockSpec((1, GROUP_WIDTH, tq), lambda i, qi, j: (i, 0, jnp.maximum(qi - j, 0)))
    return pl.pallas_call(
        functools.partial(_sb_kernel, tq=tq), grid=(b, n, n),
        in_specs=[qspec, kspec, kspec],
        out_specs=qspec,
        out_shape=jax.ShapeDtypeStruct((b, t, GROUP_WIDTH), F32),
        scratch_shapes=[pltpu.VMEM((tq, GROUP_WIDTH), BF16),
                        pltpu.VMEM((N_HEADS, tq, 128), F32), pltpu.VMEM((N_HEADS, tq, 2 * HEAD_DIM), F32),
                        pltpu.VMEM((1, 128), F32), pltpu.VMEM((n, 128), F32), pltpu.VMEM((1, 128), F32)],
        compiler_params=pltpu.CompilerParams(dimension_semantics=("arbitrary", "arbitrary", "arbitrary"),
                                             vmem_limit_bytes=VMEM_LIMIT),
        name="sb_prompt",
    )(sq, sk, sv)


GDN_CHUNK = 128
GDN_BASE = 16


def _unit_lower_inverses(lmats, ii, jj):
    ns = [jnp.where(ii // GDN_BASE == jj // GDN_BASE, -lm, 0.0) for lm in lmats]
    eye = jnp.where(ii == jj, 1.0, 0.0)
    ts = [eye + n for n in ns]
    ps = ns
    for _ in range(GDN_BASE.bit_length() - 2):
        ps = [_dot_f32(p, p) for p in ps]
        ts = [t + _dot_f32(t, p) for t, p in zip(ts, ps)]
    b = GDN_BASE
    while b < GDN_CHUNK:
        lower_left = (ii // (2 * b) == jj // (2 * b)) & (ii // b != jj // b)
        mids = [_dot_f32(t, jnp.where(lower_left, lm, 0.0)) for t, lm in zip(ts, lmats)]
        ts = [t - _dot_f32(mid, t) for t, mid in zip(ts, mids)]
        b *= 2
    return ts


def _gdn_prep_kernel(x_ref, w_ref, sm_ref, smt_ref, u_ref, wk_ref, qe_ref, attn_ref, kdt_ref, gct_ref, ext_ref, *, tt):
    t = pl.program_id(1)

    @pl.when(t == 0)
    def _():
        ext_ref[0:8, :] = jnp.zeros((8, 3 * GROUP_WIDTH), F32)

    @pl.when(t > 0)
    def _():
        ext_ref[0:8, :] = ext_ref[tt:tt + 8, :]

    ext_ref[8:8 + tt, :] = x_ref[0]
    y = ext_ref[pl.ds(8 - (CONV_WIDTH - 1), tt), :] * w_ref[0:1, :]
    for jw in range(1, CONV_WIDTH):
        y = y + ext_ref[pl.ds(8 - (CONV_WIDTH - 1) + jw, tt), :] * w_ref[jw:jw + 1, :]
    y = _silu(y)
    ones = _head_ones()
    q = y[:, :GROUP_WIDTH]
    k = y[:, GROUP_WIDTH:2 * GROUP_WIDTH]
    v = y[:, 2 * GROUP_WIDTH:]
    q = q * lax.rsqrt(_head_sum(q * q, ones) + NORM_EPS) * QK_SCALE
    k = k * lax.rsqrt(_head_sum(k * k, ones) + NORM_EPS)
    sm = sm_ref[0]
    smt = smt_ref[0]
    r = _iota((tt, tt), 0)
    c = _iota((tt, tt), 1)
    same = (r // GDN_CHUNK) == (c // GDN_CHUNK)
    gc_col = _dot_cx(jnp.where(same & (c <= r), 1.0, 0.0).astype(BF16), sm)
    gc_row = _dot_xc(smt, jnp.where(same & (r <= c), 1.0, 0.0).astype(BF16))
    gct_ref[0] = gc_row
    k_t = k.T
    ii = _iota((GDN_CHUNK, GDN_CHUNK), 0)
    jj = _iota((GDN_CHUNK, GDN_CHUNK), 1)
    n_chunks = tt // GDN_CHUNK
    lmats, rhss = [], []
    for ci in range(n_chunks):
        rs = slice(ci * GDN_CHUNK, (ci + 1) * GDN_CHUNK)
        qes, attns, kdts = [], [], []
        for h in range(N_HEADS):
            cs = slice(h * HEAD_DIM, (h + 1) * HEAD_DIM)
            gcol = gc_col[rs, LANE_GDEC + h:LANE_GDEC + h + 1]
            grow = gc_row[LANE_GDEC + h:LANE_GDEC + h + 1, rs]
            beta = sm[rs, LANE_BETA + h:LANE_BETA + h + 1]
            dec = jnp.exp(jnp.where(ii >= jj, gcol - grow, NEG_BIG))
            qh = q[rs, cs].astype(BF16)
            kth = k_t[cs, rs]
            kb = k[rs, cs] * beta
            eg = jnp.exp(gcol)
            lmats.append(jnp.where(ii > jj, _dot(kb.astype(BF16), kth.astype(BF16)) * dec, 0.0))
            rhss.append(jnp.concatenate([v[rs, cs] * beta, kb * eg], axis=1))
            qes.append(q[rs, cs] * eg)
            attns.append(_dot(qh, kth.astype(BF16)) * dec)
            kdts.append(kth * jnp.exp(grow[:, GDN_CHUNK - 1:GDN_CHUNK] - grow))
        qe_ref[0, rs, :] = jnp.concatenate(qes, axis=1).astype(BF16)
        attn_ref[0, rs, :] = jnp.concatenate(attns, axis=1).astype(BF16)
        kdt_ref[0, :, rs] = jnp.concatenate(kdts, axis=0).astype(BF16)
    tinvs = _unit_lower_inverses(lmats, ii, jj)
    sols = [_dot_f32(ti, rhs) for ti, rhs in zip(tinvs, rhss)]
    for ci in range(n_chunks):
        rs = slice(ci * GDN_CHUNK, (ci + 1) * GDN_CHUNK)
        chunk = sols[ci * N_HEADS:(ci + 1) * N_HEADS]
        u_ref[0, rs, :] = jnp.concatenate([sol[:, :HEAD_DIM] for sol in chunk], axis=1)
        wk_ref[0, rs, :] = jnp.concatenate([sol[:, HEAD_DIM:] for sol in chunk], axis=1).astype(BF16)


def _gdn_scan_kernel(u_ref, wk_ref, qe_ref, attn_ref, kdt_ref, gct_ref, o_ref, st_ref, s_ref, *, tt, nb):
    t = pl.program_id(0)

    @pl.when(t == 0)
    def _():
        s_ref[...] = jnp.zeros_like(s_ref)

    for ci in range(tt // GDN_CHUNK):
        rs = slice(ci * GDN_CHUNK, (ci + 1) * GDN_CHUNK)
        items = [(b, h) for b in range(nb) for h in range(N_HEADS)]
        cs = lambda h: slice(h * HEAD_DIM, (h + 1) * HEAD_DIM)
        us = [u_ref[b, rs, :] for b in range(nb)]
        wks = [wk_ref[b, rs, :] for b in range(nb)]
        qes = [qe_ref[b, rs, :] for b in range(nb)]
        attns = [attn_ref[b, rs, :] for b in range(nb)]
        kdts = [kdt_ref[b, :, rs] for b in range(nb)]
        decays = [jnp.exp(gct_ref[b, :, rs][:, GDN_CHUNK - 1:GDN_CHUNK]) for b in range(nb)]
        ss = [s_ref[b, h] for b, h in items]
        sbs = [s.astype(BF16) for s in ss]
        v_news = [us[b][:, cs(h)] - _dot(wks[b][:, cs(h)], sb) for (b, h), sb in zip(items, sbs)]
        vbs = [vn.astype(BF16) for vn in v_news]
        for (b, h), s, vb in zip(items, ss, vbs):
            a = decays[b][LANE_GDEC + h:LANE_GDEC + h + 1, :]
            s_ref[b, h] = a * s + _dot(kdts[b][cs(h), :], vb)
        for b in range(nb):
            outs = [_dot(qes[b][:, cs(h)], sbs[b * N_HEADS + h])
                    + _dot(attns[b][:, h * GDN_CHUNK:(h + 1) * GDN_CHUNK], vbs[b * N_HEADS + h])
                    for h in range(N_HEADS)]
            o_ref[b, rs, :] = jnp.concatenate(outs, axis=1)

    @pl.when(t == pl.num_programs(0) - 1)
    def _():
        st_ref[...] = s_ref[...]


def _gdn_prompt(dqkv, conv_w, small, small_t, *, tt):
    b, t, _ = dqkv.shape
    n_attn = N_HEADS * GDN_CHUNK
    by_rows = lambda w: pl.BlockSpec((1, tt, w), lambda i, j: (i, j, 0))
    by_cols = lambda r: pl.BlockSpec((1, r, tt), lambda i, j: (i, 0, j))
    u, wk, qe, attn, kdt, gct = pl.pallas_call(
        functools.partial(_gdn_prep_kernel, tt=tt), grid=(b, t // tt),
        in_specs=[by_rows(3 * GROUP_WIDTH), pl.BlockSpec(conv_w.shape, lambda i, j: (0, 0)),
                  by_rows(SMALL_WIDTH), by_cols(SMALL_ROWS_T)],
        out_specs=[by_rows(GROUP_WIDTH), by_rows(GROUP_WIDTH), by_rows(GROUP_WIDTH), by_rows(n_attn),
                   by_cols(GROUP_WIDTH), by_cols(SMALL_ROWS_T)],
        out_shape=[jax.ShapeDtypeStruct((b, t, GROUP_WIDTH), F32), jax.ShapeDtypeStruct((b, t, GROUP_WIDTH), BF16),
                   jax.ShapeDtypeStruct((b, t, GROUP_WIDTH), BF16), jax.ShapeDtypeStruct((b, t, n_attn), BF16),
                   jax.ShapeDtypeStruct((b, GROUP_WIDTH, t), BF16), jax.ShapeDtypeStruct((b, SMALL_ROWS_T, t), F32)],
        scratch_shapes=[pltpu.VMEM((tt + 8, 3 * GROUP_WIDTH), F32)],
        compiler_params=pltpu.CompilerParams(dimension_semantics=("arbitrary", "arbitrary"),
                                             vmem_limit_bytes=VMEM_LIMIT),
        name="gdn_prep",
    )(dqkv, conv_w, small, small_t)
    all_rows = lambda w: pl.BlockSpec((b, tt, w), lambda j: (0, j, 0))
    all_cols = lambda r: pl.BlockSpec((b, r, tt), lambda j: (0, 0, j))
    st_spec = pl.BlockSpec((b, N_HEADS, HEAD_DIM, HEAD_DIM), lambda j: (0, 0, 0, 0))
    return pl.pallas_call(
        functools.partial(_gdn_scan_kernel, tt=tt, nb=b), grid=(t // tt,),
        in_specs=[all_rows(GROUP_WIDTH), all_rows(GROUP_WIDTH), all_rows(GROUP_WIDTH), all_rows(n_attn),
                  all_cols(GROUP_WIDTH), all_cols(SMALL_ROWS_T)],
        out_specs=[all_rows(GROUP_WIDTH), st_spec],
        out_shape=[jax.ShapeDtypeStruct((b, t, GROUP_WIDTH), F32),
                   jax.ShapeDtypeStruct((b, N_HEADS, HEAD_DIM, HEAD_DIM), F32)],
        scratch_shapes=[pltpu.VMEM((b, N_HEADS, HEAD_DIM, HEAD_DIM), F32)],
        compiler_params=pltpu.CompilerParams(dimension_semantics=("arbitrary",), vmem_limit_bytes=VMEM_LIMIT),
        name="gdn_scan",
    )(u, wk, qe, attn, kdt, gct)


def _outproj_kernel(x_ref, ohg_ref, hg_ref, ofx_ref, osb_ref, odn_ref, dz_ref, gains_ref, w_ref, y_ref):
    ones = _head_ones()
    parts = [
        _head_rms(ohg_ref[...], ones, gains_ref[0:1, :]) * hg_ref[...],
        _head_rms(ofx_ref[...], ones, gains_ref[1:2, :]),
        _head_rms(osb_ref[...], ones, gains_ref[2:3, :]),
        _head_rms(odn_ref[...], ones, gains_ref[3:4, :]) * dz_ref[...],
    ]
    y = x_ref[...]
    for gidx, p in enumerate(parts):
        y = y + _dot(p.astype(BF16), w_ref[gidx * GROUP_WIDTH:(gidx + 1) * GROUP_WIDTH, :])
    y_ref[...] = y


def _outproj(x2, ohg, hgate, ofx, osb, odn, dz, gains, w_out, *, tm):
    m, d = x2.shape
    row = lambda i: (i, 0)
    const = lambda i: (0, 0)
    seg = pl.BlockSpec((tm, GROUP_WIDTH), row)
    return pl.pallas_call(
        _outproj_kernel, grid=(m // tm,),
        in_specs=[pl.BlockSpec((tm, d), row)] + [seg] * 6 + [pl.BlockSpec(gains.shape, const),
                                                            pl.BlockSpec(w_out.shape, const)],
        out_specs=pl.BlockSpec((tm, d), row),
        out_shape=jax.ShapeDtypeStruct((m, d), F32),
        compiler_params=pltpu.CompilerParams(dimension_semantics=("arbitrary",), vmem_limit_bytes=VMEM_LIMIT),
        name="outproj",
    )(x2, ohg, hgate, ofx, osb, odn, dz, gains, w_out)


def _mlp_kernel(x_ref, g2_ref, wu_ref, wd_ref, y_ref, h_ref, acc_ref):
    f = pl.program_id(1)

    @pl.when(f == 0)
    def _():
        x = x_ref[...]
        h_ref[...] = (x * lax.rsqrt(jnp.mean(x * x, axis=-1, keepdims=True) + NORM_EPS) * g2_ref[...]).astype(BF16)
        acc_ref[...] = x

    u = jnp.maximum(_dot(h_ref[...], wu_ref[...]), 0.0)
    acc_ref[...] += _dot((u * u).astype(BF16), wd_ref[...])

    @pl.when(f == pl.num_programs(1) - 1)
    def _():
        y_ref[...] = acc_ref[...]


def _mlp(x2, g2, w_up, w_down, *, tm, tf):
    m, d = x2.shape
    dff = w_up.shape[1]
    return pl.pallas_call(
        _mlp_kernel, grid=(m // tm, dff // tf),
        in_specs=[pl.BlockSpec((tm, d), lambda i, f: (i, 0)), pl.BlockSpec((1, d), lambda i, f: (0, 0)),
                  pl.BlockSpec((d, tf), lambda i, f: (0, f)), pl.BlockSpec((tf, d), lambda i, f: (f, 0))],
        out_specs=pl.BlockSpec((tm, d), lambda i, f: (i, 0)),
        out_shape=jax.ShapeDtypeStruct((m, d), F32),
        scratch_shapes=[pltpu.VMEM((tm, d), BF16), pltpu.VMEM((tm, d), F32)],
        compiler_params=pltpu.CompilerParams(dimension_semantics=("arbitrary", "arbitrary"),
                                             vmem_limit_bytes=VMEM_LIMIT),
        name="mlp",
    )(x2, g2, w_up, w_down)


PAGES_PER_STEP = 16
SEQS_PER_STEP = 2


def _head_rows(row):
    x = jnp.broadcast_to(row, (8, GROUP_WIDTH))
    return jnp.where(_iota((8, GROUP_WIDTH), 1) // HEAD_DIM == _iota((8, GROUP_WIDTH), 0), x, 0.0)


def _dec_attn_kernel(pt_ref, q_ref, kn_ref, vn_ref, sn_ref, *refs, fox, ns, pp, page):
    n_refs = ns * pp
    k_refs = refs[:n_refs]
    v_refs = refs[n_refs:2 * n_refs]
    rest = refs[2 * n_refs:]
    if fox:
        lf_refs = rest[:n_refs]
        rest = rest[n_refs:]
    o_ref, m_ref, l_ref, acc_ref, carry_ref = rest
    j = pl.program_id(1)
    seqs = range(ns)
    qbs = [_head_rows(q_ref[s]).astype(BF16) for s in seqs]
    r = _iota((page, page), 0)
    c = _iota((page, page), 1)
    later = jnp.where(r > c, 1.0, 0.0).astype(BF16)

    @pl.when(j == 0)
    def _():
        if fox:
            l_ref[...] = jnp.ones_like(l_ref)
            lane = _iota((8, SMALL_WIDTH), 1)
            row = _iota((8, SMALL_WIDTH), 0)
            for s in seqs:
                kn = jnp.broadcast_to(kn_ref[s], (8, GROUP_WIDTH)).astype(BF16)
                m_ref[s] = jnp.broadcast_to(_dot_nt(qbs[s], kn)[:, 0:1], m_ref.shape[1:])
                acc_ref[s] = jnp.broadcast_to(vn_ref[s], acc_ref.shape[1:]).astype(BF16).astype(F32)
                sn = jnp.broadcast_to(sn_ref[s], (8, SMALL_WIDTH))
                lf_new = jnp.sum(jnp.where(lane == row + LANE_FLOG, sn, 0.0), axis=-1, keepdims=True)
                carry_ref[s] = jnp.broadcast_to(jnp.where(_iota((8, 1), 0) < N_HEADS, lf_new, 0.0),
                                                carry_ref.shape[1:])
        else:
            acc_ref[...] = jnp.zeros_like(acc_ref)
            carry_ref[...] = jnp.zeros_like(carry_ref)

    g8 = pp * 8
    n8 = ns * g8
    per_page = lambda vals: jnp.concatenate([v for s in seqs for v in [vals[s]] * pp], axis=0)
    z = jnp.concatenate([_dot(qbs[s], k_refs[s * pp + i][0].astype(BF16)) for s in seqs for i in range(pp)], axis=0)
    if fox:
        pad = jnp.zeros((8 - N_HEADS, page), F32)
        x = jnp.concatenate([a for i in range(n_refs) for a in (lf_refs[i][0], pad)], axis=0)
    else:
        x = _log_sigmoid(-z)
    ri = _iota((n8, n8), 0)
    ci = _iota((n8, n8), 1)
    before = jnp.where((ri % 8 == ci % 8) & (ri // g8 == ci // g8) & (ci // 8 < ri // 8), 1.0, 0.0).astype(BF16)
    tot = _dot_xc(x, jnp.ones((page, page), BF16))
    upto = _dot_cx(before, tot) + per_page([carry_ref[s] for s in seqs])
    bias = _dot_xc(x, later) + upto
    new_carry = upto + tot
    for s in seqs:
        carry_ref[s] = new_carry[(s + 1) * g8 - 8:(s + 1) * g8, :]
    if fox:
        sc = z + bias
        m_prev = m_ref[...]
        m_new = jnp.maximum(m_prev, jnp.max(jnp.max(sc.reshape(ns, pp, 8, page), axis=1), axis=-1, keepdims=True))
        alpha = jnp.exp(m_prev - m_new)
        p = jnp.exp(sc - per_page([m_new[s] for s in seqs]))
        l_ref[...] = alpha * l_ref[...] + jnp.sum(jnp.sum(p.reshape(ns, pp, 8, page), axis=1), axis=-1, keepdims=True)
        m_ref[...] = m_new
    else:
        p = jnp.exp(z + x + bias)
    for s in seqs:
        pv = None
        for i in range(pp):
            row0 = (s * pp + i) * 8
            term = _dot_nt(p[row0:row0 + 8].astype(BF16), v_refs[s * pp + i][0].astype(BF16))
            pv = term if pv is None else pv + term
        if fox:
            acc_ref[s] = alpha[s][:, 0:1] * acc_ref[s] + pv
        else:
            acc_ref[s] = acc_ref[s] + pv

    @pl.when(j == pl.num_programs(1) - 1)
    def _():
        own = _iota((8, GROUP_WIDTH), 1) // HEAD_DIM == _iota((8, GROUP_WIDTH), 0)
        for s in seqs:
            acc = acc_ref[s]
            if fox:
                acc = acc / l_ref[s][:, 0:1]
            o_ref[s] = jnp.sum(jnp.where(own, acc, 0.0), axis=0, keepdims=True)


def _dec_attn(page_table, q, k_new, v_new, small, cache_k, cache_v, cache_lf_t, *, layer, fox):
    nb = q.shape[0]
    n_pages = page_table.shape[1]
    page = cache_k.shape[3]
    assert page == 128, "per-page statistics are kept one page per vreg row group"
    pp = _pick(n_pages, (PAGES_PER_STEP, 8, 4, 2, 1))
    ns = _pick(nb, (SEQS_PER_STEP, 1))
    row3 = lambda a: a.reshape(nb, 1, a.shape[-1])
    rspec = lambda w: pl.BlockSpec((ns, 1, w), lambda b, j, pt: (b, 0, 0))

    def page_map(s, i):
        return lambda b, j, pt: (layer, pt[b * ns + s, n_pages - 1 - (j * pp + i)], 0, 0)

    slots = [(s, i) for s in range(ns) for i in range(pp)]
    in_specs = [rspec(GROUP_WIDTH)] * 3 + [rspec(SMALL_WIDTH)]
    in_specs += [pl.BlockSpec((None, 1, GROUP_WIDTH, page), page_map(s, i)) for s, i in slots] * 2
    args = [row3(q), row3(k_new), row3(v_new), row3(small)] + [cache_k] * len(slots) + [cache_v] * len(slots)
    if fox:
        in_specs += [pl.BlockSpec((None, 1, N_HEADS, page), page_map(s, i)) for s, i in slots]
        args += [cache_lf_t] * len(slots)
    grid_spec = pltpu.PrefetchScalarGridSpec(
        num_scalar_prefetch=1, grid=(nb // ns, n_pages // pp), in_specs=in_specs,
        out_specs=pl.BlockSpec((ns, 1, GROUP_WIDTH), lambda b, j, pt: (b, 0, 0)),
        scratch_shapes=[pltpu.VMEM((ns, 8, 128), F32), pltpu.VMEM((ns, 8, 128), F32),
                        pltpu.VMEM((ns, 8, GROUP_WIDTH), F32), pltpu.VMEM((ns, 8, 128), F32)])
    out = pl.pallas_call(
        functools.partial(_dec_attn_kernel, fox=fox, ns=ns, pp=pp, page=page), grid_spec=grid_spec,
        out_shape=jax.ShapeDtypeStruct((nb, 1, GROUP_WIDTH), F32),
        compiler_params=pltpu.CompilerParams(dimension_semantics=("arbitrary", "arbitrary"),
                                             vmem_limit_bytes=VMEM_LIMIT),
        name="fox_step" if fox else "sb_step",
    )(page_table, *args)
    return out.reshape(nb, GROUP_WIDTH)


def _column(row, eye):
    return jnp.sum(eye * row, axis=1, keepdims=True)


def _rec_step_kernel(hq_ref, hlf_ref, hk_ref, hi_ref, dx_ref, sm_ref, w_ref, shg_ref, sdn_ref, buf_ref,
                     ohg_ref, odn_ref, shg_o_ref, sdn_o_ref, buf_o_ref):
    eye = jnp.where(_iota((HEAD_DIM, HEAD_DIM), 0) == _iota((HEAD_DIM, HEAD_DIM), 1), 1.0, 0.0)
    hq = hq_ref[0]
    hlf = hlf_ref[0]
    hk = hk_ref[0]
    hv = hi_ref[0]
    sm = sm_ref[0]
    buf = buf_ref[0]
    x_new = dx_ref[0]
    y = x_new * w_ref[CONV_WIDTH - 1:CONV_WIDTH, :]
    for jw in range(CONV_WIDTH - 1):
        y = y + buf[jw:jw + 1, :] * w_ref[jw:jw + 1, :]
    y = _silu(y)
    buf_o_ref[0] = jnp.concatenate([buf[1:CONV_WIDTH - 1, :], x_new], axis=0)
    for h in range(N_HEADS):
        cs = slice(h * HEAD_DIM, (h + 1) * HEAD_DIM)
        s = shg_ref[0, h]
        s = _column(jnp.exp(hlf[:, cs]), eye) * s + _column(hk[:, cs], eye) * hv[:, cs]
        shg_o_ref[0, h] = s
        ohg_ref[0, :, cs] = jnp.sum(_column(hq[:, cs], eye) * s, axis=0, keepdims=True)
        q = y[:, cs]
        k = y[:, GROUP_WIDTH + h * HEAD_DIM:GROUP_WIDTH + (h + 1) * HEAD_DIM]
        v = y[:, 2 * GROUP_WIDTH + h * HEAD_DIM:2 * GROUP_WIDTH + (h + 1) * HEAD_DIM]
        q = q * lax.rsqrt(jnp.sum(q * q, axis=-1, keepdims=True) + NORM_EPS) * QK_SCALE
        k = k * lax.rsqrt(jnp.sum(k * k, axis=-1, keepdims=True) + NORM_EPS)
        beta = sm[:, LANE_BETA + h:LANE_BETA + h + 1]
        a = jnp.exp(sm[:, LANE_GDEC + h:LANE_GDEC + h + 1])
        s = sdn_ref[0, h]
        kc = _column(k, eye)
        v_new = beta * (v - a * jnp.sum(kc * s, axis=0, keepdims=True))
        s = a * s + kc * v_new
        sdn_o_ref[0, h] = s
        odn_ref[0, :, cs] = jnp.sum(_column(q, eye) * s, axis=0, keepdims=True)


def _rec_step(hq, hlf, hk, hi, dqkv, small, conv_w, s_hg, s_dn, buf):
    nb = hq.shape[0]
    row3 = lambda a: a.reshape(nb, 1, a.shape[-1])
    rspec = lambda w: pl.BlockSpec((1, 1, w), lambda b: (b, 0, 0))
    st_spec = pl.BlockSpec((1, N_HEADS, HEAD_DIM, HEAD_DIM), lambda b: (b, 0, 0, 0))
    buf_spec = pl.BlockSpec((1, CONV_WIDTH - 1, 3 * GROUP_WIDTH), lambda b: (b, 0, 0))
    outs = pl.pallas_call(
        _rec_step_kernel, grid=(nb,),
        in_specs=[rspec(GROUP_WIDTH)] * 4 + [rspec(3 * GROUP_WIDTH), rspec(SMALL_WIDTH),
                                             pl.BlockSpec(conv_w.shape, lambda b: (0, 0)), st_spec, st_spec, buf_spec],
        out_specs=[rspec(GROUP_WIDTH), rspec(GROUP_WIDTH), st_spec, st_spec, buf_spec],
        out_shape=[jax.ShapeDtypeStruct((nb, 1, GROUP_WIDTH), F32)] * 2
        + [jax.ShapeDtypeStruct(s_hg.shape, F32), jax.ShapeDtypeStruct(s_dn.shape, F32),
           jax.ShapeDtypeStruct(buf.shape, F32)],
        compiler_params=pltpu.CompilerParams(dimension_semantics=("arbitrary",), vmem_limit_bytes=VMEM_LIMIT),
        name="recurrent_step",
    )(row3(hq), row3(hlf), row3(hk), row3(hi), row3(dqkv), row3(small), conv_w, s_hg, s_dn, buf)
    ohg, odn, s_hg_new, s_dn_new, buf_new = outs
    return ohg.reshape(nb, GROUP_WIDTH), odn.reshape(nb, GROUP_WIDTH), s_hg_new, s_dn_new, buf_new


def _tile_gain(g):
    return jnp.tile(g.astype(F32), N_HEADS)


def _relayout_w_in(w_in_t_l):
    gw = GROUP_WIDTH
    a = 7 * gw
    e = a + N_HEADS + 7 * gw
    pad = jnp.zeros((SMALL_WIDTH - 3 * N_HEADS, w_in_t_l.shape[1]), w_in_t_l.dtype)
    rows = [w_in_t_l[:a], w_in_t_l[a + N_HEADS:e], w_in_t_l[a:a + N_HEADS], w_in_t_l[e:e + 2 * N_HEADS], pad]
    return jnp.concatenate(rows, axis=0).astype(BF16)


def _small_params(f_bias, dt_bias, a_log):
    sp = jnp.zeros((8, SMALL_WIDTH), F32)
    sp = sp.at[0, LANE_FLOG:LANE_FLOG + N_HEADS].set(f_bias.astype(F32))
    sp = sp.at[1, LANE_GDEC:LANE_GDEC + N_HEADS].set(dt_bias.astype(F32))
    sp = sp.at[2, LANE_GDEC:LANE_GDEC + N_HEADS].set(a_log.astype(F32))
    return sp


def _pick(n, candidates):
    for c in candidates:
        if n % c == 0:
            return c
    return n


def kernel(x_prompt, x_sample, cache_fox_k, cache_fox_v, cache_fox_logf, cache_sb_k, cache_sb_v, state_hgrn, state_dn, state_dn_conv, page_table, hgrn_lb_param, w_in, w_out, ln1_g, ln2_g, fox_f_bias, fox_q_norm, fox_k_norm, sb_q_norm, sb_k_norm, hgrn_out_norm, fox_out_norm, sb_out_norm, dn_out_norm, dn_conv_w, dn_dt_bias, dn_a_log, w_up, w_down):
    depth = w_in.shape[0]
    bsz, seq, d = x_prompt.shape
    nb = x_sample.shape[0]
    n_phys, page = cache_fox_k.shape[1], cache_fox_k.shape[2]
    m = bsz * seq
    tm = _pick(seq, (256, 128, 64, 32, 16, 8))
    tm_mlp = _pick(m, (512, 256, 128, 64, 32, 16, 8))
    tf = _pick(w_up.shape[2], (1024, 512, 256, 128))
    tq = _pick(seq, (512, 256, 128))
    tt = _pick(seq, (256, 128, 64))

    yp = x_prompt.reshape(m, d)
    ys = x_sample.reshape(nb, d)
    lbp = hgrn_lb_param.astype(F32)
    w_in_t = jnp.transpose(w_in, (2, 0, 1))
    kv_t = lambda a: a.transpose(0, 1, 3, 4, 2).reshape(depth, n_phys, GROUP_WIDTH, page)
    lf_t = jnp.swapaxes(cache_fox_logf.astype(F32), 2, 3)
    p_out = [[] for _ in range(8)]
    s_out = [[] for _ in range(8)]
    for l in range(depth):
        w_re = _relayout_w_in(w_in_t[:, l, :])
        w_o = w_out[l].astype(BF16)
        w_u = w_up[l].astype(BF16)
        w_d = w_down[l].astype(BF16)
        g1 = ln1_g[l].reshape(1, d).astype(F32)
        g2 = ln2_g[l].reshape(1, d).astype(F32)
        qk_gains = jnp.stack([_tile_gain(fox_q_norm[l]), _tile_gain(fox_k_norm[l]),
                              _tile_gain(sb_q_norm[l]), _tile_gain(sb_k_norm[l])])
        out_gains = jnp.stack([_tile_gain(hgrn_out_norm[l]), _tile_gain(fox_out_norm[l]),
                               _tile_gain(sb_out_norm[l]), _tile_gain(dn_out_norm[l])])
        sp = _small_params(fox_f_bias[l], dn_dt_bias[l], dn_a_log[l])
        conv_w = dn_conv_w[l].astype(F32)

        (hq, hlf, hk, hi, hgate, fq, fk, fv, sq, sk, sv, dqkv, dz, small, cum, small_t, cum_t) = _inproj(
            yp, g1, w_re, lbp, qk_gains, sp, layer=l, tm=tm, rows_per_seq=seq, with_time=True)
        b3 = lambda a: a.reshape(bsz, seq, a.shape[-1])
        o_hg, st_hg = _hgrn_prompt(b3(hq), b3(hlf), b3(hk), b3(hi), tt=tt)
        o_fx = _fox_prompt(b3(fq), fk, fv, b3(cum), cum_t, tq=tq)
        o_sb = _sb_prompt(b3(sq), sk, sv, tq=tq)
        o_dn, st_dn = _gdn_prompt(b3(dqkv), conv_w, b3(small), small_t, tt=tt)
        yp = _outproj(yp, o_hg.reshape(m, -1), hgate, o_fx.reshape(m, -1), o_sb.reshape(m, -1),
                      o_dn.reshape(m, -1), dz, out_gains, w_o, tm=tm)
        yp = _mlp(yp, g2, w_u, w_d, tm=tm_mlp, tf=tf)
        by_head = lambda a: a.reshape(bsz, N_HEADS, HEAD_DIM, seq).transpose(0, 3, 1, 2)
        p_out[0].append(by_head(fk))
        p_out[1].append(by_head(fv))
        p_out[2].append(b3(small)[:, :, LANE_FLOG:LANE_FLOG + N_HEADS])
        p_out[3].append(by_head(sk))
        p_out[4].append(by_head(sv))
        p_out[5].append(jnp.swapaxes(st_hg, -1, -2))
        p_out[6].append(st_dn)
        p_out[7].append(b3(dqkv)[:, seq - (CONV_WIDTH - 1):, :])

        (hq, hlf, hk, hi, hgate, fq, fk, fv, sq, sk, sv, dqkv, dz, small) = _inproj(
            ys, g1, w_re, lbp, qk_gains, sp, layer=l, tm=nb, rows_per_seq=nb, with_time=False)
        o_fx = _dec_attn(page_table, fq, fk, fv, small, kv_t(cache_fox_k), kv_t(cache_fox_v), lf_t,
                         layer=l, fox=True)
        o_sb = _dec_attn(page_table, sq, sk, sv, small, kv_t(cache_sb_k), kv_t(cache_sb_v), None,
                         layer=l, fox=False)
        o_hg, o_dn, s_hg_new, s_dn_new, buf_new = _rec_step(
            hq, hlf, hk, hi, dqkv, small, conv_w, state_hgrn[l].astype(F32), state_dn[l].astype(F32),
            state_dn_conv[l].astype(F32))
        ys = _outproj(ys, o_hg, hgate, o_fx, o_sb, o_dn, dz, out_gains, w_o, tm=nb)
        ys = _mlp(ys, g2, w_u, w_d, tm=nb, tf=tf)
        sshape = (nb, 1, N_HEADS, HEAD_DIM)
        s_out[0].append(fk.reshape(sshape))
        s_out[1].append(fv.reshape(sshape))
        s_out[2].append(small[:, LANE_FLOG:LANE_FLOG + N_HEADS].reshape(nb, 1, N_HEADS))
        s_out[3].append(sk.reshape(sshape))
        s_out[4].append(sv.reshape(sshape))
        s_out[5].append(s_hg_new)
        s_out[6].append(s_dn_new)
        s_out[7].append(buf_new)

    p = [jnp.stack(v) for v in p_out]
    s = [jnp.stack(v) for v in s_out]
    return (yp.reshape(bsz, seq, d), ys.reshape(nb, 1, d), *p, *s)
```

```python
import functools

import jax
import jax.numpy as jnp
from jax import lax
from jax.experimental import pallas as pl
from jax.experimental.pallas import tpu as pltpu

F32 = jnp.float32
BF16 = jnp.bfloat16

HEAD_DIM = 64
N_HEADS = 4
GROUP_WIDTH = N_HEADS * HEAD_DIM
N_SEGMENTS = 14
SMALL_WIDTH = 128
CONV_WIDTH = 4
NORM_EPS = 1e-6
NEG_BIG = -1e30
QK_SCALE = HEAD_DIM ** -0.5
LOG2E = 1.4426950408889634
INV_LN2 = LOG2E
PRUNE_LOG2 = 160.0
BOUND_SLACK = 1.001
VMEM_LIMIT = 56 * 1024 * 1024

LANE_FLOG = 0
LANE_BETA = 4
LANE_GDEC = 8
SMALL_ROWS_T = 16


def _iota(shape, dim):
    return lax.broadcasted_iota(jnp.int32, shape, dim)


def _dot(a, b):
    return jnp.dot(a, b, preferred_element_type=F32)


def _dot_nt(a, b):
    return lax.dot_general(a, b, (((1,), (1,)), ((), ())), preferred_element_type=F32)


def _split3(x):
    hi = x.astype(BF16)
    r = x - hi.astype(F32)
    mid = r.astype(BF16)
    lo = (r - mid.astype(F32)).astype(BF16)
    return hi, mid, lo


def _dot_xc(x, c, parts=3):
    ps = _split3(x)[:parts]
    out = _dot(ps[0], c)
    for p in ps[1:]:
        out = out + _dot(p, c)
    return out


def _dot_cx(c, x, parts=3):
    ps = _split3(x)[:parts]
    out = _dot(c, ps[0])
    for p in ps[1:]:
        out = out + _dot(c, p)
    return out


def _dot_f32(a, b):
    ah = a.astype(BF16)
    al = (a - ah.astype(F32)).astype(BF16)
    bh = b.astype(BF16)
    bl = (b - bh.astype(F32)).astype(BF16)
    return _dot(ah, bh) + _dot(ah, bl) + _dot(al, bh)


def _head_ones(n=GROUP_WIDTH):
    return (_iota((n, n), 0) // HEAD_DIM == _iota((n, n), 1) // HEAD_DIM).astype(BF16)


def _head_sum(x, ones):
    return _dot_xc(x, ones)


def _head_rms(x, ones, gain):
    ms = _head_sum(x * x, ones) * (1.0 / HEAD_DIM)
    return x * lax.rsqrt(ms + NORM_EPS) * gain


def _log_sigmoid(x):
    return jnp.minimum(x, 0.0) - jnp.log1p(jnp.exp(-jnp.abs(x)))


def _softplus(x):
    return jnp.maximum(x, 0.0) + jnp.log1p(jnp.exp(-jnp.abs(x)))


def _sigmoid(x):
    return 1.0 / (1.0 + jnp.exp(-x))


def _silu(x):
    return x * _sigmoid(x)


def _max_all(x):
    return jnp.max(jnp.max(x, axis=0, keepdims=True), axis=1, keepdims=True)


def _lane_pack(vals):
    lane = _iota((1, 128), 1)
    out = jnp.zeros((1, 128), F32)
    for i, v in enumerate(vals):
        out = jnp.where(lane == i, v, out)
    return out


def _any_head(mask):
    lane = _iota(mask.shape, 1)
    return jnp.max(jnp.where(mask & (lane < N_HEADS), 1.0, 0.0)) > 0.0


def _inproj_kernel(x_ref, g1_ref, w_ref, lbp_ref, gains_ref, sp_ref, *refs, layer, depth, tiles_per_seq,
                   with_time):
    (hq_ref, hlf_ref, hk_ref, hi_ref, hg_ref, fq_ref, fk_ref, fv_ref, sq_ref, sk_ref, sv_ref,
     dqkv_ref, dz_ref, small_ref) = refs[:14]
    x = x_ref[...]
    h = (x * lax.rsqrt(jnp.mean(x * x, axis=-1, keepdims=True) + NORM_EPS) * g1_ref[...]).astype(BF16)

    def seg(j, width=GROUP_WIDTH):
        return _dot_nt(h, w_ref[j * GROUP_WIDTH:j * GROUP_WIDTH + width, :])

    ones = _head_ones()

    rows = [lbp_ref[i:i + 1, :] for i in range(depth)]
    mx = functools.reduce(jnp.maximum, rows)
    es = [jnp.exp(r - mx) for r in rows]
    lb = sum(es[1:layer + 1], jnp.zeros_like(mx)) / sum(es)
    hq_ref[...] = seg(0)
    hf = seg(1)
    a = jnp.log(lb)
    b = jnp.log1p(-lb) + _log_sigmoid(hf)
    hi = jnp.maximum(a, b)
    lo = jnp.minimum(a, b)
    hlf_ref[...] = hi + jnp.log1p(jnp.exp(lo - hi))
    hk_ref[...] = (1.0 - lb) * _sigmoid(-hf)
    hi_ref[...] = seg(2)
    hg_ref[...] = _silu(seg(3))

    fq_ref[...] = _head_rms(seg(4), ones, gains_ref[0:1, :]) * QK_SCALE
    sq_ref[...] = _head_rms(seg(7), ones, gains_ref[2:3, :]) * QK_SCALE
    kv = (_head_rms(seg(5), ones, gains_ref[1:2, :]), seg(6), _head_rms(seg(8), ones, gains_ref[3:4, :]), seg(9))
    for idx, (ref, val) in enumerate(zip((fk_ref, fv_ref, sk_ref, sv_ref), kv)):
        if with_time:
            val_t = val.T
            ref[0] = val_t
            refs[17 + idx][0, 0] = val_t.astype(BF16)
        else:
            ref[...] = val

    dqkv_ref[...] = seg(10, 3 * GROUP_WIDTH)
    dz_ref[...] = _silu(seg(13))

    s = _dot_nt(h, w_ref[N_SEGMENTS * GROUP_WIDTH:, :])
    lane = _iota(s.shape, 1)
    f_log = _log_sigmoid(s + sp_ref[0:1, :])
    beta = _sigmoid(s)
    g_dec = -jnp.exp(sp_ref[2:3, :]) * _softplus(s + sp_ref[1:2, :])
    small = jnp.where(lane < LANE_BETA, f_log, jnp.where(lane < LANE_GDEC, beta, g_dec))
    small_ref[...] = small

    if with_time:
        cum_ref, small_t_ref, cum_t_ref = refs[14:17]
        carry_ref = refs[21]
        tm = s.shape[0]

        @pl.when(pl.program_id(0) % tiles_per_seq == 0)
        def _():
            carry_ref[...] = jnp.zeros_like(carry_ref)

        tril = (_iota((tm, tm), 1) <= _iota((tm, tm), 0)).astype(BF16)
        cum = _dot_cx(tril, small) + carry_ref[0:1, :]
        cum_ref[...] = cum
        carry_ref[...] = jnp.broadcast_to(cum[tm - 1:tm, :], carry_ref.shape)
        small_t_ref[0] = small.T[:SMALL_ROWS_T, :]
        cum_t_ref[0] = cum.T[:SMALL_ROWS_T, :]


def _inproj(x2, g1, w_re, lbp, gains, sp, *, layer, tm, rows_per_seq, with_time, key_block=None):
    m, d = x2.shape
    depth = lbp.shape[0]
    grid = (m // tm,)
    row = lambda i: (i, 0)
    const = lambda i: (0, 0)
    tps = rows_per_seq // tm
    nseq = m // rows_per_seq
    seg_shape = jax.ShapeDtypeStruct((m, GROUP_WIDTH), F32)
    seg_spec = pl.BlockSpec((tm, GROUP_WIDTH), row)
    out_shape = [seg_shape] * 11 + [jax.ShapeDtypeStruct((m, 3 * GROUP_WIDTH), F32), seg_shape,
                                    jax.ShapeDtypeStruct((m, SMALL_WIDTH), F32)]
    out_specs = [seg_spec] * 11 + [pl.BlockSpec((tm, 3 * GROUP_WIDTH), row), seg_spec,
                                   pl.BlockSpec((tm, SMALL_WIDTH), row)]
    scratch = []
    if with_time:
        def by_time(rows):
            return (jax.ShapeDtypeStruct((nseq, rows, rows_per_seq), F32),
                    pl.BlockSpec((1, rows, tm), lambda i: (i // tps, 0, i % tps)))
        for idx in (6, 7, 9, 10):
            out_shape[idx], out_specs[idx] = by_time(GROUP_WIDTH)
        out_shape += [jax.ShapeDtypeStruct((m, SMALL_WIDTH), F32)]
        out_specs += [pl.BlockSpec((tm, SMALL_WIDTH), row)]
        for _ in range(2):
            sh, sp_ = by_time(SMALL_ROWS_T)
            out_shape.append(sh)
            out_specs.append(sp_)
        per_blk = key_block // tm
        for _ in range(4):
            out_shape.append(jax.ShapeDtypeStruct((nseq, rows_per_seq // key_block, GROUP_WIDTH, key_block), BF16))
            out_specs.append(pl.BlockSpec((1, 1, GROUP_WIDTH, tm),
                                          lambda i: (i // tps, (i % tps) // per_blk, 0, (i % tps) % per_blk)))
        scratch = [pltpu.VMEM((8, SMALL_WIDTH), F32)]
    kern = functools.partial(_inproj_kernel, layer=layer, depth=depth, tiles_per_seq=tps,
                             with_time=with_time)
    return pl.pallas_call(
        kern, grid=grid,
        in_specs=[pl.BlockSpec((tm, d), row), pl.BlockSpec((1, d), const), pl.BlockSpec(w_re.shape, const),
                  pl.BlockSpec(lbp.shape, const), pl.BlockSpec(gains.shape, const), pl.BlockSpec(sp.shape, const)],
        out_specs=out_specs, out_shape=out_shape, scratch_shapes=scratch,
        compiler_params=pltpu.CompilerParams(dimension_semantics=("arbitrary",), vmem_limit_bytes=VMEM_LIMIT),
        name="inproj_time" if with_time else "inproj_step",
    )(x2, g1, w_re, lbp, gains, sp)


HGRN_SUB = 16


def _hgrn_kernel(q_ref, lf_ref, k_ref, v_ref, o_ref, st_ref, s_ref, oi_ref, *, tt):
    t = pl.program_id(1)

    @pl.when(t == 0)
    def _():
        s_ref[...] = jnp.zeros_like(s_ref)

    q = q_ref[0]
    lf = lf_ref[0]
    kin = k_ref[0]
    v = v_ref[0]
    r = _iota((tt, tt), 0)
    c = _iota((tt, tt), 1)
    same = (r // HGRN_SUB) == (c // HGRN_SUB)
    g = _dot_cx(jnp.where(same & (c <= r), 1.0, 0.0).astype(BF16), lf)
    gl = _dot_cx(jnp.where(same, 1.0, 0.0).astype(BF16), lf)
    qg = q * jnp.exp(g)
    kg = kin * jnp.exp(gl - g)

    ones = _head_ones()
    rowmod = _iota((tt, GROUP_WIDTH), 0) % HGRN_SUB
    o = jnp.zeros((tt, GROUP_WIDTH), F32)
    for d in range(HGRN_SUB):
        if d == 0:
            kd, gd, vd = kin, g, v
        else:
            kd = pltpu.roll(kin, d, 0)
            gd = pltpu.roll(g, d, 0)
            vd = pltpu.roll(v, d, 0)
        e = jnp.where(rowmod >= d, g - gd, NEG_BIG)
        p = q * kd * jnp.exp(e)
        o = o + _dot_xc(p, ones, parts=2) * vd

    v_t = v.T
    for i in range(tt // HGRN_SUB):
        rs = slice(i * HGRN_SUB, (i + 1) * HGRN_SUB)
        for h in range(N_HEADS):
            cs = slice(h * HEAD_DIM, (h + 1) * HEAD_DIM)
            s = s_ref[h]
            oi_ref[rs, cs] = _dot_nt(qg[rs, cs].astype(BF16), s.astype(BF16))
            dec = jnp.exp(gl[i * HGRN_SUB:i * HGRN_SUB + 1, cs])
            s_ref[h] = dec * s + _dot(v_t[cs, rs].astype(BF16), kg[rs, cs].astype(BF16))
    o_ref[0] = o + oi_ref[...]

    @pl.when(t == pl.num_programs(1) - 1)
    def _():
        st_ref[0] = s_ref[...]


def _hgrn_prompt(hq, hlf, hk, hi, *, tt):
    b, t, _ = hq.shape
    blk = pl.BlockSpec((1, tt, GROUP_WIDTH), lambda i, j: (i, j, 0))
    return pl.pallas_call(
        functools.partial(_hgrn_kernel, tt=tt), grid=(b, t // tt),
        in_specs=[blk] * 4,
        out_specs=[blk, pl.BlockSpec((1, N_HEADS, HEAD_DIM, HEAD_DIM), lambda i, j: (i, 0, 0, 0))],
        out_shape=[jax.ShapeDtypeStruct((b, t, GROUP_WIDTH), F32),
                   jax.ShapeDtypeStruct((b, N_HEADS, HEAD_DIM, HEAD_DIM), F32)],
        scratch_shapes=[pltpu.VMEM((N_HEADS, HEAD_DIM, HEAD_DIM), F32), pltpu.VMEM((tt, GROUP_WIDTH), F32)],
        compiler_params=pltpu.CompilerParams(dimension_semantics=("arbitrary", "arbitrary"),
                                             vmem_limit_bytes=VMEM_LIMIT),
        name="hgrn_prompt",
    )(hq, hlf, hk, hi)


def _fox_kernel(q_ref, k_ref, v_ref, fq_ref, fk_ref, o_ref, qa_ref, m_ref, acc_ref, qn_ref, fq_max_ref, m_min_ref,
                kn_ref, fk_min_ref, *, tq):
    qi = pl.program_id(1)
    j = pl.program_id(2)

    @pl.when(j == 0)
    def _():
        m_ref[...] = jnp.full_like(m_ref, NEG_BIG)
        acc_ref[...] = jnp.zeros_like(acc_ref)
        q = q_ref[0] * LOG2E
        f = fq_ref[0] * LOG2E
        lane = _iota((tq, HEAD_DIM), 1)
        for h in range(N_HEADS):
            hi, mid, lo = [p.astype(F32) for p in _split3(f[:, LANE_FLOG + h:LANE_FLOG + h + 1])]
            ext = jnp.where(lane == 0, hi, jnp.where(lane == 1, mid, jnp.where(lane == 2, lo,
                                                                               jnp.where(lane < 6, 1.0, 0.0))))
            qa_ref[h] = jnp.concatenate([q[:, h * HEAD_DIM:(h + 1) * HEAD_DIM], ext], axis=1).astype(BF16)

        heads = range(N_HEADS)
        rows = lambda h: slice(h * HEAD_DIM, (h + 1) * HEAD_DIM)
        qb = q.astype(BF16).astype(F32)
        qn2 = _head_sum(qb * qb, _head_ones())
        kf = k_ref[0, 0].astype(F32)
        k2 = kf * kf
        fk = fk_ref[0] * LOG2E
        qn_ref[...] = _lane_pack([_max_all(qn2[:, rows(h)]) for h in heads])
        fq_max_ref[...] = _lane_pack([_max_all(f[:, LANE_FLOG + h:LANE_FLOG + h + 1]) for h in heads])
        kn_ref[pl.ds(qi, 1), :] = _lane_pack([_max_all(jnp.sum(k2[rows(h), :], axis=0, keepdims=True))
                                              for h in heads])
        fk_min_ref[pl.ds(qi, 1), :] = _lane_pack([-_max_all(-fk[LANE_FLOG + h:LANE_FLOG + h + 1, :])
                                                  for h in heads])

    def block_matters():
        zcap2 = qn_ref[...] * kn_ref[pl.ds(qi - j, 1), :] * (BOUND_SLACK * BOUND_SLACK)
        bias_cap = fq_max_ref[...] - fk_min_ref[pl.ds(qi - j, 1), :]
        room = m_min_ref[...] - bias_cap - PRUNE_LOG2
        return _any_head((room < 0.0) | (zcap2 > room * room))

    def step(masked):
        kf = k_ref[0, 0]
        vf = v_ref[0, 0]
        fk = fk_ref[0] * LOG2E
        row = _iota((8, tq), 0)
        pad = jnp.zeros((HEAD_DIM - 8, tq), F32)
        v_ext = jnp.concatenate([jnp.where(row == 0, 1.0, 0.0), pad], axis=0).astype(BF16)
        if masked:
            keep = _iota((tq, tq), 1) <= _iota((tq, tq), 0)
        heads = range(N_HEADS)
        ss = []
        for h in heads:
            cs = slice(h * HEAD_DIM, (h + 1) * HEAD_DIM)
            hi, mid, lo = [p.astype(F32) for p in _split3(fk[LANE_FLOG + h:LANE_FLOG + h + 1, :])]
            k_ext = jnp.where(row < 3, 1.0, jnp.where(row == 3, -hi, jnp.where(row == 4, -mid,
                                                                               jnp.where(row == 5, -lo, 0.0))))
            ka = jnp.concatenate([kf[cs, :], jnp.concatenate([k_ext, pad], axis=0).astype(BF16)], axis=0)
            ss.append(_dot(qa_ref[h], ka))
        if masked:
            ss = [jnp.where(keep, s, NEG_BIG) for s in ss]
        m_prevs = [m_ref[h] for h in heads]
        m_news = [jnp.maximum(m_prevs[h], jnp.max(ss[h], axis=-1, keepdims=True)) for h in heads]
        ps = [jnp.exp2(ss[h] - jnp.concatenate([m_news[h]] * (tq // 128), axis=1)) for h in heads]
        for h in heads:
            va = jnp.concatenate([vf[h * HEAD_DIM:(h + 1) * HEAD_DIM, :], v_ext], axis=0)
            m_ref[h] = m_news[h]
            acc_ref[h] = jnp.exp2(m_prevs[h] - m_news[h]) * acc_ref[h] + _dot_nt(ps[h].astype(BF16), va)
        m_min_ref[...] = _lane_pack([-_max_all(-m_news[h]) for h in heads])

    @pl.when(j == 0)
    def _():
        step(True)

    @pl.when((j > 0) & (j <= qi))
    def _():
        @pl.when(block_matters())
        def _():
            step(False)

    @pl.when(j == qi)
    def _():
        for h in range(N_HEADS):
            acc = acc_ref[h]
            o_ref[0, :, h * HEAD_DIM:(h + 1) * HEAD_DIM] = acc[:, :HEAD_DIM] / acc[:, HEAD_DIM:HEAD_DIM + 1]


def _fox_prompt(fq, fk, fv, cum, cum_t, *, tq):
    b, t, _ = fq.shape
    n = t // tq
    qspec = pl.BlockSpec((1, tq, GROUP_WIDTH), lambda i, qi, j: (i, qi, 0))
    kspec = pl.BlockSpec((1, 1, GROUP_WIDTH, tq), lambda i, qi, j: (i, jnp.maximum(qi - j, 0), 0, 0))
    return pl.pallas_call(
        functools.partial(_fox_kernel, tq=tq), grid=(b, n, n),
        in_specs=[qspec, kspec, kspec,
                  pl.BlockSpec((1, tq, SMALL_WIDTH), lambda i, qi, j: (i, qi, 0)),
                  pl.BlockSpec((1, SMALL_ROWS_T, tq), lambda i, qi, j: (i, 0, jnp.maximum(qi - j, 0)))],
        out_specs=qspec,
        out_shape=jax.ShapeDtypeStruct((b, t, GROUP_WIDTH), F32),
        scratch_shapes=[pltpu.VMEM((N_HEADS, tq, 2 * HEAD_DIM), BF16),
                        pltpu.VMEM((N_HEADS, tq, 128), F32), pltpu.VMEM((N_HEADS, tq, 2 * HEAD_DIM), F32),
                        pltpu.VMEM((1, 128), F32), pltpu.VMEM((1, 128), F32), pltpu.VMEM((1, 128), F32),
                        pltpu.VMEM((n, 128), F32), pltpu.VMEM((n, 128), F32)],
        compiler_params=pltpu.CompilerParams(dimension_semantics=("arbitrary", "arbitrary", "arbitrary"),
                                             vmem_limit_bytes=VMEM_LIMIT),
        name="fox_prompt",
    )(fq, fk, fv, cum, cum_t)


SB_SUB = 256


def _sb_kernel(q_ref, k_ref, v_ref, o_ref, qb_ref, carry_ref, acc_ref, qn_ref, kn_ref, cmin_ref, *, tq):
    qi = pl.program_id(1)
    j = pl.program_id(2)

    heads = range(N_HEADS)
    rows = lambda h: slice(h * HEAD_DIM, (h + 1) * HEAD_DIM)

    @pl.when(j == 0)
    def _():
        carry_ref[...] = jnp.zeros_like(carry_ref)
        acc_ref[...] = jnp.zeros_like(acc_ref)
        qb = (q_ref[0] * LOG2E).astype(BF16)
        qb_ref[...] = qb
        qn2 = _head_sum(qb.astype(F32) * qb.astype(F32), _head_ones())
        kf = k_ref[0, 0].astype(F32)
        k2 = kf * kf
        qn_ref[...] = _lane_pack([_max_all(qn2[:, rows(h)]) for h in heads])
        kn_ref[pl.ds(qi, 1), :] = _lane_pack([_max_all(jnp.sum(k2[rows(h), :], axis=0, keepdims=True))
                                              for h in heads])
        cmin_ref[...] = jnp.zeros_like(cmin_ref)

    def block_matters():
        zcap2 = qn_ref[...] * kn_ref[pl.ds(qi - j, 1), :] * (BOUND_SLACK * BOUND_SLACK)
        room = cmin_ref[...] - PRUNE_LOG2
        return _any_head((room < 0.0) | (zcap2 > room * room))

    def step(masked):
        k = k_ref[0, 0]
        v = v_ref[0, 0]
        r = _iota((tq, tq), 0)
        c = _iota((tq, tq), 1)
        sub = min(SB_SUB, tq)
        suffix = jnp.where(_iota((sub, sub), 1) <= _iota((sub, sub), 0), 1.0, 0.0).astype(BF16)
        z2s = [_dot(qb_ref[:, rows(h)], k[rows(h), :]) for h in heads]
        sps = [jnp.maximum(z2, 0.0) + jnp.log(1.0 + jnp.exp2(-jnp.abs(z2))) * INV_LN2 for z2 in z2s]
        if masked:
            sps = [jnp.where(c < r, sp, 0.0) for sp in sps]
        carries = [carry_ref[h] for h in heads]
        cum_parts = [[None] * (tq // sub) for _ in heads]
        for part in reversed(range(tq // sub)):
            ks = slice(part * sub, (part + 1) * sub)
            for h in heads:
                wide = jnp.concatenate([carries[h]] * (sub // 128), axis=1)
                cum = _dot_xc(sps[h][:, ks], suffix, parts=2) + wide
                cum_parts[h][part] = cum
                carries[h] = jnp.broadcast_to(cum[:, 0:1], carries[h].shape)
        es = [z2s[h] - jnp.concatenate(cum_parts[h], axis=1) for h in heads]
        if masked:
            es = [jnp.where(c < r, e, NEG_BIG) for e in es]
        pad = jnp.zeros((HEAD_DIM, tq), BF16)
        for h in heads:
            va = jnp.concatenate([v[rows(h), :], pad], axis=0)
            acc_ref[h] = acc_ref[h] + _dot_nt(jnp.exp2(es[h]).astype(BF16), va)
            carry_ref[h] = carries[h]
        cmin_ref[...] = _lane_pack([-_max_all(-carries[h]) for h in heads])

    @pl.when(j == 0)
    def _():
        step(True)

    @pl.when((j > 0) & (j <= qi))
    def _():
        @pl.when(block_matters())
        def _():
            step(False)

    @pl.when(j == qi)
    def _():
        for h in range(N_HEADS):
            o_ref[0, :, h * HEAD_DIM:(h + 1) * HEAD_DIM] = acc_ref[h][:, :HEAD_DIM]


def _sb_prompt(sq, sk, sv, *, tq):
    b, t, _ = sq.shape
    n = t // tq
    qspec = pl.BlockSpec((1, tq, GROUP_WIDTH), lambda i, qi, j: (i, qi, 0))
    kspec = pl.BlockSpec((1, 1, GROUP_WIDTH, tq), lambda i, qi, j: (i, jnp.maximum(qi - j, 0), 0, 0))
    return pl.pallas_call(
        functools.partial(_sb_kernel, tq=tq), grid=(b, n, n),
        in_specs=[qspec, kspec, kspec],
        out_specs=qspec,
        out_shape=jax.ShapeDtypeStruct((b, t, GROUP_WIDTH), F32),
        scratch_shapes=[pltpu.VMEM((tq, GROUP_WIDTH), BF16),
                        pltpu.VMEM((N_HEADS, tq, 128), F32), pltpu.VMEM((N_HEADS, tq, 2 * HEAD_DIM), F32),
                        pltpu.VMEM((1, 128), F32), pltpu.VMEM((n, 128), F32), pltpu.VMEM((1, 128), F32)],
        compiler_params=pltpu.CompilerParams(dimension_semantics=("arbitrary", "arbitrary", "arbitrary"),
                                             vmem_limit_bytes=VMEM_LIMIT),
        name="sb_prompt",
    )(sq, sk, sv)


GDN_CHUNK = 128
GDN_BASE = 16


def _unit_lower_inverses(lmats, ii, jj):
    ns = [jnp.where(ii // GDN_BASE == jj // GDN_BASE, -lm, 0.0) for lm in lmats]
    eye = jnp.where(ii == jj, 1.0, 0.0)
    ts = [eye + n for n in ns]
    ps = ns
    for _ in range(GDN_BASE.bit_length() - 2):
        ps = [_dot_f32(p, p) for p in ps]
        ts = [t + _dot_f32(t, p) for t, p in zip(ts, ps)]
    b = GDN_BASE
    while b < GDN_CHUNK:
        lower_left = (ii // (2 * b) == jj // (2 * b)) & (ii // b != jj // b)
        mids = [_dot_f32(t, jnp.where(lower_left, lm, 0.0)) for t, lm in zip(ts, lmats)]
        ts = [t - _dot_f32(mid, t) for t, mid in zip(ts, mids)]
        b *= 2
    return ts


def _gdn_prep_kernel(x_ref, w_ref, sm_ref, smt_ref, u_ref, wk_ref, qe_ref, attn_ref, kdt_ref, gct_ref, ext_ref, *, tt):
    t = pl.program_id(1)

    @pl.when(t == 0)
    def _():
        ext_ref[0:8, :] = jnp.zeros((8, 3 * GROUP_WIDTH), F32)

    @pl.when(t > 0)
    def _():
        ext_ref[0:8, :] = ext_ref[tt:tt + 8, :]

    ext_ref[8:8 + tt, :] = x_ref[0]
    y = ext_ref[pl.ds(8 - (CONV_WIDTH - 1), tt), :] * w_ref[0:1, :]
    for jw in range(1, CONV_WIDTH):
        y = y + ext_ref[pl.ds(8 - (CONV_WIDTH - 1) + jw, tt), :] * w_ref[jw:jw + 1, :]
    y = _silu(y)
    ones = _head_ones()
    q = y[:, :GROUP_WIDTH]
    k = y[:, GROUP_WIDTH:2 * GROUP_WIDTH]
    v = y[:, 2 * GROUP_WIDTH:]
    q = q * lax.rsqrt(_head_sum(q * q, ones) + NORM_EPS) * QK_SCALE
    k = k * lax.rsqrt(_head_sum(k * k, ones) + NORM_EPS)
    sm = sm_ref[0]
    smt = smt_ref[0]
    r = _iota((tt, tt), 0)
    c = _iota((tt, tt), 1)
    same = (r // GDN_CHUNK) == (c // GDN_CHUNK)
    gc_col = _dot_cx(jnp.where(same & (c <= r), 1.0, 0.0).astype(BF16), sm)
    gc_row = _dot_xc(smt, jnp.where(same & (r <= c), 1.0, 0.0).astype(BF16))
    gct_ref[0] = gc_row
    k_t = k.T
    ii = _iota((GDN_CHUNK, GDN_CHUNK), 0)
    jj = _iota((GDN_CHUNK, GDN_CHUNK), 1)
    n_chunks = tt // GDN_CHUNK
    lmats, rhss = [], []
    for ci in range(n_chunks):
        rs = slice(ci * GDN_CHUNK, (ci + 1) * GDN_CHUNK)
        qes, attns, kdts = [], [], []
        for h in range(N_HEADS):
            cs = slice(h * HEAD_DIM, (h + 1) * HEAD_DIM)
            gcol = gc_col[rs, LANE_GDEC + h:LANE_GDEC + h + 1]
            grow = gc_row[LANE_GDEC + h:LANE_GDEC + h + 1, rs]
            beta = sm[rs, LANE_BETA + h:LANE_BETA + h + 1]
            dec = jnp.exp(jnp.where(ii >= jj, gcol - grow, NEG_BIG))
            qh = q[rs, cs].astype(BF16)
            kth = k_t[cs, rs]
            kb = k[rs, cs] * beta
            eg = jnp.exp(gcol)
            lmats.append(jnp.where(ii > jj, _dot(kb.astype(BF16), kth.astype(BF16)) * dec, 0.0))
            rhss.append(jnp.concatenate([v[rs, cs] * beta, kb * eg], axis=1))
            qes.append(q[rs, cs] * eg)
            attns.append(_dot(qh, kth.astype(BF16)) * dec)
            kdts.append(kth * jnp.exp(grow[:, GDN_CHUNK - 1:GDN_CHUNK] - grow))
        qe_ref[0, rs, :] = jnp.concatenate(qes, axis=1).astype(BF16)
        attn_ref[0, rs, :] = jnp.concatenate(attns, axis=1).astype(BF16)
        kdt_ref[0, :, rs] = jnp.concatenate(kdts, axis=0).astype(BF16)
    tinvs = _unit_lower_inverses(lmats, ii, jj)
    sols = [_dot_f32(ti, rhs) for ti, rhs in zip(tinvs, rhss)]
    for ci in range(n_chunks):
        rs = slice(ci * GDN_CHUNK, (ci + 1) * GDN_CHUNK)
        chunk = sols[ci * N_HEADS:(ci + 1) * N_HEADS]
        u_ref[0, rs, :] = jnp.concatenate([sol[:, :HEAD_DIM] for sol in chunk], axis=1)
        wk_ref[0, rs, :] = jnp.concatenate([sol[:, HEAD_DIM:] for sol in chunk], axis=1).astype(BF16)


def _gdn_scan_kernel(u_ref, wk_ref, qe_ref, attn_ref, kdt_ref, gct_ref, o_ref, st_ref, s_ref, *, tt, nb):
    t = pl.program_id(0)

    @pl.when(t == 0)
    def _():
        s_ref[...] = jnp.zeros_like(s_ref)

    for ci in range(tt // GDN_CHUNK):
        rs = slice(ci * GDN_CHUNK, (ci + 1) * GDN_CHUNK)
        items = [(b, h) for b in range(nb) for h in range(N_HEADS)]
        cs = lambda h: slice(h * HEAD_DIM, (h + 1) * HEAD_DIM)
        us = [u_ref[b, rs, :] for b in range(nb)]
        wks = [wk_ref[b, rs, :] for b in range(nb)]
        qes = [qe_ref[b, rs, :] for b in range(nb)]
        attns = [attn_ref[b, rs, :] for b in range(nb)]
        kdts = [kdt_ref[b, :, rs] for b in range(nb)]
        decays = [jnp.exp(gct_ref[b, :, rs][:, GDN_CHUNK - 1:GDN_CHUNK]) for b in range(nb)]
        ss = [s_ref[b, h] for b, h in items]
        sbs = [s.astype(BF16) for s in ss]
        v_news = [us[b][:, cs(h)] - _dot(wks[b][:, cs(h)], sb) for (b, h), sb in zip(items, sbs)]
        vbs = [vn.astype(BF16) for vn in v_news]
        for (b, h), s, vb in zip(items, ss, vbs):
            a = decays[b][LANE_GDEC + h:LANE_GDEC + h + 1, :]
            s_ref[b, h] = a * s + _dot(kdts[b][cs(h), :], vb)
        for b in range(nb):
            outs = [_dot(qes[b][:, cs(h)], sbs[b * N_HEADS + h])
                    + _dot(attns[b][:, h * GDN_CHUNK:(h + 1) * GDN_CHUNK], vbs[b * N_HEADS + h])
                    for h in range(N_HEADS)]
            o_ref[b, rs, :] = jnp.concatenate(outs, axis=1)

    @pl.when(t == pl.num_programs(0) - 1)
    def _():
        st_ref[...] = s_ref[...]


def _gdn_prompt(dqkv, conv_w, small, small_t, *, tt):
    b, t, _ = dqkv.shape
    n_attn = N_HEADS * GDN_CHUNK
    by_rows = lambda w: pl.BlockSpec((1, tt, w), lambda i, j: (i, j, 0))
    by_cols = lambda r: pl.BlockSpec((1, r, tt), lambda i, j: (i, 0, j))
    u, wk, qe, attn, kdt, gct = pl.pallas_call(
        functools.partial(_gdn_prep_kernel, tt=tt), grid=(b, t // tt),
        in_specs=[by_rows(3 * GROUP_WIDTH), pl.BlockSpec(conv_w.shape, lambda i, j: (0, 0)),
                  by_rows(SMALL_WIDTH), by_cols(SMALL_ROWS_T)],
        out_specs=[by_rows(GROUP_WIDTH), by_rows(GROUP_WIDTH), by_rows(GROUP_WIDTH), by_rows(n_attn),
                   by_cols(GROUP_WIDTH), by_cols(SMALL_ROWS_T)],
        out_shape=[jax.ShapeDtypeStruct((b, t, GROUP_WIDTH), F32), jax.ShapeDtypeStruct((b, t, GROUP_WIDTH), BF16),
                   jax.ShapeDtypeStruct((b, t, GROUP_WIDTH), BF16), jax.ShapeDtypeStruct((b, t, n_attn), BF16),
                   jax.ShapeDtypeStruct((b, GROUP_WIDTH, t), BF16), jax.ShapeDtypeStruct((b, SMALL_ROWS_T, t), F32)],
        scratch_shapes=[pltpu.VMEM((tt + 8, 3 * GROUP_WIDTH), F32)],
        compiler_params=pltpu.CompilerParams(dimension_semantics=("arbitrary", "arbitrary"),
                                             vmem_limit_bytes=VMEM_LIMIT),
        name="gdn_prep",
    )(dqkv, conv_w, small, small_t)
    all_rows = lambda w: pl.BlockSpec((b, tt, w), lambda j: (0, j, 0))
    all_cols = lambda r: pl.BlockSpec((b, r, tt), lambda j: (0, 0, j))
    st_spec = pl.BlockSpec((b, N_HEADS, HEAD_DIM, HEAD_DIM), lambda j: (0, 0, 0, 0))
    return pl.pallas_call(
        functools.partial(_gdn_scan_kernel, tt=tt, nb=b), grid=(t // tt,),
        in_specs=[all_rows(GROUP_WIDTH), all_rows(GROUP_WIDTH), all_rows(GROUP_WIDTH), all_rows(n_attn),
                  all_cols(GROUP_WIDTH), all_cols(SMALL_ROWS_T)],
        out_specs=[all_rows(GROUP_WIDTH), st_spec],
        out_shape=[jax.ShapeDtypeStruct((b, t, GROUP_WIDTH), F32),
                   jax.ShapeDtypeStruct((b, N_HEADS, HEAD_DIM, HEAD_DIM), F32)],
        scratch_shapes=[pltpu.VMEM((b, N_HEADS, HEAD_DIM, HEAD_DIM), F32)],
        compiler_params=pltpu.CompilerParams(dimension_semantics=("arbitrary",), vmem_limit_bytes=VMEM_LIMIT),
        name="gdn_scan",
    )(u, wk, qe, attn, kdt, gct)


def _outproj_kernel(x_ref, ohg_ref, hg_ref, ofx_ref, osb_ref, odn_ref, dz_ref, gains_ref, w_ref, y_ref):
    ones = _head_ones()
    parts = [
        _head_rms(ohg_ref[...], ones, gains_ref[0:1, :]) * hg_ref[...],
        _head_rms(ofx_ref[...], ones, gains_ref[1:2, :]),
        _head_rms(osb_ref[...], ones, gains_ref[2:3, :]),
        _head_rms(odn_ref[...], ones, gains_ref[3:4, :]) * dz_ref[...],
    ]
    y = x_ref[...]
    for gidx, p in enumerate(parts):
        y = y + _dot(p.astype(BF16), w_ref[gidx * GROUP_WIDTH:(gidx + 1) * GROUP_WIDTH, :])
    y_ref[...] = y


def _outproj(x2, ohg, hgate, ofx, osb, odn, dz, gains, w_out, *, tm):
    m, d = x2.shape
    row = lambda i: (i, 0)
    const = lambda i: (0, 0)
    seg = pl.BlockSpec((tm, GROUP_WIDTH), row)
    return pl.pallas_call(
        _outproj_kernel, grid=(m // tm,),
        in_specs=[pl.BlockSpec((tm, d), row)] + [seg] * 6 + [pl.BlockSpec(gains.shape, const),
                                                            pl.BlockSpec(w_out.shape, const)],
        out_specs=pl.BlockSpec((tm, d), row),
        out_shape=jax.ShapeDtypeStruct((m, d), F32),
        compiler_params=pltpu.CompilerParams(dimension_semantics=("arbitrary",), vmem_limit_bytes=VMEM_LIMIT),
        name="outproj",
    )(x2, ohg, hgate, ofx, osb, odn, dz, gains, w_out)


def _mlp_kernel(x_ref, g2_ref, wu_ref, wd_ref, y_ref, h_ref, acc_ref):
    f = pl.program_id(1)

    @pl.when(f == 0)
    def _():
        x = x_ref[...]
        h_ref[...] = (x * lax.rsqrt(jnp.mean(x * x, axis=-1, keepdims=True) + NORM_EPS) * g2_ref[...]).astype(BF16)
        acc_ref[...] = x

    u = jnp.maximum(_dot(h_ref[...], wu_ref[...]), 0.0)
    acc_ref[...] += _dot((u * u).astype(BF16), wd_ref[...])

    @pl.when(f == pl.num_programs(1) - 1)
    def _():
        y_ref[...] = acc_ref[...]


def _mlp(x2, g2, w_up, w_down, *, tm, tf):
    m, d = x2.shape
    dff = w_up.shape[1]
    return pl.pallas_call(
        _mlp_kernel, grid=(m // tm, dff // tf),
        in_specs=[pl.BlockSpec((tm, d), lambda i, f: (i, 0)), pl.BlockSpec((1, d), lambda i, f: (0, 0)),
                  pl.BlockSpec((d, tf), lambda i, f: (0, f)), pl.BlockSpec((tf, d), lambda i, f: (f, 0))],
        out_specs=pl.BlockSpec((tm, d), lambda i, f: (i, 0)),
        out_shape=jax.ShapeDtypeStruct((m, d), F32),
        scratch_shapes=[pltpu.VMEM((tm, d), BF16), pltpu.VMEM((tm, d), F32)],
        compiler_params=pltpu.CompilerParams(dimension_semantics=("arbitrary", "arbitrary"),
                                             vmem_limit_bytes=VMEM_LIMIT),
        name="mlp",
    )(x2, g2, w_up, w_down)


PAGES_PER_STEP = 16
SEQS_PER_STEP = 2


def _head_rows(row):
    x = jnp.broadcast_to(row, (8, GROUP_WIDTH))
    return jnp.where(_iota((8, GROUP_WIDTH), 1) // HEAD_DIM == _iota((8, GROUP_WIDTH), 0), x, 0.0)


def _dec_attn_kernel(pt_ref, q_ref, kn_ref, vn_ref, sn_ref, *refs, fox, ns, pp, page):
    n_refs = ns * pp
    k_refs = refs[:n_refs]
    v_refs = refs[n_refs:2 * n_refs]
    rest = refs[2 * n_refs:]
    if fox:
        lf_refs = rest[:n_refs]
        rest = rest[n_refs:]
    o_ref, m_ref, l_ref, acc_ref, carry_ref = rest
    j = pl.program_id(1)
    seqs = range(ns)
    qbs = [_head_rows(q_ref[s]).astype(BF16) for s in seqs]
    r = _iota((page, page), 0)
    c = _iota((page, page), 1)
    later = jnp.where(r > c, 1.0, 0.0).astype(BF16)

    @pl.when(j == 0)
    def _():
        if fox:
            l_ref[...] = jnp.ones_like(l_ref)
            lane = _iota((8, SMALL_WIDTH), 1)
            row = _iota((8, SMALL_WIDTH), 0)
            for s in seqs:
                kn = jnp.broadcast_to(kn_ref[s], (8, GROUP_WIDTH)).astype(BF16)
                m_ref[s] = jnp.broadcast_to(_dot_nt(qbs[s], kn)[:, 0:1], m_ref.shape[1:])
                acc_ref[s] = jnp.broadcast_to(vn_ref[s], acc_ref.shape[1:]).astype(BF16).astype(F32)
                sn = jnp.broadcast_to(sn_ref[s], (8, SMALL_WIDTH))
                lf_new = jnp.sum(jnp.where(lane == row + LANE_FLOG, sn, 0.0), axis=-1, keepdims=True)
                carry_ref[s] = jnp.broadcast_to(jnp.where(_iota((8, 1), 0) < N_HEADS, lf_new, 0.0),
                                                carry_ref.shape[1:])
        else:
            acc_ref[...] = jnp.zeros_like(acc_ref)
            carry_ref[...] = jnp.zeros_like(carry_ref)

    g8 = pp * 8
    n8 = ns * g8
    per_page = lambda vals: jnp.concatenate([v for s in seqs for v in [vals[s]] * pp], axis=0)
    z = jnp.concatenate([_dot(qbs[s], k_refs[s * pp + i][0].astype(BF16)) for s in seqs for i in range(pp)], axis=0)
    if fox:
        pad = jnp.zeros((8 - N_HEADS, page), F32)
        x = jnp.concatenate([a for i in range(n_refs) for a in (lf_refs[i][0], pad)], axis=0)
    else:
        x = _log_sigmoid(-z)
    ri = _iota((n8, n8), 0)
    ci = _iota((n8, n8), 1)
    before = jnp.where((ri % 8 == ci % 8) & (ri // g8 == ci // g8) & (ci // 8 < ri // 8), 1.0, 0.0).astype(BF16)
    tot = _dot_xc(x, jnp.ones((page, page), BF16))
    upto = _dot_cx(before, tot) + per_page([carry_ref[s] for s in seqs])
    bias = _dot_xc(x, later) + upto
    new_carry = upto + tot
    for s in seqs:
        carry_ref[s] = new_carry[(s + 1) * g8 - 8:(s + 1) * g8, :]
    if fox:
        sc = z + bias
        m_prev = m_ref[...]
        m_new = jnp.maximum(m_prev, jnp.max(jnp.max(sc.reshape(ns, pp, 8, page), axis=1), axis=-1, keepdims=True))
        alpha = jnp.exp(m_prev - m_new)
        p = jnp.exp(sc - per_page([m_new[s] for s in seqs]))
        l_ref[...] = alpha * l_ref[...] + jnp.sum(jnp.sum(p.reshape(ns, pp, 8, page), axis=1), axis=-1, keepdims=True)
        m_ref[...] = m_new
    else:
        p = jnp.exp(z + x + bias)
    for s in seqs:
        pv = None
        for i in range(pp):
            row0 = (s * pp + i) * 8
            term = _dot_nt(p[row0:row0 + 8].astype(BF16), v_refs[s * pp + i][0].astype(BF16))
            pv = term if pv is None else pv + term
        if fox:
            acc_ref[s] = alpha[s][:, 0:1] * acc_ref[s] + pv
        else:
            acc_ref[s] = acc_ref[s] + pv

    @pl.when(j == pl.num_programs(1) - 1)
    def _():
        own = _iota((8, GROUP_WIDTH), 1) // HEAD_DIM == _iota((8, GROUP_WIDTH), 0)
        for s in seqs:
            acc = acc_ref[s]
            if fox:
                acc = acc / l_ref[s][:, 0:1]
            o_ref[s] = jnp.sum(jnp.where(own, acc, 0.0), axis=0, keepdims=True)


def _dec_attn(page_table, q, k_new, v_new, small, cache_k, cache_v, cache_lf_t, *, layer, fox):
    nb = q.shape[0]
    n_pages = page_table.shape[1]
    page = cache_k.shape[3]
    assert page == 128, "per-page statistics are kept one page per vreg row group"
    pp = _pick(n_pages, (PAGES_PER_STEP, 8, 4, 2, 1))
    ns = _pick(nb, (SEQS_PER_STEP, 1))
    row3 = lambda a: a.reshape(nb, 1, a.shape[-1])
    rspec = lambda w: pl.BlockSpec((ns, 1, w), lambda b, j, pt: (b, 0, 0))

    def page_map(s, i):
        return lambda b, j, pt: (layer, pt[b * ns + s, n_pages - 1 - (j * pp + i)], 0, 0)

    slots = [(s, i) for s in range(ns) for i in range(pp)]
    in_specs = [rspec(GROUP_WIDTH)] * 3 + [rspec(SMALL_WIDTH)]
    in_specs += [pl.BlockSpec((None, 1, GROUP_WIDTH, page), page_map(s, i)) for s, i in slots] * 2
    args = [row3(q), row3(k_new), row3(v_new), row3(small)] + [cache_k] * len(slots) + [cache_v] * len(slots)
    if fox:
        in_specs += [pl.BlockSpec((None, 1, N_HEADS, page), page_map(s, i)) for s, i in slots]
        args += [cache_lf_t] * len(slots)
    grid_spec = pltpu.PrefetchScalarGridSpec(
        num_scalar_prefetch=1, grid=(nb // ns, n_pages // pp), in_specs=in_specs,
        out_specs=pl.BlockSpec((ns, 1, GROUP_WIDTH), lambda b, j, pt: (b, 0, 0)),
        scratch_shapes=[pltpu.VMEM((ns, 8, 128), F32), pltpu.VMEM((ns, 8, 128), F32),
                        pltpu.VMEM((ns, 8, GROUP_WIDTH), F32), pltpu.VMEM((ns, 8, 128), F32)])
    out = pl.pallas_call(
        functools.partial(_dec_attn_kernel, fox=fox, ns=ns, pp=pp, page=page), grid_spec=grid_spec,
        out_shape=jax.ShapeDtypeStruct((nb, 1, GROUP_WIDTH), F32),
        compiler_params=pltpu.CompilerParams(dimension_semantics=("arbitrary", "arbitrary"),
                                             vmem_limit_bytes=VMEM_LIMIT),
        name="fox_step" if fox else "sb_step",
    )(page_table, *args)
    return out.reshape(nb, GROUP_WIDTH)


def _column(row, eye):
    return jnp.sum(eye * row, axis=1, keepdims=True)


def _rec_step_kernel(hq_ref, hlf_ref, hk_ref, hi_ref, dx_ref, sm_ref, w_ref, shg_ref, sdn_ref, buf_ref,
                     ohg_ref, odn_ref, shg_o_ref, sdn_o_ref, buf_o_ref):
    eye = jnp.where(_iota((HEAD_DIM, HEAD_DIM), 0) == _iota((HEAD_DIM, HEAD_DIM), 1), 1.0, 0.0)
    hq = hq_ref[0]
    hlf = hlf_ref[0]
    hk = hk_ref[0]
    hv = hi_ref[0]
    sm = sm_ref[0]
    buf = buf_ref[0]
    x_new = dx_ref[0]
    y = x_new * w_ref[CONV_WIDTH - 1:CONV_WIDTH, :]
    for jw in range(CONV_WIDTH - 1):
        y = y + buf[jw:jw + 1, :] * w_ref[jw:jw + 1, :]
    y = _silu(y)
    buf_o_ref[0] = jnp.concatenate([buf[1:CONV_WIDTH - 1, :], x_new], axis=0)
    for h in range(N_HEADS):
        cs = slice(h * HEAD_DIM, (h + 1) * HEAD_DIM)
        s = shg_ref[0, h]
        s = _column(jnp.exp(hlf[:, cs]), eye) * s + _column(hk[:, cs], eye) * hv[:, cs]
        shg_o_ref[0, h] = s
        ohg_ref[0, :, cs] = jnp.sum(_column(hq[:, cs], eye) * s, axis=0, keepdims=True)
        q = y[:, cs]
        k = y[:, GROUP_WIDTH + h * HEAD_DIM:GROUP_WIDTH + (h + 1) * HEAD_DIM]
        v = y[:, 2 * GROUP_WIDTH + h * HEAD_DIM:2 * GROUP_WIDTH + (h + 1) * HEAD_DIM]
        q = q * lax.rsqrt(jnp.sum(q * q, axis=-1, keepdims=True) + NORM_EPS) * QK_SCALE
        k = k * lax.rsqrt(jnp.sum(k * k, axis=-1, keepdims=True) + NORM_EPS)
        beta = sm[:, LANE_BETA + h:LANE_BETA + h + 1]
        a = jnp.exp(sm[:, LANE_GDEC + h:LANE_GDEC + h + 1])
        s = sdn_ref[0, h]
        kc = _column(k, eye)
        v_new = beta * (v - a * jnp.sum(kc * s, axis=0, keepdims=True))
        s = a * s + kc * v_new
        sdn_o_ref[0, h] = s
        odn_ref[0, :, cs] = jnp.sum(_column(q, eye) * s, axis=0, keepdims=True)


def _rec_step(hq, hlf, hk, hi, dqkv, small, conv_w, s_hg, s_dn, buf):
    nb = hq.shape[0]
    row3 = lambda a: a.reshape(nb, 1, a.shape[-1])
    rspec = lambda w: pl.BlockSpec((1, 1, w), lambda b: (b, 0, 0))
    st_spec = pl.BlockSpec((1, N_HEADS, HEAD_DIM, HEAD_DIM), lambda b: (b, 0, 0, 0))
    buf_spec = pl.BlockSpec((1, CONV_WIDTH - 1, 3 * GROUP_WIDTH), lambda b: (b, 0, 0))
    outs = pl.pallas_call(
        _rec_step_kernel, grid=(nb,),
        in_specs=[rspec(GROUP_WIDTH)] * 4 + [rspec(3 * GROUP_WIDTH), rspec(SMALL_WIDTH),
                                             pl.BlockSpec(conv_w.shape, lambda b: (0, 0)), st_spec, st_spec, buf_spec],
        out_specs=[rspec(GROUP_WIDTH), rspec(GROUP_WIDTH), st_spec, st_spec, buf_spec],
        out_shape=[jax.ShapeDtypeStruct((nb, 1, GROUP_WIDTH), F32)] * 2
        + [jax.ShapeDtypeStruct(s_hg.shape, F32), jax.ShapeDtypeStruct(s_dn.shape, F32),
           jax.ShapeDtypeStruct(buf.shape, F32)],
        compiler_params=pltpu.CompilerParams(dimension_semantics=("arbitrary",), vmem_limit_bytes=VMEM_LIMIT),
        name="recurrent_step",
    )(row3(hq), row3(hlf), row3(hk), row3(hi), row3(dqkv), row3(small), conv_w, s_hg, s_dn, buf)
    ohg, odn, s_hg_new, s_dn_new, buf_new = outs
    return ohg.reshape(nb, GROUP_WIDTH), odn.reshape(nb, GROUP_WIDTH), s_hg_new, s_dn_new, buf_new


def _tile_gain(g):
    return jnp.tile(g.astype(F32), N_HEADS)


def _relayout_w_in(w_in_t_l):
    gw = GROUP_WIDTH
    a = 7 * gw
    e = a + N_HEADS + 7 * gw
    pad = jnp.zeros((SMALL_WIDTH - 3 * N_HEADS, w_in_t_l.shape[1]), w_in_t_l.dtype)
    rows = [w_in_t_l[:a], w_in_t_l[a + N_HEADS:e], w_in_t_l[a:a + N_HEADS], w_in_t_l[e:e + 2 * N_HEADS], pad]
    return jnp.concatenate(rows, axis=0).astype(BF16)


def _small_params(f_bias, dt_bias, a_log):
    sp = jnp.zeros((8, SMALL_WIDTH), F32)
    sp = sp.at[0, LANE_FLOG:LANE_FLOG + N_HEADS].set(f_bias.astype(F32))
    sp = sp.at[1, LANE_GDEC:LANE_GDEC + N_HEADS].set(dt_bias.astype(F32))
    sp = sp.at[2, LANE_GDEC:LANE_GDEC + N_HEADS].set(a_log.astype(F32))
    return sp


def _pick(n, candidates):
    for c in candidates:
        if n % c == 0:
            return c
    return n


def kernel(x_prompt, x_sample, cache_fox_k, cache_fox_v, cache_fox_logf, cache_sb_k, cache_sb_v, state_hgrn, state_dn, state_dn_conv, page_table, hgrn_lb_param, w_in, w_out, ln1_g, ln2_g, fox_f_bias, fox_q_norm, fox_k_norm, sb_q_norm, sb_k_norm, hgrn_out_norm, fox_out_norm, sb_out_norm, dn_out_norm, dn_conv_w, dn_dt_bias, dn_a_log, w_up, w_down):
    depth = w_in.shape[0]
    bsz, seq, d = x_prompt.shape
    nb = x_sample.shape[0]
    n_phys, page = cache_fox_k.shape[1], cache_fox_k.shape[2]
    m = bsz * seq
    tq = _pick(seq, (512, 256, 128))
    tm = _pick(tq, (256, 128))
    tm_mlp = _pick(m, (512, 256, 128, 64, 32, 16, 8))
    tf = _pick(w_up.shape[2], (1024, 512, 256, 128))
    tt = _pick(seq, (256, 128, 64))

    yp = x_prompt.reshape(m, d)
    ys = x_sample.reshape(nb, d)
    lbp = hgrn_lb_param.astype(F32)
    w_in_t = jnp.transpose(w_in, (2, 0, 1))
    kv_t = lambda a: a.transpose(0, 1, 3, 4, 2).reshape(depth, n_phys, GROUP_WIDTH, page)
    lf_t = jnp.swapaxes(cache_fox_logf.astype(F32), 2, 3)
    p_out = [[] for _ in range(8)]
    s_out = [[] for _ in range(8)]
    for l in range(depth):
        w_re = _relayout_w_in(w_in_t[:, l, :])
        w_o = w_out[l].astype(BF16)
        w_u = w_up[l].astype(BF16)
        w_d = w_down[l].astype(BF16)
        g1 = ln1_g[l].reshape(1, d).astype(F32)
        g2 = ln2_g[l].reshape(1, d).astype(F32)
        qk_gains = jnp.stack([_tile_gain(fox_q_norm[l]), _tile_gain(fox_k_norm[l]),
                              _tile_gain(sb_q_norm[l]), _tile_gain(sb_k_norm[l])])
        out_gains = jnp.stack([_tile_gain(hgrn_out_norm[l]), _tile_gain(fox_out_norm[l]),
                               _tile_gain(sb_out_norm[l]), _tile_gain(dn_out_norm[l])])
        sp = _small_params(fox_f_bias[l], dn_dt_bias[l], dn_a_log[l])
        conv_w = dn_conv_w[l].astype(F32)

        (hq, hlf, hk, hi, hgate, fq, fk, fv, sq, sk, sv, dqkv, dz, small, cum, small_t, cum_t,
         fk_blk, fv_blk, sk_blk, sv_blk) = _inproj(
            yp, g1, w_re, lbp, qk_gains, sp, layer=l, tm=tm, rows_per_seq=seq, with_time=True, key_block=tq)
        b3 = lambda a: a.reshape(bsz, seq, a.shape[-1])
        o_hg, st_hg = _hgrn_prompt(b3(hq), b3(hlf), b3(hk), b3(hi), tt=tt)
        o_fx = _fox_prompt(b3(fq), fk_blk, fv_blk, b3(cum), cum_t, tq=tq)
        o_sb = _sb_prompt(b3(sq), sk_blk, sv_blk, tq=tq)
        o_dn, st_dn = _gdn_prompt(b3(dqkv), conv_w, b3(small), small_t, tt=tt)
        yp = _outproj(yp, o_hg.reshape(m, -1), hgate, o_fx.reshape(m, -1), o_sb.reshape(m, -1),
                      o_dn.reshape(m, -1), dz, out_gains, w_o, tm=tm)
        yp = _mlp(yp, g2, w_u, w_d, tm=tm_mlp, tf=tf)
        by_head = lambda a: a.reshape(bsz, N_HEADS, HEAD_DIM, seq).transpose(0, 3, 1, 2)
        p_out[0].append(by_head(fk))
        p_out[1].append(by_head(fv))
        p_out[2].append(b3(small)[:, :, LANE_FLOG:LANE_FLOG + N_HEADS])
        p_out[3].append(by_head(sk))
        p_out[4].append(by_head(sv))
        p_out[5].append(jnp.swapaxes(st_hg, -1, -2))
        p_out[6].append(st_dn)
        p_out[7].append(b3(dqkv)[:, seq - (CONV_WIDTH - 1):, :])

        (hq, hlf, hk, hi, hgate, fq, fk, fv, sq, sk, sv, dqkv, dz, small) = _inproj(
            ys, g1, w_re, lbp, qk_gains, sp, layer=l, tm=nb, rows_per_seq=nb, with_time=False)
        o_fx = _dec_attn(page_table, fq, fk, fv, small, kv_t(cache_fox_k), kv_t(cache_fox_v), lf_t,
                         layer=l, fox=True)
        o_sb = _dec_attn(page_table, sq, sk, sv, small, kv_t(cache_sb_k), kv_t(cache_sb_v), None,
                         layer=l, fox=False)
        o_hg, o_dn, s_hg_new, s_dn_new, buf_new = _rec_step(
            hq, hlf, hk, hi, dqkv, small, conv_w, state_hgrn[l].astype(F32), state_dn[l].astype(F32),
            state_dn_conv[l].astype(F32))
        ys = _outproj(ys, o_hg, hgate, o_fx, o_sb, o_dn, dz, out_gains, w_o, tm=nb)
        ys = _mlp(ys, g2, w_u, w_d, tm=nb, tf=tf)
        sshape = (nb, 1, N_HEADS, HEAD_DIM)
        s_out[0].append(fk.reshape(sshape))
        s_out[1].append(fv.reshape(sshape))
        s_out[2].append(small[:, LANE_FLOG:LANE_FLOG + N_HEADS].reshape(nb, 1, N_HEADS))
        s_out[3].append(sk.reshape(sshape))
        s_out[4].append(sv.reshape(sshape))
        s_out[5].append(s_hg_new)
        s_out[6].append(s_dn_new)
        s_out[7].append(buf_new)

    p = [jnp.stack(v) for v in p_out]
    s = [jnp.stack(v) for v in s_out]
    return (yp.reshape(bsz, seq, d), ys.reshape(nb, 1, d), *p, *s)
```

```python
import functools

import jax
import jax.numpy as jnp
from jax import lax
from jax.experimental import pallas as pl
from jax.experimental.pallas import tpu as pltpu

F32 = jnp.float32
BF16 = jnp.bfloat16

HEAD_DIM = 64
N_HEADS = 4
GROUP_WIDTH = N_HEADS * HEAD_DIM
N_SEGMENTS = 14
SMALL_WIDTH = 128
CONV_WIDTH = 4
NORM_EPS = 1e-6
NEG_BIG = -1e30
QK_SCALE = HEAD_DIM ** -0.5
LOG2E = 1.4426950408889634
INV_LN2 = LOG2E
PRUNE_LOG2 = 160.0
BOUND_SLACK = 1.001
VMEM_LIMIT = 56 * 1024 * 1024

LANE_FLOG = 0
LANE_BETA = 4
LANE_GDEC = 8
SMALL_ROWS_T = 16


def _iota(shape, dim):
    return lax.broadcasted_iota(jnp.int32, shape, dim)


def _dot(a, b):
    return jnp.dot(a, b, preferred_element_type=F32)


def _dot_nt(a, b):
    return lax.dot_general(a, b, (((1,), (1,)), ((), ())), preferred_element_type=F32)


def _split3(x):
    hi = x.astype(BF16)
    r = x - hi.astype(F32)
    mid = r.astype(BF16)
    lo = (r - mid.astype(F32)).astype(BF16)
    return hi, mid, lo


def _dot_xc(x, c, parts=3):
    ps = _split3(x)[:parts]
    out = _dot(ps[0], c)
    for p in ps[1:]:
        out = out + _dot(p, c)
    return out


def _dot_cx(c, x, parts=3):
    ps = _split3(x)[:parts]
    out = _dot(c, ps[0])
    for p in ps[1:]:
        out = out + _dot(c, p)
    return out


def _dot_f32(a, b):
    ah = a.astype(BF16)
    al = (a - ah.astype(F32)).astype(BF16)
    bh = b.astype(BF16)
    bl = (b - bh.astype(F32)).astype(BF16)
    return _dot(ah, bh) + _dot(ah, bl) + _dot(al, bh)


def _head_ones(n=GROUP_WIDTH):
    return (_iota((n, n), 0) // HEAD_DIM == _iota((n, n), 1) // HEAD_DIM).astype(BF16)


def _head_sum(x, ones):
    return _dot_xc(x, ones)


def _head_rms(x, ones, gain):
    ms = _head_sum(x * x, ones) * (1.0 / HEAD_DIM)
    return x * lax.rsqrt(ms + NORM_EPS) * gain


def _log_sigmoid(x):
    return jnp.minimum(x, 0.0) - jnp.log1p(jnp.exp(-jnp.abs(x)))


def _softplus(x):
    return jnp.maximum(x, 0.0) + jnp.log1p(jnp.exp(-jnp.abs(x)))


def _sigmoid(x):
    return 1.0 / (1.0 + jnp.exp(-x))


def _silu(x):
    return x * _sigmoid(x)


def _max_all(x):
    return jnp.max(jnp.max(x, axis=0, keepdims=True), axis=1, keepdims=True)


def _lane_pack(vals):
    lane = _iota((1, 128), 1)
    out = jnp.zeros((1, 128), F32)
    for i, v in enumerate(vals):
        out = jnp.where(lane == i, v, out)
    return out


def _any_head(mask):
    lane = _iota(mask.shape, 1)
    return jnp.max(jnp.where(mask & (lane < N_HEADS), 1.0, 0.0)) > 0.0


def _inproj_kernel(x_ref, g1_ref, w_ref, lbp_ref, gains_ref, sp_ref, *refs, layer, depth, tiles_per_seq,
                   with_time):
    (hq_ref, hlf_ref, hk_ref, hi_ref, hg_ref, fq_ref, fk_ref, fv_ref, sq_ref, sk_ref, sv_ref,
     dqkv_ref, dz_ref, small_ref) = refs[:14]
    x = x_ref[...]
    h = (x * lax.rsqrt(jnp.mean(x * x, axis=-1, keepdims=True) + NORM_EPS) * g1_ref[...]).astype(BF16)

    def seg(j, width=GROUP_WIDTH):
        return _dot_nt(h, w_ref[j * GROUP_WIDTH:j * GROUP_WIDTH + width, :])

    ones = _head_ones()

    rows = [lbp_ref[i:i + 1, :] for i in range(depth)]
    mx = functools.reduce(jnp.maximum, rows)
    es = [jnp.exp(r - mx) for r in rows]
    lb = sum(es[1:layer + 1], jnp.zeros_like(mx)) / sum(es)
    hq_ref[...] = seg(0)
    hf = seg(1)
    a = jnp.log(lb)
    b = jnp.log1p(-lb) + _log_sigmoid(hf)
    hi = jnp.maximum(a, b)
    lo = jnp.minimum(a, b)
    hlf_ref[...] = hi + jnp.log1p(jnp.exp(lo - hi))
    hk_ref[...] = (1.0 - lb) * _sigmoid(-hf)
    hi_ref[...] = seg(2)
    hg_ref[...] = _silu(seg(3))

    fq_ref[...] = _head_rms(seg(4), ones, gains_ref[0:1, :]) * QK_SCALE
    sq_ref[...] = _head_rms(seg(7), ones, gains_ref[2:3, :]) * QK_SCALE
    kv = (_head_rms(seg(5), ones, gains_ref[1:2, :]), seg(6), _head_rms(seg(8), ones, gains_ref[3:4, :]), seg(9))
    for idx, (ref, val) in enumerate(zip((fk_ref, fv_ref, sk_ref, sv_ref), kv)):
        if with_time:
            val_t = val.T
            ref[0] = val_t
            refs[17 + idx][0, 0] = val_t.astype(BF16)
        else:
            ref[...] = val

    dqkv_ref[...] = seg(10, 3 * GROUP_WIDTH)
    dz_ref[...] = _silu(seg(13))

    s = _dot_nt(h, w_ref[N_SEGMENTS * GROUP_WIDTH:, :])
    lane = _iota(s.shape, 1)
    f_log = _log_sigmoid(s + sp_ref[0:1, :])
    beta = _sigmoid(s)
    g_dec = -jnp.exp(sp_ref[2:3, :]) * _softplus(s + sp_ref[1:2, :])
    small = jnp.where(lane < LANE_BETA, f_log, jnp.where(lane < LANE_GDEC, beta, g_dec))
    small_ref[...] = small

    if with_time:
        cum_ref, small_t_ref, cum_t_ref = refs[14:17]
        carry_ref = refs[21]
        tm = s.shape[0]

        @pl.when(pl.program_id(0) % tiles_per_seq == 0)
        def _():
            carry_ref[...] = jnp.zeros_like(carry_ref)

        tril = (_iota((tm, tm), 1) <= _iota((tm, tm), 0)).astype(BF16)
        cum = _dot_cx(tril, small) + carry_ref[0:1, :]
        cum_ref[...] = cum
        carry_ref[...] = jnp.broadcast_to(cum[tm - 1:tm, :], carry_ref.shape)
        small_t_ref[0] = small.T[:SMALL_ROWS_T, :]
        cum_t_ref[0] = cum.T[:SMALL_ROWS_T, :]


def _inproj(x2, g1, w_re, lbp, gains, sp, *, layer, tm, rows_per_seq, with_time, key_block=None):
    m, d = x2.shape
    depth = lbp.shape[0]
    grid = (m // tm,)
    row = lambda i: (i, 0)
    const = lambda i: (0, 0)
    tps = rows_per_seq // tm
    nseq = m // rows_per_seq
    seg_shape = jax.ShapeDtypeStruct((m, GROUP_WIDTH), F32)
    seg_spec = pl.BlockSpec((tm, GROUP_WIDTH), row)
    out_shape = [seg_shape] * 11 + [jax.ShapeDtypeStruct((m, 3 * GROUP_WIDTH), F32), seg_shape,
                                    jax.ShapeDtypeStruct((m, SMALL_WIDTH), F32)]
    out_specs = [seg_spec] * 11 + [pl.BlockSpec((tm, 3 * GROUP_WIDTH), row), seg_spec,
                                   pl.BlockSpec((tm, SMALL_WIDTH), row)]
    scratch = []
    if with_time:
        def by_time(rows):
            return (jax.ShapeDtypeStruct((nseq, rows, rows_per_seq), F32),
                    pl.BlockSpec((1, rows, tm), lambda i: (i // tps, 0, i % tps)))
        for idx in (6, 7, 9, 10):
            out_shape[idx], out_specs[idx] = by_time(GROUP_WIDTH)
        out_shape += [jax.ShapeDtypeStruct((m, SMALL_WIDTH), F32)]
        out_specs += [pl.BlockSpec((tm, SMALL_WIDTH), row)]
        for _ in range(2):
            sh, sp_ = by_time(SMALL_ROWS_T)
            out_shape.append(sh)
            out_specs.append(sp_)
        per_blk = key_block // tm
        for _ in range(4):
            out_shape.append(jax.ShapeDtypeStruct((nseq, rows_per_seq // key_block, GROUP_WIDTH, key_block), BF16))
            out_specs.append(pl.BlockSpec((1, 1, GROUP_WIDTH, tm),
                                          lambda i: (i // tps, (i % tps) // per_blk, 0, (i % tps) % per_blk)))
        scratch = [pltpu.VMEM((8, SMALL_WIDTH), F32)]
    kern = functools.partial(_inproj_kernel, layer=layer, depth=depth, tiles_per_seq=tps,
                             with_time=with_time)
    return pl.pallas_call(
        kern, grid=grid,
        in_specs=[pl.BlockSpec((tm, d), row), pl.BlockSpec((1, d), const), pl.BlockSpec(w_re.shape, const),
                  pl.BlockSpec(lbp.shape, const), pl.BlockSpec(gains.shape, const), pl.BlockSpec(sp.shape, const)],
        out_specs=out_specs, out_shape=out_shape, scratch_shapes=scratch,
        compiler_params=pltpu.CompilerParams(dimension_semantics=("arbitrary",), vmem_limit_bytes=VMEM_LIMIT),
        name="inproj_time" if with_time else "inproj_step",
    )(x2, g1, w_re, lbp, gains, sp)


HGRN_SUB = 16


def _hgrn_kernel(q_ref, lf_ref, k_ref, v_ref, o_ref, st_ref, s_ref, oi_ref, *, tt):
    t = pl.program_id(1)

    @pl.when(t == 0)
    def _():
        s_ref[...] = jnp.zeros_like(s_ref)

    q = q_ref[0]
    lf = lf_ref[0]
    kin = k_ref[0]
    v = v_ref[0]
    r = _iota((tt, tt), 0)
    c = _iota((tt, tt), 1)
    same = (r // HGRN_SUB) == (c // HGRN_SUB)
    g = _dot_cx(jnp.where(same & (c <= r), 1.0, 0.0).astype(BF16), lf)
    gl = _dot_cx(jnp.where(same, 1.0, 0.0).astype(BF16), lf)
    qg = q * jnp.exp(g)
    kg = kin * jnp.exp(gl - g)

    ones = _head_ones()
    rowmod = _iota((tt, GROUP_WIDTH), 0) % HGRN_SUB
    o = jnp.zeros((tt, GROUP_WIDTH), F32)
    for d in range(HGRN_SUB):
        if d == 0:
            kd, gd, vd = kin, g, v
        else:
            kd = pltpu.roll(kin, d, 0)
            gd = pltpu.roll(g, d, 0)
            vd = pltpu.roll(v, d, 0)
        e = jnp.where(rowmod >= d, g - gd, NEG_BIG)
        p = q * kd * jnp.exp(e)
        o = o + _dot_xc(p, ones, parts=2) * vd

    v_t = v.T
    for i in range(tt // HGRN_SUB):
        rs = slice(i * HGRN_SUB, (i + 1) * HGRN_SUB)
        for h in range(N_HEADS):
            cs = slice(h * HEAD_DIM, (h + 1) * HEAD_DIM)
            s = s_ref[h]
            oi_ref[rs, cs] = _dot_nt(qg[rs, cs].astype(BF16), s.astype(BF16))
            dec = jnp.exp(gl[i * HGRN_SUB:i * HGRN_SUB + 1, cs])
            s_ref[h] = dec * s + _dot(v_t[cs, rs].astype(BF16), kg[rs, cs].astype(BF16))
    o_ref[0] = o + oi_ref[...]

    @pl.when(t == pl.num_programs(1) - 1)
    def _():
        st_ref[0] = s_ref[...]


def _hgrn_prompt(hq, hlf, hk, hi, *, tt):
    b, t, _ = hq.shape
    blk = pl.BlockSpec((1, tt, GROUP_WIDTH), lambda i, j: (i, j, 0))
    return pl.pallas_call(
        functools.partial(_hgrn_kernel, tt=tt), grid=(b, t // tt),
        in_specs=[blk] * 4,
        out_specs=[blk, pl.BlockSpec((1, N_HEADS, HEAD_DIM, HEAD_DIM), lambda i, j: (i, 0, 0, 0))],
        out_shape=[jax.ShapeDtypeStruct((b, t, GROUP_WIDTH), F32),
                   jax.ShapeDtypeStruct((b, N_HEADS, HEAD_DIM, HEAD_DIM), F32)],
        scratch_shapes=[pltpu.VMEM((N_HEADS, HEAD_DIM, HEAD_DIM), F32), pltpu.VMEM((tt, GROUP_WIDTH), F32)],
        compiler_params=pltpu.CompilerParams(dimension_semantics=("arbitrary", "arbitrary"),
                                             vmem_limit_bytes=VMEM_LIMIT),
        name="hgrn_prompt",
    )(hq, hlf, hk, hi)


def _fox_kernel(q_ref, k_ref, v_ref, fq_ref, fk_ref, o_ref, qa_ref, m_ref, acc_ref, qn_ref, fq_max_ref, m_min_ref,
                kn_ref, fk_min_ref, *, tq):
    qi = pl.program_id(1)
    j = pl.program_id(2)

    @pl.when(j == 0)
    def _():
        m_ref[...] = jnp.full_like(m_ref, NEG_BIG)
        acc_ref[...] = jnp.zeros_like(acc_ref)
        q = q_ref[0] * LOG2E
        f = fq_ref[0] * LOG2E
        lane = _iota((tq, HEAD_DIM), 1)
        for h in range(N_HEADS):
            hi, mid, lo = [p.astype(F32) for p in _split3(f[:, LANE_FLOG + h:LANE_FLOG + h + 1])]
            ext = jnp.where(lane == 0, hi, jnp.where(lane == 1, mid, jnp.where(lane == 2, lo,
                                                                               jnp.where(lane < 6, 1.0, 0.0))))
            qa_ref[h] = jnp.concatenate([q[:, h * HEAD_DIM:(h + 1) * HEAD_DIM], ext], axis=1).astype(BF16)

        heads = range(N_HEADS)
        rows = lambda h: slice(h * HEAD_DIM, (h + 1) * HEAD_DIM)
        qb = q.astype(BF16).astype(F32)
        qn2 = _head_sum(qb * qb, _head_ones())
        kf = k_ref[0, 0].astype(F32)
        k2 = kf * kf
        fk = fk_ref[0] * LOG2E
        qn_ref[...] = _lane_pack([_max_all(qn2[:, rows(h)]) for h in heads])
        fq_max_ref[...] = _lane_pack([_max_all(f[:, LANE_FLOG + h:LANE_FLOG + h + 1]) for h in heads])
        kn_ref[pl.ds(qi, 1), :] = _lane_pack([_max_all(jnp.sum(k2[rows(h), :], axis=0, keepdims=True))
                                              for h in heads])
        fk_min_ref[pl.ds(qi, 1), :] = _lane_pack([-_max_all(-fk[LANE_FLOG + h:LANE_FLOG + h + 1, :])
                                                  for h in heads])

    def block_matters():
        zcap2 = qn_ref[...] * kn_ref[pl.ds(qi - j, 1), :] * (BOUND_SLACK * BOUND_SLACK)
        bias_cap = fq_max_ref[...] - fk_min_ref[pl.ds(qi - j, 1), :]
        room = m_min_ref[...] - bias_cap - PRUNE_LOG2
        return _any_head((room < 0.0) | (zcap2 > room * room))

    def step(masked):
        kf = k_ref[0, 0]
        vf = v_ref[0, 0]
        fk = fk_ref[0] * LOG2E
        row = _iota((8, tq), 0)
        pad = jnp.zeros((HEAD_DIM - 8, tq), F32)
        v_ext = jnp.concatenate([jnp.where(row == 0, 1.0, 0.0), pad], axis=0).astype(BF16)
        if masked:
            keep = _iota((tq, tq), 1) <= _iota((tq, tq), 0)
        heads = range(N_HEADS)
        ss = []
        for h in heads:
            cs = slice(h * HEAD_DIM, (h + 1) * HEAD_DIM)
            hi, mid, lo = [p.astype(F32) for p in _split3(fk[LANE_FLOG + h:LANE_FLOG + h + 1, :])]
            k_ext = jnp.where(row < 3, 1.0, jnp.where(row == 3, -hi, jnp.where(row == 4, -mid,
                                                                               jnp.where(row == 5, -lo, 0.0))))
            ka = jnp.concatenate([kf[cs, :], jnp.concatenate([k_ext, pad], axis=0).astype(BF16)], axis=0)
            ss.append(_dot(qa_ref[h], ka))
        if masked:
            ss = [jnp.where(keep, s, NEG_BIG) for s in ss]
        m_prevs = [m_ref[h] for h in heads]
        m_news = [jnp.maximum(m_prevs[h], jnp.max(ss[h], axis=-1, keepdims=True)) for h in heads]
        ps = [jnp.exp2(ss[h] - jnp.concatenate([m_news[h]] * (tq // 128), axis=1)) for h in heads]
        for h in heads:
            va = jnp.concatenate([vf[h * HEAD_DIM:(h + 1) * HEAD_DIM, :], v_ext], axis=0)
            m_ref[h] = m_news[h]
            acc_ref[h] = jnp.exp2(m_prevs[h] - m_news[h]) * acc_ref[h] + _dot_nt(ps[h].astype(BF16), va)
        m_min_ref[...] = _lane_pack([-_max_all(-m_news[h]) for h in heads])

    @pl.when(j == 0)
    def _():
        step(True)

    @pl.when((j > 0) & (j <= qi))
    def _():
        @pl.when(block_matters())
        def _():
            step(False)

    @pl.when(j == qi)
    def _():
        for h in range(N_HEADS):
            acc = acc_ref[h]
            o_ref[0, :, h * HEAD_DIM:(h + 1) * HEAD_DIM] = acc[:, :HEAD_DIM] / acc[:, HEAD_DIM:HEAD_DIM + 1]


def _fox_prompt(fq, fk, fv, cum, cum_t, *, tq):
    b, t, _ = fq.shape
    n = t // tq
    qspec = pl.BlockSpec((1, tq, GROUP_WIDTH), lambda i, qi, j: (i, qi, 0))
    kspec = pl.BlockSpec((1, 1, GROUP_WIDTH, tq), lambda i, qi, j: (i, jnp.maximum(qi - j, 0), 0, 0))
    return pl.pallas_call(
        functools.partial(_fox_kernel, tq=tq), grid=(b, n, n),
        in_specs=[qspec, kspec, kspec,
                  pl.BlockSpec((1, tq, SMALL_WIDTH), lambda i, qi, j: (i, qi, 0)),
                  pl.BlockSpec((1, SMALL_ROWS_T, tq), lambda i, qi, j: (i, 0, jnp.maximum(qi - j, 0)))],
        out_specs=qspec,
        out_shape=jax.ShapeDtypeStruct((b, t, GROUP_WIDTH), F32),
        scratch_shapes=[pltpu.VMEM((N_HEADS, tq, 2 * HEAD_DIM), BF16),
                        pltpu.VMEM((N_HEADS, tq, 128), F32), pltpu.VMEM((N_HEADS, tq, 2 * HEAD_DIM), F32),
                        pltpu.VMEM((1, 128), F32), pltpu.VMEM((1, 128), F32), pltpu.VMEM((1, 128), F32),
                        pltpu.VMEM((n, 128), F32), pltpu.VMEM((n, 128), F32)],
        compiler_params=pltpu.CompilerParams(dimension_semantics=("arbitrary", "arbitrary", "arbitrary"),
                                             vmem_limit_bytes=VMEM_LIMIT),
        name="fox_prompt",
    )(fq, fk, fv, cum, cum_t)


SB_SUB = 256


def _sb_kernel(q_ref, k_ref, v_ref, o_ref, *refs, tq, windowed):
    if windowed:
        more_ref, *refs = refs
    qb_ref, carry_ref, acc_ref, qn_ref, kn_ref, cmin_ref = refs
    qi = pl.program_id(1)
    j = pl.program_id(2)
    last_j = jnp.minimum(qi, pl.num_programs(2) - 1)

    heads = range(N_HEADS)
    rows = lambda h: slice(h * HEAD_DIM, (h + 1) * HEAD_DIM)

    @pl.when((j == 0) & (qi == 0))
    def _():
        kn_ref[...] = jnp.zeros_like(kn_ref)

    @pl.when(j == 0)
    def _():
        carry_ref[...] = jnp.zeros_like(carry_ref)
        acc_ref[...] = jnp.zeros_like(acc_ref)
        qb = (q_ref[0] * LOG2E).astype(BF16)
        qb_ref[...] = qb
        qn2 = _head_sum(qb.astype(F32) * qb.astype(F32), _head_ones())
        kf = k_ref[0, 0].astype(F32)
        k2 = kf * kf
        qn_ref[...] = _lane_pack([_max_all(qn2[:, rows(h)]) for h in heads])
        kn_ref[pl.ds(qi, 1), :] = _lane_pack([_max_all(jnp.sum(k2[rows(h), :], axis=0, keepdims=True))
                                              for h in heads])
        cmin_ref[...] = jnp.zeros_like(cmin_ref)

    def may_matter(kn_rows):
        zcap2 = qn_ref[...] * kn_rows * (BOUND_SLACK * BOUND_SLACK)
        room = cmin_ref[...] - PRUNE_LOG2
        return (room < 0.0) | (zcap2 > room * room)

    def block_matters():
        return _any_head(may_matter(kn_ref[pl.ds(qi - j, 1), :]))

    def step(masked):
        k = k_ref[0, 0]
        v = v_ref[0, 0]
        r = _iota((tq, tq), 0)
        c = _iota((tq, tq), 1)
        sub = min(SB_SUB, tq)
        suffix = jnp.where(_iota((sub, sub), 1) <= _iota((sub, sub), 0), 1.0, 0.0).astype(BF16)
        z2s = [_dot(qb_ref[:, rows(h)], k[rows(h), :]) for h in heads]
        sps = [jnp.maximum(z2, 0.0) + jnp.log(1.0 + jnp.exp2(-jnp.abs(z2))) * INV_LN2 for z2 in z2s]
        if masked:
            sps = [jnp.where(c < r, sp, 0.0) for sp in sps]
        carries = [carry_ref[h] for h in heads]
        cum_parts = [[None] * (tq // sub) for _ in heads]
        for part in reversed(range(tq // sub)):
            ks = slice(part * sub, (part + 1) * sub)
            for h in heads:
                wide = jnp.concatenate([carries[h]] * (sub // 128), axis=1)
                cum = _dot_xc(sps[h][:, ks], suffix, parts=2) + wide
                cum_parts[h][part] = cum
                carries[h] = jnp.broadcast_to(cum[:, 0:1], carries[h].shape)
        es = [z2s[h] - jnp.concatenate(cum_parts[h], axis=1) for h in heads]
        if masked:
            es = [jnp.where(c < r, e, NEG_BIG) for e in es]
        pad = jnp.zeros((HEAD_DIM, tq), BF16)
        for h in heads:
            va = jnp.concatenate([v[rows(h), :], pad], axis=0)
            acc_ref[h] = acc_ref[h] + _dot_nt(jnp.exp2(es[h]).astype(BF16), va)
            carry_ref[h] = carries[h]
        cmin_ref[...] = _lane_pack([-_max_all(-carries[h]) for h in heads])

    @pl.when(j == 0)
    def _():
        step(True)

    @pl.when((j > 0) & (j <= qi))
    def _():
        @pl.when(block_matters())
        def _():
            step(False)

    @pl.when(j == last_j)
    def _():
        for h in range(N_HEADS):
            o_ref[0, :, h * HEAD_DIM:(h + 1) * HEAD_DIM] = acc_ref[h][:, :HEAD_DIM]
        if windowed:
            older = _iota(kn_ref.shape, 0) <= qi - pl.num_programs(2)
            more = _any_head(may_matter(kn_ref[...]) & older)
            more_ref[0, 0] = jnp.where(more, jnp.ones((8, 128), F32), jnp.zeros((8, 128), F32))


SB_WINDOW = 4


def _sb_prompt(sq, sk, sv, *, tq):
    b, t, _ = sq.shape
    n = t // tq
    qspec = pl.BlockSpec((1, tq, GROUP_WIDTH), lambda i, qi, j: (i, qi, 0))
    kspec = pl.BlockSpec((1, 1, GROUP_WIDTH, tq), lambda i, qi, j: (i, jnp.maximum(qi - j, 0), 0, 0))

    def call(n_keys):
        windowed = n_keys < n
        out_specs = [qspec]
        out_shape = [jax.ShapeDtypeStruct((b, t, GROUP_WIDTH), F32)]
        if windowed:
            out_specs.append(pl.BlockSpec((1, 1, 8, 128), lambda i, qi, j: (i, qi, 0, 0)))
            out_shape.append(jax.ShapeDtypeStruct((b, n, 8, 128), F32))
        return pl.pallas_call(
            functools.partial(_sb_kernel, tq=tq, windowed=windowed), grid=(b, n, n_keys),
            in_specs=[qspec, kspec, kspec],
            out_specs=out_specs, out_shape=out_shape,
            scratch_shapes=[pltpu.VMEM((tq, GROUP_WIDTH), BF16),
                            pltpu.VMEM((N_HEADS, tq, 128), F32), pltpu.VMEM((N_HEADS, tq, 2 * HEAD_DIM), F32),
                            pltpu.VMEM((1, 128), F32), pltpu.VMEM((n, 128), F32), pltpu.VMEM((1, 128), F32)],
            compiler_params=pltpu.CompilerParams(dimension_semantics=("arbitrary", "arbitrary", "arbitrary"),
                                                 vmem_limit_bytes=VMEM_LIMIT),
            name="sb_prompt_window" if windowed else "sb_prompt",
        )(sq, sk, sv)

    if n <= SB_WINDOW:
        return call(n)[0]
    o_near, more = call(SB_WINDOW)
    return lax.cond(jnp.max(more) > 0.0, lambda: call(n)[0], lambda: o_near)


GDN_CHUNK = 128
GDN_BASE = 16


def _unit_lower_inverses(lmats, ii, jj):
    ns = [jnp.where(ii // GDN_BASE == jj // GDN_BASE, -lm, 0.0) for lm in lmats]
    eye = jnp.where(ii == jj, 1.0, 0.0)
    ts = [eye + n for n in ns]
    ps = ns
    for _ in range(GDN_BASE.bit_length() - 2):
        ps = [_dot_f32(p, p) for p in ps]
        ts = [t + _dot_f32(t, p) for t, p in zip(ts, ps)]
    b = GDN_BASE
    while b < GDN_CHUNK:
        lower_left = (ii // (2 * b) == jj // (2 * b)) & (ii // b != jj // b)
        mids = [_dot_f32(t, jnp.where(lower_left, lm, 0.0)) for t, lm in zip(ts, lmats)]
        ts = [t - _dot_f32(mid, t) for t, mid in zip(ts, mids)]
        b *= 2
    return ts


def _gdn_prep_kernel(x_ref, w_ref, sm_ref, smt_ref, u_ref, wk_ref, qe_ref, attn_ref, kdt_ref, gct_ref, ext_ref, *, tt):
    t = pl.program_id(1)

    @pl.when(t == 0)
    def _():
        ext_ref[0:8, :] = jnp.zeros((8, 3 * GROUP_WIDTH), F32)

    @pl.when(t > 0)
    def _():
        ext_ref[0:8, :] = ext_ref[tt:tt + 8, :]

    ext_ref[8:8 + tt, :] = x_ref[0]
    y = ext_ref[pl.ds(8 - (CONV_WIDTH - 1), tt), :] * w_ref[0:1, :]
    for jw in range(1, CONV_WIDTH):
        y = y + ext_ref[pl.ds(8 - (CONV_WIDTH - 1) + jw, tt), :] * w_ref[jw:jw + 1, :]
    y = _silu(y)
    ones = _head_ones()
    q = y[:, :GROUP_WIDTH]
    k = y[:, GROUP_WIDTH:2 * GROUP_WIDTH]
    v = y[:, 2 * GROUP_WIDTH:]
    q = q * lax.rsqrt(_head_sum(q * q, ones) + NORM_EPS) * QK_SCALE
    k = k * lax.rsqrt(_head_sum(k * k, ones) + NORM_EPS)
    sm = sm_ref[0]
    smt = smt_ref[0]
    r = _iota((tt, tt), 0)
    c = _iota((tt, tt), 1)
    same = (r // GDN_CHUNK) == (c // GDN_CHUNK)
    gc_col = _dot_cx(jnp.where(same & (c <= r), 1.0, 0.0).astype(BF16), sm)
    gc_row = _dot_xc(smt, jnp.where(same & (r <= c), 1.0, 0.0).astype(BF16))
    gct_ref[0] = gc_row
    k_t = k.T
    ii = _iota((GDN_CHUNK, GDN_CHUNK), 0)
    jj = _iota((GDN_CHUNK, GDN_CHUNK), 1)
    n_chunks = tt // GDN_CHUNK
    lmats, rhss = [], []
    for ci in range(n_chunks):
        rs = slice(ci * GDN_CHUNK, (ci + 1) * GDN_CHUNK)
        qes, attns, kdts = [], [], []
        for h in range(N_HEADS):
            cs = slice(h * HEAD_DIM, (h + 1) * HEAD_DIM)
            gcol = gc_col[rs, LANE_GDEC + h:LANE_GDEC + h + 1]
            grow = gc_row[LANE_GDEC + h:LANE_GDEC + h + 1, rs]
            beta = sm[rs, LANE_BETA + h:LANE_BETA + h + 1]
            dec = jnp.exp(jnp.where(ii >= jj, gcol - grow, NEG_BIG))
            qh = q[rs, cs].astype(BF16)
            kth = k_t[cs, rs]
            kb = k[rs, cs] * beta
            eg = jnp.exp(gcol)
            lmats.append(jnp.where(ii > jj, _dot(kb.astype(BF16), kth.astype(BF16)) * dec, 0.0))
            rhss.append(jnp.concatenate([v[rs, cs] * beta, kb * eg], axis=1))
            qes.append(q[rs, cs] * eg)
            attns.append(_dot(qh, kth.astype(BF16)) * dec)
            kdts.append(kth * jnp.exp(grow[:, GDN_CHUNK - 1:GDN_CHUNK] - grow))
        qe_ref[0, rs, :] = jnp.concatenate(qes, axis=1).astype(BF16)
        attn_ref[0, rs, :] = jnp.concatenate(attns, axis=1).astype(BF16)
        kdt_ref[0, :, rs] = jnp.concatenate(kdts, axis=0).astype(BF16)
    tinvs = _unit_lower_inverses(lmats, ii, jj)
    sols = [_dot_f32(ti, rhs) for ti, rhs in zip(tinvs, rhss)]
    for ci in range(n_chunks):
        rs = slice(ci * GDN_CHUNK, (ci + 1) * GDN_CHUNK)
        chunk = sols[ci * N_HEADS:(ci + 1) * N_HEADS]
        u_ref[0, rs, :] = jnp.concatenate([sol[:, :HEAD_DIM] for sol in chunk], axis=1)
        wk_ref[0, rs, :] = jnp.concatenate([sol[:, HEAD_DIM:] for sol in chunk], axis=1).astype(BF16)


def _gdn_scan_kernel(u_ref, wk_ref, qe_ref, attn_ref, kdt_ref, gct_ref, o_ref, st_ref, s_ref, *, tt, nb):
    t = pl.program_id(0)

    @pl.when(t == 0)
    def _():
        s_ref[...] = jnp.zeros_like(s_ref)

    for ci in range(tt // GDN_CHUNK):
        rs = slice(ci * GDN_CHUNK, (ci + 1) * GDN_CHUNK)
        items = [(b, h) for b in range(nb) for h in range(N_HEADS)]
        cs = lambda h: slice(h * HEAD_DIM, (h + 1) * HEAD_DIM)
        us = [u_ref[b, rs, :] for b in range(nb)]
        wks = [wk_ref[b, rs, :] for b in range(nb)]
        qes = [qe_ref[b, rs, :] for b in range(nb)]
        attns = [attn_ref[b, rs, :] for b in range(nb)]
        kdts = [kdt_ref[b, :, rs] for b in range(nb)]
        decays = [jnp.exp(gct_ref[b, :, rs][:, GDN_CHUNK - 1:GDN_CHUNK]) for b in range(nb)]
        ss = [s_ref[b, h] for b, h in items]
        sbs = [s.astype(BF16) for s in ss]
        v_news = [us[b][:, cs(h)] - _dot(wks[b][:, cs(h)], sb) for (b, h), sb in zip(items, sbs)]
        vbs = [vn.astype(BF16) for vn in v_news]
        for (b, h), s, vb in zip(items, ss, vbs):
            a = decays[b][LANE_GDEC + h:LANE_GDEC + h + 1, :]
            s_ref[b, h] = a * s + _dot(kdts[b][cs(h), :], vb)
        for b in range(nb):
            outs = [_dot(qes[b][:, cs(h)], sbs[b * N_HEADS + h])
                    + _dot(attns[b][:, h * GDN_CHUNK:(h + 1) * GDN_CHUNK], vbs[b * N_HEADS + h])
                    for h in range(N_HEADS)]
            o_ref[b, rs, :] = jnp.concatenate(outs, axis=1)

    @pl.when(t == pl.num_programs(0) - 1)
    def _():
        st_ref[...] = s_ref[...]


def _gdn_prompt(dqkv, conv_w, small, small_t, *, tt):
    b, t, _ = dqkv.shape
    n_attn = N_HEADS * GDN_CHUNK
    by_rows = lambda w: pl.BlockSpec((1, tt, w), lambda i, j: (i, j, 0))
    by_cols = lambda r: pl.BlockSpec((1, r, tt), lambda i, j: (i, 0, j))
    u, wk, qe, attn, kdt, gct = pl.pallas_call(
        functools.partial(_gdn_prep_kernel, tt=tt), grid=(b, t // tt),
        in_specs=[by_rows(3 * GROUP_WIDTH), pl.BlockSpec(conv_w.shape, lambda i, j: (0, 0)),
                  by_rows(SMALL_WIDTH), by_cols(SMALL_ROWS_T)],
        out_specs=[by_rows(GROUP_WIDTH), by_rows(GROUP_WIDTH), by_rows(GROUP_WIDTH), by_rows(n_attn),
                   by_cols(GROUP_WIDTH), by_cols(SMALL_ROWS_T)],
        out_shape=[jax.ShapeDtypeStruct((b, t, GROUP_WIDTH), F32), jax.ShapeDtypeStruct((b, t, GROUP_WIDTH), BF16),
                   jax.ShapeDtypeStruct((b, t, GROUP_WIDTH), BF16), jax.ShapeDtypeStruct((b, t, n_attn), BF16),
                   jax.ShapeDtypeStruct((b, GROUP_WIDTH, t), BF16), jax.ShapeDtypeStruct((b, SMALL_ROWS_T, t), F32)],
        scratch_shapes=[pltpu.VMEM((tt + 8, 3 * GROUP_WIDTH), F32)],
        compiler_params=pltpu.CompilerParams(dimension_semantics=("arbitrary", "arbitrary"),
                                             vmem_limit_bytes=VMEM_LIMIT),
        name="gdn_prep",
    )(dqkv, conv_w, small, small_t)
    all_rows = lambda w: pl.BlockSpec((b, tt, w), lambda j: (0, j, 0))
    all_cols = lambda r: pl.BlockSpec((b, r, tt), lambda j: (0, 0, j))
    st_spec = pl.BlockSpec((b, N_HEADS, HEAD_DIM, HEAD_DIM), lambda j: (0, 0, 0, 0))
    return pl.pallas_call(
        functools.partial(_gdn_scan_kernel, tt=tt, nb=b), grid=(t // tt,),
        in_specs=[all_rows(GROUP_WIDTH), all_rows(GROUP_WIDTH), all_rows(GROUP_WIDTH), all_rows(n_attn),
                  all_cols(GROUP_WIDTH), all_cols(SMALL_ROWS_T)],
        out_specs=[all_rows(GROUP_WIDTH), st_spec],
        out_shape=[jax.ShapeDtypeStruct((b, t, GROUP_WIDTH), F32),
                   jax.ShapeDtypeStruct((b, N_HEADS, HEAD_DIM, HEAD_DIM), F32)],
        scratch_shapes=[pltpu.VMEM((b, N_HEADS, HEAD_DIM, HEAD_DIM), F32)],
        compiler_params=pltpu.CompilerParams(dimension_semantics=("arbitrary",), vmem_limit_bytes=VMEM_LIMIT),
        name="gdn_scan",
    )(u, wk, qe, attn, kdt, gct)


def _outproj_kernel(x_ref, ohg_ref, hg_ref, ofx_ref, osb_ref, odn_ref, dz_ref, gains_ref, w_ref, y_ref):
    ones = _head_ones()
    parts = [
        _head_rms(ohg_ref[...], ones, gains_ref[0:1, :]) * hg_ref[...],
        _head_rms(ofx_ref[...], ones, gains_ref[1:2, :]),
        _head_rms(osb_ref[...], ones, gains_ref[2:3, :]),
        _head_rms(odn_ref[...], ones, gains_ref[3:4, :]) * dz_ref[...],
    ]
    y = x_ref[...]
    for gidx, p in enumerate(parts):
        y = y + _dot(p.astype(BF16), w_ref[gidx * GROUP_WIDTH:(gidx + 1) * GROUP_WIDTH, :])
    y_ref[...] = y


def _outproj(x2, ohg, hgate, ofx, osb, odn, dz, gains, w_out, *, tm):
    m, d = x2.shape
    row = lambda i: (i, 0)
    const = lambda i: (0, 0)
    seg = pl.BlockSpec((tm, GROUP_WIDTH), row)
    return pl.pallas_call(
        _outproj_kernel, grid=(m // tm,),
        in_specs=[pl.BlockSpec((tm, d), row)] + [seg] * 6 + [pl.BlockSpec(gains.shape, const),
                                                            pl.BlockSpec(w_out.shape, const)],
        out_specs=pl.BlockSpec((tm, d), row),
        out_shape=jax.ShapeDtypeStruct((m, d), F32),
        compiler_params=pltpu.CompilerParams(dimension_semantics=("arbitrary",), vmem_limit_bytes=VMEM_LIMIT),
        name="outproj",
    )(x2, ohg, hgate, ofx, osb, odn, dz, gains, w_out)


def _mlp_kernel(x_ref, g2_ref, wu_ref, wd_ref, y_ref, h_ref, acc_ref):
    f = pl.program_id(1)

    @pl.when(f == 0)
    def _():
        x = x_ref[...]
        h_ref[...] = (x * lax.rsqrt(jnp.mean(x * x, axis=-1, keepdims=True) + NORM_EPS) * g2_ref[...]).astype(BF16)
        acc_ref[...] = x

    u = jnp.maximum(_dot(h_ref[...], wu_ref[...]), 0.0)
    acc_ref[...] += _dot((u * u).astype(BF16), wd_ref[...])

    @pl.when(f == pl.num_programs(1) - 1)
    def _():
        y_ref[...] = acc_ref[...]


def _mlp(x2, g2, w_up, w_down, *, tm, tf):
    m, d = x2.shape
    dff = w_up.shape[1]
    return pl.pallas_call(
        _mlp_kernel, grid=(m // tm, dff // tf),
        in_specs=[pl.BlockSpec((tm, d), lambda i, f: (i, 0)), pl.BlockSpec((1, d), lambda i, f: (0, 0)),
                  pl.BlockSpec((d, tf), lambda i, f: (0, f)), pl.BlockSpec((tf, d), lambda i, f: (f, 0))],
        out_specs=pl.BlockSpec((tm, d), lambda i, f: (i, 0)),
        out_shape=jax.ShapeDtypeStruct((m, d), F32),
        scratch_shapes=[pltpu.VMEM((tm, d), BF16), pltpu.VMEM((tm, d), F32)],
        compiler_params=pltpu.CompilerParams(dimension_semantics=("arbitrary", "arbitrary"),
                                             vmem_limit_bytes=VMEM_LIMIT),
        name="mlp",
    )(x2, g2, w_up, w_down)


PAGES_PER_STEP = 16
SEQS_PER_STEP = 4


def _head_rows(row):
    x = jnp.broadcast_to(row, (8, GROUP_WIDTH))
    return jnp.where(_iota((8, GROUP_WIDTH), 1) // HEAD_DIM == _iota((8, GROUP_WIDTH), 0), x, 0.0)


def _dec_attn_kernel(pt_ref, q_ref, kn_ref, vn_ref, sn_ref, *refs, fox, ns, pp, page):
    n_refs = ns * pp
    k_refs = refs[:n_refs]
    v_refs = refs[n_refs:2 * n_refs]
    rest = refs[2 * n_refs:]
    if fox:
        lf_refs = rest[:n_refs]
        rest = rest[n_refs:]
    o_ref, m_ref, l_ref, acc_ref, carry_ref = rest
    j = pl.program_id(1)
    seqs = range(ns)
    qbs = [_head_rows(q_ref[s]).astype(BF16) for s in seqs]
    r = _iota((page, page), 0)
    c = _iota((page, page), 1)
    later = jnp.where(r > c, 1.0, 0.0).astype(BF16)

    @pl.when(j == 0)
    def _():
        if fox:
            l_ref[...] = jnp.ones_like(l_ref)
            lane = _iota((8, SMALL_WIDTH), 1)
            row = _iota((8, SMALL_WIDTH), 0)
            for s in seqs:
                kn = jnp.broadcast_to(kn_ref[s], (8, GROUP_WIDTH)).astype(BF16)
                m_ref[s] = jnp.broadcast_to(_dot_nt(qbs[s], kn)[:, 0:1], m_ref.shape[1:])
                acc_ref[s] = jnp.broadcast_to(vn_ref[s], acc_ref.shape[1:]).astype(BF16).astype(F32)
                sn = jnp.broadcast_to(sn_ref[s], (8, SMALL_WIDTH))
                lf_new = jnp.sum(jnp.where(lane == row + LANE_FLOG, sn, 0.0), axis=-1, keepdims=True)
                carry_ref[s] = jnp.broadcast_to(jnp.where(_iota((8, 1), 0) < N_HEADS, lf_new, 0.0),
                                                carry_ref.shape[1:])
        else:
            acc_ref[...] = jnp.zeros_like(acc_ref)
            carry_ref[...] = jnp.zeros_like(carry_ref)

    g8 = pp * 8
    n8 = ns * g8
    per_page = lambda vals: jnp.concatenate([v for s in seqs for v in [vals[s]] * pp], axis=0)
    z = jnp.concatenate([_dot(qbs[s], k_refs[s * pp + i][0].astype(BF16)) for s in seqs for i in range(pp)], axis=0)
    if fox:
        pad = jnp.zeros((8 - N_HEADS, page), F32)
        x = jnp.concatenate([a for i in range(n_refs) for a in (lf_refs[i][0], pad)], axis=0)
    else:
        x = _log_sigmoid(-z)
    ri = _iota((n8, n8), 0)
    ci = _iota((n8, n8), 1)
    before = jnp.where((ri % 8 == ci % 8) & (ri // g8 == ci // g8) & (ci // 8 < ri // 8), 1.0, 0.0).astype(BF16)
    tot = _dot_xc(x, jnp.ones((page, page), BF16))
    upto = _dot_cx(before, tot) + per_page([carry_ref[s] for s in seqs])
    bias = _dot_xc(x, later) + upto
    new_carry = upto + tot
    for s in seqs:
        carry_ref[s] = new_carry[(s + 1) * g8 - 8:(s + 1) * g8, :]
    if fox:
        sc = z + bias
        m_prev = m_ref[...]
        m_new = jnp.maximum(m_prev, jnp.max(jnp.max(sc.reshape(ns, pp, 8, page), axis=1), axis=-1, keepdims=True))
        alpha = jnp.exp(m_prev - m_new)
        p = jnp.exp(sc - per_page([m_new[s] for s in seqs]))
        l_ref[...] = alpha * l_ref[...] + jnp.sum(jnp.sum(p.reshape(ns, pp, 8, page), axis=1), axis=-1, keepdims=True)
        m_ref[...] = m_new
    else:
        p = jnp.exp(z + x + bias)
    for s in seqs:
        pv = None
        for i in range(pp):
            row0 = (s * pp + i) * 8
            term = _dot_nt(p[row0:row0 + 8].astype(BF16), v_refs[s * pp + i][0].astype(BF16))
            pv = term if pv is None else pv + term
        if fox:
            acc_ref[s] = alpha[s][:, 0:1] * acc_ref[s] + pv
        else:
            acc_ref[s] = acc_ref[s] + pv

    @pl.when(j == pl.num_programs(1) - 1)
    def _():
        own = _iota((8, GROUP_WIDTH), 1) // HEAD_DIM == _iota((8, GROUP_WIDTH), 0)
        for s in seqs:
            acc = acc_ref[s]
            if fox:
                acc = acc / l_ref[s][:, 0:1]
            o_ref[s] = jnp.sum(jnp.where(own, acc, 0.0), axis=0, keepdims=True)


def _dec_attn(page_table, q, k_new, v_new, small, cache_k, cache_v, cache_lf_t, *, layer, fox):
    nb = q.shape[0]
    n_pages = page_table.shape[1]
    page = cache_k.shape[3]
    assert page == 128, "per-page statistics are kept one page per vreg row group"
    pp = _pick(n_pages, (PAGES_PER_STEP, 8, 4, 2, 1))
    ns = _pick(nb, (SEQS_PER_STEP, 1))
    row3 = lambda a: a.reshape(nb, 1, a.shape[-1])
    rspec = lambda w: pl.BlockSpec((ns, 1, w), lambda b, j, pt: (b, 0, 0))

    def page_map(s, i):
        return lambda b, j, pt: (layer, pt[b * ns + s, n_pages - 1 - (j * pp + i)], 0, 0)

    slots = [(s, i) for s in range(ns) for i in range(pp)]
    in_specs = [rspec(GROUP_WIDTH)] * 3 + [rspec(SMALL_WIDTH)]
    in_specs += [pl.BlockSpec((None, 1, GROUP_WIDTH, page), page_map(s, i)) for s, i in slots] * 2
    args = [row3(q), row3(k_new), row3(v_new), row3(small)] + [cache_k] * len(slots) + [cache_v] * len(slots)
    if fox:
        in_specs += [pl.BlockSpec((None, 1, N_HEADS, page), page_map(s, i)) for s, i in slots]
        args += [cache_lf_t] * len(slots)
    grid_spec = pltpu.PrefetchScalarGridSpec(
        num_scalar_prefetch=1, grid=(nb // ns, n_pages // pp), in_specs=in_specs,
        out_specs=pl.BlockSpec((ns, 1, GROUP_WIDTH), lambda b, j, pt: (b, 0, 0)),
        scratch_shapes=[pltpu.VMEM((ns, 8, 128), F32), pltpu.VMEM((ns, 8, 128), F32),
                        pltpu.VMEM((ns, 8, GROUP_WIDTH), F32), pltpu.VMEM((ns, 8, 128), F32)])
    out = pl.pallas_call(
        functools.partial(_dec_attn_kernel, fox=fox, ns=ns, pp=pp, page=page), grid_spec=grid_spec,
        out_shape=jax.ShapeDtypeStruct((nb, 1, GROUP_WIDTH), F32),
        compiler_params=pltpu.CompilerParams(dimension_semantics=("arbitrary", "arbitrary"),
                                             vmem_limit_bytes=VMEM_LIMIT),
        name="fox_step" if fox else "sb_step",
    )(page_table, *args)
    return out.reshape(nb, GROUP_WIDTH)


def _column(row, eye):
    return jnp.sum(eye * row, axis=1, keepdims=True)


def _rec_step_kernel(hq_ref, hlf_ref, hk_ref, hi_ref, dx_ref, sm_ref, w_ref, shg_ref, sdn_ref, buf_ref,
                     ohg_ref, odn_ref, shg_o_ref, sdn_o_ref, buf_o_ref):
    eye = jnp.where(_iota((HEAD_DIM, HEAD_DIM), 0) == _iota((HEAD_DIM, HEAD_DIM), 1), 1.0, 0.0)
    hq = hq_ref[0]
    hlf = hlf_ref[0]
    hk = hk_ref[0]
    hv = hi_ref[0]
    sm = sm_ref[0]
    buf = buf_ref[0]
    x_new = dx_ref[0]
    y = x_new * w_ref[CONV_WIDTH - 1:CONV_WIDTH, :]
    for jw in range(CONV_WIDTH - 1):
        y = y + buf[jw:jw + 1, :] * w_ref[jw:jw + 1, :]
    y = _silu(y)
    buf_o_ref[0] = jnp.concatenate([buf[1:CONV_WIDTH - 1, :], x_new], axis=0)
    for h in range(N_HEADS):
        cs = slice(h * HEAD_DIM, (h + 1) * HEAD_DIM)
        s = shg_ref[0, h]
        s = _column(jnp.exp(hlf[:, cs]), eye) * s + _column(hk[:, cs], eye) * hv[:, cs]
        shg_o_ref[0, h] = s
        ohg_ref[0, :, cs] = jnp.sum(_column(hq[:, cs], eye) * s, axis=0, keepdims=True)
        q = y[:, cs]
        k = y[:, GROUP_WIDTH + h * HEAD_DIM:GROUP_WIDTH + (h + 1) * HEAD_DIM]
        v = y[:, 2 * GROUP_WIDTH + h * HEAD_DIM:2 * GROUP_WIDTH + (h + 1) * HEAD_DIM]
        q = q * lax.rsqrt(jnp.sum(q * q, axis=-1, keepdims=True) + NORM_EPS) * QK_SCALE
        k = k * lax.rsqrt(jnp.sum(k * k, axis=-1, keepdims=True) + NORM_EPS)
        beta = sm[:, LANE_BETA + h:LANE_BETA + h + 1]
        a = jnp.exp(sm[:, LANE_GDEC + h:LANE_GDEC + h + 1])
        s = sdn_ref[0, h]
        kc = _column(k, eye)
        v_new = beta * (v - a * jnp.sum(kc * s, axis=0, keepdims=True))
        s = a * s + kc * v_new
        sdn_o_ref[0, h] = s
        odn_ref[0, :, cs] = jnp.sum(_column(q, eye) * s, axis=0, keepdims=True)


def _rec_step(hq, hlf, hk, hi, dqkv, small, conv_w, s_hg, s_dn, buf):
    nb = hq.shape[0]
    row3 = lambda a: a.reshape(nb, 1, a.shape[-1])
    rspec = lambda w: pl.BlockSpec((1, 1, w), lambda b: (b, 0, 0))
    st_spec = pl.BlockSpec((1, N_HEADS, HEAD_DIM, HEAD_DIM), lambda b: (b, 0, 0, 0))
    buf_spec = pl.BlockSpec((1, CONV_WIDTH - 1, 3 * GROUP_WIDTH), lambda b: (b, 0, 0))
    outs = pl.pallas_call(
        _rec_step_kernel, grid=(nb,),
        in_specs=[rspec(GROUP_WIDTH)] * 4 + [rspec(3 * GROUP_WIDTH), rspec(SMALL_WIDTH),
                                             pl.BlockSpec(conv_w.shape, lambda b: (0, 0)), st_spec, st_spec, buf_spec],
        out_specs=[rspec(GROUP_WIDTH), rspec(GROUP_WIDTH), st_spec, st_spec, buf_spec],
        out_shape=[jax.ShapeDtypeStruct((nb, 1, GROUP_WIDTH), F32)] * 2
        + [jax.ShapeDtypeStruct(s_hg.shape, F32), jax.ShapeDtypeStruct(s_dn.shape, F32),
           jax.ShapeDtypeStruct(buf.shape, F32)],
        compiler_params=pltpu.CompilerParams(dimension_semantics=("arbitrary",), vmem_limit_bytes=VMEM_LIMIT),
        name="recurrent_step",
    )(row3(hq), row3(hlf), row3(hk), row3(hi), row3(dqkv), row3(small), conv_w, s_hg, s_dn, buf)
    ohg, odn, s_hg_new, s_dn_new, buf_new = outs
    return ohg.reshape(nb, GROUP_WIDTH), odn.reshape(nb, GROUP_WIDTH), s_hg_new, s_dn_new, buf_new


def _tile_gain(g):
    return jnp.tile(g.astype(F32), N_HEADS)


def _relayout_w_in(w_in_t_l):
    gw = GROUP_WIDTH
    a = 7 * gw
    e = a + N_HEADS + 7 * gw
    pad = jnp.zeros((SMALL_WIDTH - 3 * N_HEADS, w_in_t_l.shape[1]), w_in_t_l.dtype)
    rows = [w_in_t_l[:a], w_in_t_l[a + N_HEADS:e], w_in_t_l[a:a + N_HEADS], w_in_t_l[e:e + 2 * N_HEADS], pad]
    return jnp.concatenate(rows, axis=0).astype(BF16)


def _small_params(f_bias, dt_bias, a_log):
    sp = jnp.zeros((8, SMALL_WIDTH), F32)
    sp = sp.at[0, LANE_FLOG:LANE_FLOG + N_HEADS].set(f_bias.astype(F32))
    sp = sp.at[1, LANE_GDEC:LANE_GDEC + N_HEADS].set(dt_bias.astype(F32))
    sp = sp.at[2, LANE_GDEC:LANE_GDEC + N_HEADS].set(a_log.astype(F32))
    return sp


def _pick(n, candidates):
    for c in candidates:
        if n % c == 0:
            return c
    return n


def kernel(x_prompt, x_sample, cache_fox_k, cache_fox_v, cache_fox_logf, cache_sb_k, cache_sb_v, state_hgrn, state_dn, state_dn_conv, page_table, hgrn_lb_param, w_in, w_out, ln1_g, ln2_g, fox_f_bias, fox_q_norm, fox_k_norm, sb_q_norm, sb_k_norm, hgrn_out_norm, fox_out_norm, sb_out_norm, dn_out_norm, dn_conv_w, dn_dt_bias, dn_a_log, w_up, w_down):
    depth = w_in.shape[0]
    bsz, seq, d = x_prompt.shape
    nb = x_sample.shape[0]
    n_phys, page = cache_fox_k.shape[1], cache_fox_k.shape[2]
    m = bsz * seq
    tq = _pick(seq, (512, 256, 128))
    tm = _pick(tq, (256, 128))
    tm_mlp = _pick(m, (512, 256, 128, 64, 32, 16, 8))
    tf = _pick(w_up.shape[2], (1024, 512, 256, 128))
    tt = _pick(seq, (256, 128, 64))

    yp = x_prompt.reshape(m, d)
    ys = x_sample.reshape(nb, d)
    lbp = hgrn_lb_param.astype(F32)
    w_in_t = jnp.transpose(w_in, (2, 0, 1))
    kv_t = lambda a: a.transpose(0, 1, 3, 4, 2).reshape(depth, n_phys, GROUP_WIDTH, page)
    lf_t = jnp.swapaxes(cache_fox_logf.astype(F32), 2, 3)
    p_out = [[] for _ in range(8)]
    s_out = [[] for _ in range(8)]
    for l in range(depth):
        w_re = _relayout_w_in(w_in_t[:, l, :])
        w_o = w_out[l].astype(BF16)
        w_u = w_up[l].astype(BF16)
        w_d = w_down[l].astype(BF16)
        g1 = ln1_g[l].reshape(1, d).astype(F32)
        g2 = ln2_g[l].reshape(1, d).astype(F32)
        qk_gains = jnp.stack([_tile_gain(fox_q_norm[l]), _tile_gain(fox_k_norm[l]),
                              _tile_gain(sb_q_norm[l]), _tile_gain(sb_k_norm[l])])
        out_gains = jnp.stack([_tile_gain(hgrn_out_norm[l]), _tile_gain(fox_out_norm[l]),
                               _tile_gain(sb_out_norm[l]), _tile_gain(dn_out_norm[l])])
        sp = _small_params(fox_f_bias[l], dn_dt_bias[l], dn_a_log[l])
        conv_w = dn_conv_w[l].astype(F32)

        (hq, hlf, hk, hi, hgate, fq, fk, fv, sq, sk, sv, dqkv, dz, small, cum, small_t, cum_t,
         fk_blk, fv_blk, sk_blk, sv_blk) = _inproj(
            yp, g1, w_re, lbp, qk_gains, sp, layer=l, tm=tm, rows_per_seq=seq, with_time=True, key_block=tq)
        b3 = lambda a: a.reshape(bsz, seq, a.shape[-1])
        o_hg, st_hg = _hgrn_prompt(b3(hq), b3(hlf), b3(hk), b3(hi), tt=tt)
        o_fx = _fox_prompt(b3(fq), fk_blk, fv_blk, b3(cum), cum_t, tq=tq)
        o_sb = _sb_prompt(b3(sq), sk_blk, sv_blk, tq=tq)
        o_dn, st_dn = _gdn_prompt(b3(dqkv), conv_w, b3(small), small_t, tt=tt)
        yp = _outproj(yp, o_hg.reshape(m, -1), hgate, o_fx.reshape(m, -1), o_sb.reshape(m, -1),
                      o_dn.reshape(m, -1), dz, out_gains, w_o, tm=tm)
        yp = _mlp(yp, g2, w_u, w_d, tm=tm_mlp, tf=tf)
        by_head = lambda a: a.reshape(bsz, N_HEADS, HEAD_DIM, seq).transpose(0, 3, 1, 2)
        p_out[0].append(by_head(fk))
        p_out[1].append(by_head(fv))
        p_out[2].append(b3(small)[:, :, LANE_FLOG:LANE_FLOG + N_HEADS])
        p_out[3].append(by_head(sk))
        p_out[4].append(by_head(sv))
        p_out[5].append(jnp.swapaxes(st_hg, -1, -2))
        p_out[6].append(st_dn)
        p_out[7].append(b3(dqkv)[:, seq - (CONV_WIDTH - 1):, :])

        (hq, hlf, hk, hi, hgate, fq, fk, fv, sq, sk, sv, dqkv, dz, small) = _inproj(
            ys, g1, w_re, lbp, qk_gains, sp, layer=l, tm=nb, rows_per_seq=nb, with_time=False)
        o_fx = _dec_attn(page_table, fq, fk, fv, small, kv_t(cache_fox_k), kv_t(cache_fox_v), lf_t,
                         layer=l, fox=True)
        o_sb = _dec_attn(page_table, sq, sk, sv, small, kv_t(cache_sb_k), kv_t(cache_sb_v), None,
                         layer=l, fox=False)
        o_hg, o_dn, s_hg_new, s_dn_new, buf_new = _rec_step(
            hq, hlf, hk, hi, dqkv, small, conv_w, state_hgrn[l].astype(F32), state_dn[l].astype(F32),
            state_dn_conv[l].astype(F32))
        ys = _outproj(ys, o_hg, hgate, o_fx, o_sb, o_dn, dz, out_gains, w_o, tm=nb)
        ys = _mlp(ys, g2, w_u, w_d, tm=nb, tf=tf)
        sshape = (nb, 1, N_HEADS, HEAD_DIM)
        s_out[0].append(fk.reshape(sshape))
        s_out[1].append(fv.reshape(sshape))
        s_out[2].append(small[:, LANE_FLOG:LANE_FLOG + N_HEADS].reshape(nb, 1, N_HEADS))
        s_out[3].append(sk.reshape(sshape))
        s_out[4].append(sv.reshape(sshape))
        s_out[5].append(s_hg_new)
        s_out[6].append(s_dn_new)
        s_out[7].append(buf_new)

    p = [jnp.stack(v) for v in p_out]
    s = [jnp.stack(v) for v in s_out]
    return (yp.reshape(bsz, seq, d), ys.reshape(nb, 1, d), *p, *s)
```

```python
import functools

import jax
import jax.numpy as jnp
from jax import lax
from jax.experimental import pallas as pl
from jax.experimental.pallas import tpu as pltpu

F32 = jnp.float32
BF16 = jnp.bfloat16

HEAD_DIM = 64
N_HEADS = 4
GROUP_WIDTH = N_HEADS * HEAD_DIM
N_SEGMENTS = 14
SMALL_WIDTH = 128
CONV_WIDTH = 4
NORM_EPS = 1e-6
NEG_BIG = -1e30
QK_SCALE = HEAD_DIM ** -0.5
LOG2E = 1.4426950408889634
INV_LN2 = LOG2E
PRUNE_LOG2 = 160.0
BOUND_SLACK = 1.001
VMEM_LIMIT = 56 * 1024 * 1024

LANE_FLOG = 0
LANE_BETA = 4
LANE_GDEC = 8
SMALL_ROWS_T = 16


def _iota(shape, dim):
    return lax.broadcasted_iota(jnp.int32, shape, dim)


def _dot(a, b):
    return jnp.dot(a, b, preferred_element_type=F32)


def _dot_nt(a, b):
    return lax.dot_general(a, b, (((1,), (1,)), ((), ())), preferred_element_type=F32)


def _split3(x):
    hi = x.astype(BF16)
    r = x - hi.astype(F32)
    mid = r.astype(BF16)
    lo = (r - mid.astype(F32)).astype(BF16)
    return hi, mid, lo


def _dot_xc(x, c, parts=3):
    ps = _split3(x)[:parts]
    out = _dot(ps[0], c)
    for p in ps[1:]:
        out = out + _dot(p, c)
    return out


def _dot_cx(c, x, parts=3):
    ps = _split3(x)[:parts]
    out = _dot(c, ps[0])
    for p in ps[1:]:
        out = out + _dot(c, p)
    return out


def _dot_f32(a, b):
    ah = a.astype(BF16)
    al = (a - ah.astype(F32)).astype(BF16)
    bh = b.astype(BF16)
    bl = (b - bh.astype(F32)).astype(BF16)
    return _dot(ah, bh) + _dot(ah, bl) + _dot(al, bh)


def _head_ones(n=GROUP_WIDTH):
    return (_iota((n, n), 0) // HEAD_DIM == _iota((n, n), 1) // HEAD_DIM).astype(BF16)


def _head_sum(x, ones):
    return _dot_xc(x, ones)


def _head_rms(x, ones, gain):
    ms = _head_sum(x * x, ones) * (1.0 / HEAD_DIM)
    return x * lax.rsqrt(ms + NORM_EPS) * gain


def _log_sigmoid(x):
    return jnp.minimum(x, 0.0) - jnp.log1p(jnp.exp(-jnp.abs(x)))


def _softplus(x):
    return jnp.maximum(x, 0.0) + jnp.log1p(jnp.exp(-jnp.abs(x)))


def _sigmoid(x):
    return 1.0 / (1.0 + jnp.exp(-x))


def _silu(x):
    return x * _sigmoid(x)


def _max_all(x):
    return jnp.max(jnp.max(x, axis=0, keepdims=True), axis=1, keepdims=True)


def _lane_pack(vals):
    lane = _iota((1, 128), 1)
    out = jnp.zeros((1, 128), F32)
    for i, v in enumerate(vals):
        out = jnp.where(lane == i, v, out)
    return out


def _any_head(mask):
    lane = _iota(mask.shape, 1)
    return jnp.max(jnp.where(mask & (lane < N_HEADS), 1.0, 0.0)) > 0.0


def _inproj_kernel(x_ref, g1_ref, w_ref, lbp_ref, gains_ref, sp_ref, *refs, layer, depth, tiles_per_seq,
                   with_time):
    (hq_ref, hlf_ref, hk_ref, hi_ref, hg_ref, fq_ref, fk_ref, fv_ref, sq_ref, sk_ref, sv_ref,
     dqkv_ref, dz_ref, small_ref) = refs[:14]
    x = x_ref[...]
    h = (x * lax.rsqrt(jnp.mean(x * x, axis=-1, keepdims=True) + NORM_EPS) * g1_ref[...]).astype(BF16)

    def seg(j, width=GROUP_WIDTH):
        return _dot_nt(h, w_ref[j * GROUP_WIDTH:j * GROUP_WIDTH + width, :])

    ones = _head_ones()

    rows = [lbp_ref[i:i + 1, :] for i in range(depth)]
    mx = functools.reduce(jnp.maximum, rows)
    es = [jnp.exp(r - mx) for r in rows]
    lb = sum(es[1:layer + 1], jnp.zeros_like(mx)) / sum(es)
    hq_ref[...] = seg(0)
    hf = seg(1)
    a = jnp.log(lb)
    b = jnp.log1p(-lb) + _log_sigmoid(hf)
    hi = jnp.maximum(a, b)
    lo = jnp.minimum(a, b)
    hlf_ref[...] = hi + jnp.log1p(jnp.exp(lo - hi))
    hk_ref[...] = (1.0 - lb) * _sigmoid(-hf)
    hi_ref[...] = seg(2)
    hg_ref[...] = _silu(seg(3))

    fq_ref[...] = _head_rms(seg(4), ones, gains_ref[0:1, :]) * QK_SCALE
    sq_ref[...] = _head_rms(seg(7), ones, gains_ref[2:3, :]) * QK_SCALE
    kv = (_head_rms(seg(5), ones, gains_ref[1:2, :]), seg(6), _head_rms(seg(8), ones, gains_ref[3:4, :]), seg(9))
    for idx, (ref, val) in enumerate(zip((fk_ref, fv_ref, sk_ref, sv_ref), kv)):
        if with_time:
            val_t = val.T
            ref[0] = val_t
            refs[17 + idx][0, 0] = val_t.astype(BF16)
        else:
            ref[...] = val

    dqkv_ref[...] = seg(10, 3 * GROUP_WIDTH)
    dz_ref[...] = _silu(seg(13))

    s = _dot_nt(h, w_ref[N_SEGMENTS * GROUP_WIDTH:, :])
    lane = _iota(s.shape, 1)
    f_log = _log_sigmoid(s + sp_ref[0:1, :])
    beta = _sigmoid(s)
    g_dec = -jnp.exp(sp_ref[2:3, :]) * _softplus(s + sp_ref[1:2, :])
    small = jnp.where(lane < LANE_BETA, f_log, jnp.where(lane < LANE_GDEC, beta, g_dec))
    small_ref[...] = small

    if with_time:
        cum_ref, small_t_ref, cum_t_ref = refs[14:17]
        carry_ref = refs[21]
        tm = s.shape[0]

        @pl.when(pl.program_id(0) % tiles_per_seq == 0)
        def _():
            carry_ref[...] = jnp.zeros_like(carry_ref)

        tril = (_iota((tm, tm), 1) <= _iota((tm, tm), 0)).astype(BF16)
        cum = _dot_cx(tril, small) + carry_ref[0:1, :]
        cum_ref[...] = cum
        carry_ref[...] = jnp.broadcast_to(cum[tm - 1:tm, :], carry_ref.shape)
        small_t_ref[0] = small.T[:SMALL_ROWS_T, :]
        cum_t_ref[0] = cum.T[:SMALL_ROWS_T, :]


def _inproj(x2, g1, w_re, lbp, gains, sp, *, layer, tm, rows_per_seq, with_time, key_block=None):
    m, d = x2.shape
    depth = lbp.shape[0]
    grid = (m // tm,)
    row = lambda i: (i, 0)
    const = lambda i: (0, 0)
    tps = rows_per_seq // tm
    nseq = m // rows_per_seq
    seg_shape = jax.ShapeDtypeStruct((m, GROUP_WIDTH), F32)
    seg_spec = pl.BlockSpec((tm, GROUP_WIDTH), row)
    out_shape = [seg_shape] * 11 + [jax.ShapeDtypeStruct((m, 3 * GROUP_WIDTH), F32), seg_shape,
                                    jax.ShapeDtypeStruct((m, SMALL_WIDTH), F32)]
    out_specs = [seg_spec] * 11 + [pl.BlockSpec((tm, 3 * GROUP_WIDTH), row), seg_spec,
                                   pl.BlockSpec((tm, SMALL_WIDTH), row)]
    scratch = []
    if with_time:
        def by_time(rows):
            return (jax.ShapeDtypeStruct((nseq, rows, rows_per_seq), F32),
                    pl.BlockSpec((1, rows, tm), lambda i: (i // tps, 0, i % tps)))
        for idx in (6, 7, 9, 10):
            out_shape[idx], out_specs[idx] = by_time(GROUP_WIDTH)
        out_shape += [jax.ShapeDtypeStruct((m, SMALL_WIDTH), F32)]
        out_specs += [pl.BlockSpec((tm, SMALL_WIDTH), row)]
        for _ in range(2):
            sh, sp_ = by_time(SMALL_ROWS_T)
            out_shape.append(sh)
            out_specs.append(sp_)
        per_blk = key_block // tm
        for _ in range(4):
            out_shape.append(jax.ShapeDtypeStruct((nseq, rows_per_seq // key_block, GROUP_WIDTH, key_block), BF16))
            out_specs.append(pl.BlockSpec((1, 1, GROUP_WIDTH, tm),
                                          lambda i: (i // tps, (i % tps) // per_blk, 0, (i % tps) % per_blk)))
        scratch = [pltpu.VMEM((8, SMALL_WIDTH), F32)]
    kern = functools.partial(_inproj_kernel, layer=layer, depth=depth, tiles_per_seq=tps,
                             with_time=with_time)
    return pl.pallas_call(
        kern, grid=grid,
        in_specs=[pl.BlockSpec((tm, d), row), pl.BlockSpec((1, d), const), pl.BlockSpec(w_re.shape, const),
                  pl.BlockSpec(lbp.shape, const), pl.BlockSpec(gains.shape, const), pl.BlockSpec(sp.shape, const)],
        out_specs=out_specs, out_shape=out_shape, scratch_shapes=scratch,
        compiler_params=pltpu.CompilerParams(dimension_semantics=("arbitrary",), vmem_limit_bytes=VMEM_LIMIT),
        name="inproj_time" if with_time else "inproj_step",
    )(x2, g1, w_re, lbp, gains, sp)


HGRN_SUB = 16


def _hgrn_kernel(q_ref, lf_ref, k_ref, v_ref, o_ref, st_ref, s_ref, oi_ref, *, tt):
    t = pl.program_id(1)

    @pl.when(t == 0)
    def _():
        s_ref[...] = jnp.zeros_like(s_ref)

    q = q_ref[0]
    lf = lf_ref[0]
    kin = k_ref[0]
    v = v_ref[0]
    r = _iota((tt, tt), 0)
    c = _iota((tt, tt), 1)
    same = (r // HGRN_SUB) == (c // HGRN_SUB)
    g = _dot_cx(jnp.where(same & (c <= r), 1.0, 0.0).astype(BF16), lf)
    gl = _dot_cx(jnp.where(same, 1.0, 0.0).astype(BF16), lf)
    qg = q * jnp.exp(g)
    kg = kin * jnp.exp(gl - g)

    ones = _head_ones()
    rowmod = _iota((tt, GROUP_WIDTH), 0) % HGRN_SUB
    o = jnp.zeros((tt, GROUP_WIDTH), F32)
    for d in range(HGRN_SUB):
        if d == 0:
            kd, gd, vd = kin, g, v
        else:
            kd = pltpu.roll(kin, d, 0)
            gd = pltpu.roll(g, d, 0)
            vd = pltpu.roll(v, d, 0)
        e = jnp.where(rowmod >= d, g - gd, NEG_BIG)
        p = q * kd * jnp.exp(e)
        o = o + _dot_xc(p, ones, parts=1) * vd

    v_t = v.T
    for i in range(tt // HGRN_SUB):
        rs = slice(i * HGRN_SUB, (i + 1) * HGRN_SUB)
        for h in range(N_HEADS):
            cs = slice(h * HEAD_DIM, (h + 1) * HEAD_DIM)
            s = s_ref[h]
            oi_ref[rs, cs] = _dot_nt(qg[rs, cs].astype(BF16), s.astype(BF16))
            dec = jnp.exp(gl[i * HGRN_SUB:i * HGRN_SUB + 1, cs])
            s_ref[h] = dec * s + _dot(v_t[cs, rs].astype(BF16), kg[rs, cs].astype(BF16))
    o_ref[0] = o + oi_ref[...]

    @pl.when(t == pl.num_programs(1) - 1)
    def _():
        st_ref[0] = s_ref[...]


def _hgrn_prompt(hq, hlf, hk, hi, *, tt):
    b, t, _ = hq.shape
    blk = pl.BlockSpec((1, tt, GROUP_WIDTH), lambda i, j: (i, j, 0))
    return pl.pallas_call(
        functools.partial(_hgrn_kernel, tt=tt), grid=(b, t // tt),
        in_specs=[blk] * 4,
        out_specs=[blk, pl.BlockSpec((1, N_HEADS, HEAD_DIM, HEAD_DIM), lambda i, j: (i, 0, 0, 0))],
        out_shape=[jax.ShapeDtypeStruct((b, t, GROUP_WIDTH), F32),
                   jax.ShapeDtypeStruct((b, N_HEADS, HEAD_DIM, HEAD_DIM), F32)],
        scratch_shapes=[pltpu.VMEM((N_HEADS, HEAD_DIM, HEAD_DIM), F32), pltpu.VMEM((tt, GROUP_WIDTH), F32)],
        compiler_params=pltpu.CompilerParams(dimension_semantics=("arbitrary", "arbitrary"),
                                             vmem_limit_bytes=VMEM_LIMIT),
        name="hgrn_prompt",
    )(hq, hlf, hk, hi)


FOX_WINDOW = 8


def _fox_kernel(q_ref, k_ref, v_ref, fq_ref, fk_ref, o_ref, *refs, tq, windowed):
    if windowed:
        more_ref, *refs = refs
    qa_ref, m_ref, acc_ref, qn_ref, fq_max_ref, m_min_ref, kn_ref, fk_min_ref = refs
    qi = pl.program_id(1)
    j = pl.program_id(2)
    last_j = jnp.minimum(qi, pl.num_programs(2) - 1)

    @pl.when((j == 0) & (qi == 0))
    def _():
        kn_ref[...] = jnp.zeros_like(kn_ref)
        fk_min_ref[...] = jnp.zeros_like(fk_min_ref)

    @pl.when(j == 0)
    def _():
        m_ref[...] = jnp.full_like(m_ref, NEG_BIG)
        acc_ref[...] = jnp.zeros_like(acc_ref)
        q = q_ref[0] * LOG2E
        f = fq_ref[0] * LOG2E
        lane = _iota((tq, HEAD_DIM), 1)
        for h in range(N_HEADS):
            hi, mid, lo = [p.astype(F32) for p in _split3(f[:, LANE_FLOG + h:LANE_FLOG + h + 1])]
            ext = jnp.where(lane == 0, hi, jnp.where(lane == 1, mid, jnp.where(lane == 2, lo,
                                                                               jnp.where(lane < 6, 1.0, 0.0))))
            qa_ref[h] = jnp.concatenate([q[:, h * HEAD_DIM:(h + 1) * HEAD_DIM], ext], axis=1).astype(BF16)

        heads = range(N_HEADS)
        rows = lambda h: slice(h * HEAD_DIM, (h + 1) * HEAD_DIM)
        qb = q.astype(BF16).astype(F32)
        qn2 = _head_sum(qb * qb, _head_ones())
        kf = k_ref[0, 0].astype(F32)
        k2 = kf * kf
        fk = fk_ref[0] * LOG2E
        qn_ref[...] = _lane_pack([_max_all(qn2[:, rows(h)]) for h in heads])
        fq_max_ref[...] = _lane_pack([_max_all(f[:, LANE_FLOG + h:LANE_FLOG + h + 1]) for h in heads])
        kn_ref[pl.ds(qi, 1), :] = _lane_pack([_max_all(jnp.sum(k2[rows(h), :], axis=0, keepdims=True))
                                              for h in heads])
        fk_min_ref[pl.ds(qi, 1), :] = _lane_pack([-_max_all(-fk[LANE_FLOG + h:LANE_FLOG + h + 1, :])
                                                  for h in heads])

    def may_matter(kn_rows, fk_min_rows):
        zcap2 = qn_ref[...] * kn_rows * (BOUND_SLACK * BOUND_SLACK)
        bias_cap = fq_max_ref[...] - fk_min_rows
        room = m_min_ref[...] - bias_cap - PRUNE_LOG2
        return (room < 0.0) | (zcap2 > room * room)

    def block_matters():
        return _any_head(may_matter(kn_ref[pl.ds(qi - j, 1), :], fk_min_ref[pl.ds(qi - j, 1), :]))

    def step(masked):
        kf = k_ref[0, 0]
        vf = v_ref[0, 0]
        fk = fk_ref[0] * LOG2E
        row = _iota((8, tq), 0)
        pad = jnp.zeros((HEAD_DIM - 8, tq), F32)
        v_ext = jnp.concatenate([jnp.where(row == 0, 1.0, 0.0), pad], axis=0).astype(BF16)
        if masked:
            keep = _iota((tq, tq), 1) <= _iota((tq, tq), 0)
        heads = range(N_HEADS)
        ss = []
        for h in heads:
            cs = slice(h * HEAD_DIM, (h + 1) * HEAD_DIM)
            hi, mid, lo = [p.astype(F32) for p in _split3(fk[LANE_FLOG + h:LANE_FLOG + h + 1, :])]
            k_ext = jnp.where(row < 3, 1.0, jnp.where(row == 3, -hi, jnp.where(row == 4, -mid,
                                                                               jnp.where(row == 5, -lo, 0.0))))
            ka = jnp.concatenate([kf[cs, :], jnp.concatenate([k_ext, pad], axis=0).astype(BF16)], axis=0)
            ss.append(_dot(qa_ref[h], ka))
        if masked:
            ss = [jnp.where(keep, s, NEG_BIG) for s in ss]
        m_prevs = [m_ref[h] for h in heads]
        m_news = [jnp.maximum(m_prevs[h], jnp.max(ss[h], axis=-1, keepdims=True)) for h in heads]
        ps = [jnp.exp2(ss[h] - jnp.concatenate([m_news[h]] * (tq // 128), axis=1)) for h in heads]
        for h in heads:
            va = jnp.concatenate([vf[h * HEAD_DIM:(h + 1) * HEAD_DIM, :], v_ext], axis=0)
            m_ref[h] = m_news[h]
            acc_ref[h] = jnp.exp2(m_prevs[h] - m_news[h]) * acc_ref[h] + _dot_nt(ps[h].astype(BF16), va)
        m_min_ref[...] = _lane_pack([-_max_all(-m_news[h]) for h in heads])

    @pl.when(j == 0)
    def _():
        step(True)

    @pl.when((j > 0) & (j <= qi))
    def _():
        @pl.when(block_matters())
        def _():
            step(False)

    @pl.when(j == last_j)
    def _():
        for h in range(N_HEADS):
            acc = acc_ref[h]
            o_ref[0, :, h * HEAD_DIM:(h + 1) * HEAD_DIM] = acc[:, :HEAD_DIM] / acc[:, HEAD_DIM:HEAD_DIM + 1]
        if windowed:
            older = _iota(kn_ref.shape, 0) <= qi - pl.num_programs(2)
            more = _any_head(may_matter(kn_ref[...], fk_min_ref[...]) & older)
            more_ref[0, 0] = jnp.where(more, jnp.ones((8, 128), F32), jnp.zeros((8, 128), F32))


def _fox_prompt(fq, fk, fv, cum, cum_t, *, tq):
    b, t, _ = fq.shape
    n = t // tq
    qspec = pl.BlockSpec((1, tq, GROUP_WIDTH), lambda i, qi, j: (i, qi, 0))
    kspec = pl.BlockSpec((1, 1, GROUP_WIDTH, tq), lambda i, qi, j: (i, jnp.maximum(qi - j, 0), 0, 0))

    def call(n_keys):
        windowed = n_keys < n
        out_specs = [qspec]
        out_shape = [jax.ShapeDtypeStruct((b, t, GROUP_WIDTH), F32)]
        if windowed:
            out_specs.append(pl.BlockSpec((1, 1, 8, 128), lambda i, qi, j: (i, qi, 0, 0)))
            out_shape.append(jax.ShapeDtypeStruct((b, n, 8, 128), F32))
        return pl.pallas_call(
            functools.partial(_fox_kernel, tq=tq, windowed=windowed), grid=(b, n, n_keys),
            in_specs=[qspec, kspec, kspec,
                      pl.BlockSpec((1, tq, SMALL_WIDTH), lambda i, qi, j: (i, qi, 0)),
                      pl.BlockSpec((1, SMALL_ROWS_T, tq), lambda i, qi, j: (i, 0, jnp.maximum(qi - j, 0)))],
            out_specs=out_specs, out_shape=out_shape,
            scratch_shapes=[pltpu.VMEM((N_HEADS, tq, 2 * HEAD_DIM), BF16),
                            pltpu.VMEM((N_HEADS, tq, 128), F32), pltpu.VMEM((N_HEADS, tq, 2 * HEAD_DIM), F32),
                            pltpu.VMEM((1, 128), F32), pltpu.VMEM((1, 128), F32), pltpu.VMEM((1, 128), F32),
                            pltpu.VMEM((n, 128), F32), pltpu.VMEM((n, 128), F32)],
            compiler_params=pltpu.CompilerParams(dimension_semantics=("arbitrary", "arbitrary", "arbitrary"),
                                                 vmem_limit_bytes=VMEM_LIMIT),
            name="fox_prompt_window" if windowed else "fox_prompt",
        )(fq, fk, fv, cum, cum_t)

    if n <= FOX_WINDOW:
        return call(n)[0]
    o_near, more = call(FOX_WINDOW)
    return lax.cond(jnp.max(more) > 0.0, lambda: call(n)[0], lambda: o_near)


SB_SUB = 256


def _sb_kernel(q_ref, k_ref, v_ref, o_ref, *refs, tq, windowed):
    if windowed:
        more_ref, *refs = refs
    qb_ref, carry_ref, acc_ref, qn_ref, kn_ref, cmin_ref = refs
    qi = pl.program_id(1)
    j = pl.program_id(2)
    last_j = jnp.minimum(qi, pl.num_programs(2) - 1)

    heads = range(N_HEADS)
    rows = lambda h: slice(h * HEAD_DIM, (h + 1) * HEAD_DIM)

    @pl.when((j == 0) & (qi == 0))
    def _():
        kn_ref[...] = jnp.zeros_like(kn_ref)

    @pl.when(j == 0)
    def _():
        carry_ref[...] = jnp.zeros_like(carry_ref)
        acc_ref[...] = jnp.zeros_like(acc_ref)
        qb = (q_ref[0] * LOG2E).astype(BF16)
        qb_ref[...] = qb
        qn2 = _head_sum(qb.astype(F32) * qb.astype(F32), _head_ones())
        kf = k_ref[0, 0].astype(F32)
        k2 = kf * kf
        qn_ref[...] = _lane_pack([_max_all(qn2[:, rows(h)]) for h in heads])
        kn_ref[pl.ds(qi, 1), :] = _lane_pack([_max_all(jnp.sum(k2[rows(h), :], axis=0, keepdims=True))
                                              for h in heads])
        cmin_ref[...] = jnp.zeros_like(cmin_ref)

    def may_matter(kn_rows):
        zcap2 = qn_ref[...] * kn_rows * (BOUND_SLACK * BOUND_SLACK)
        room = cmin_ref[...] - PRUNE_LOG2
        return (room < 0.0) | (zcap2 > room * room)

    def block_matters():
        return _any_head(may_matter(kn_ref[pl.ds(qi - j, 1), :]))

    def step(masked):
        k = k_ref[0, 0]
        v = v_ref[0, 0]
        r = _iota((tq, tq), 0)
        c = _iota((tq, tq), 1)
        sub = min(SB_SUB, tq)
        suffix = jnp.where(_iota((sub, sub), 1) <= _iota((sub, sub), 0), 1.0, 0.0).astype(BF16)
        z2s = [_dot(qb_ref[:, rows(h)], k[rows(h), :]) for h in heads]
        sps = [jnp.maximum(z2, 0.0) + jnp.log(1.0 + jnp.exp2(-jnp.abs(z2))) * INV_LN2 for z2 in z2s]
        if masked:
            sps = [jnp.where(c < r, sp, 0.0) for sp in sps]
        carries = [carry_ref[h] for h in heads]
        cum_parts = [[None] * (tq // sub) for _ in heads]
        for part in reversed(range(tq // sub)):
            ks = slice(part * sub, (part + 1) * sub)
            for h in heads:
                wide = jnp.concatenate([carries[h]] * (sub // 128), axis=1)
                cum = _dot_xc(sps[h][:, ks], suffix, parts=2) + wide
                cum_parts[h][part] = cum
                carries[h] = jnp.broadcast_to(cum[:, 0:1], carries[h].shape)
        es = [z2s[h] - jnp.concatenate(cum_parts[h], axis=1) for h in heads]
        if masked:
            es = [jnp.where(c < r, e, NEG_BIG) for e in es]
        pad = jnp.zeros((HEAD_DIM, tq), BF16)
        for h in heads:
            va = jnp.concatenate([v[rows(h), :], pad], axis=0)
            acc_ref[h] = acc_ref[h] + _dot_nt(jnp.exp2(es[h]).astype(BF16), va)
            carry_ref[h] = carries[h]
        cmin_ref[...] = _lane_pack([-_max_all(-carries[h]) for h in heads])

    @pl.when(j == 0)
    def _():
        step(True)

    @pl.when((j > 0) & (j <= qi))
    def _():
        @pl.when(block_matters())
        def _():
            step(False)

    @pl.when(j == last_j)
    def _():
        for h in range(N_HEADS):
            o_ref[0, :, h * HEAD_DIM:(h + 1) * HEAD_DIM] = acc_ref[h][:, :HEAD_DIM]
        if windowed:
            older = _iota(kn_ref.shape, 0) <= qi - pl.num_programs(2)
            more = _any_head(may_matter(kn_ref[...]) & older)
            more_ref[0, 0] = jnp.where(more, jnp.ones((8, 128), F32), jnp.zeros((8, 128), F32))


SB_WINDOW = 4


def _sb_prompt(sq, sk, sv, *, tq):
    b, t, _ = sq.shape
    n = t // tq
    qspec = pl.BlockSpec((1, tq, GROUP_WIDTH), lambda i, qi, j: (i, qi, 0))
    kspec = pl.BlockSpec((1, 1, GROUP_WIDTH, tq), lambda i, qi, j: (i, jnp.maximum(qi - j, 0), 0, 0))

    def call(n_keys):
        windowed = n_keys < n
        out_specs = [qspec]
        out_shape = [jax.ShapeDtypeStruct((b, t, GROUP_WIDTH), F32)]
        if windowed:
            out_specs.append(pl.BlockSpec((1, 1, 8, 128), lambda i, qi, j: (i, qi, 0, 0)))
            out_shape.append(jax.ShapeDtypeStruct((b, n, 8, 128), F32))
        return pl.pallas_call(
            functools.partial(_sb_kernel, tq=tq, windowed=windowed), grid=(b, n, n_keys),
            in_specs=[qspec, kspec, kspec],
            out_specs=out_specs, out_shape=out_shape,
            scratch_shapes=[pltpu.VMEM((tq, GROUP_WIDTH), BF16),
                            pltpu.VMEM((N_HEADS, tq, 128), F32), pltpu.VMEM((N_HEADS, tq, 2 * HEAD_DIM), F32),
                            pltpu.VMEM((1, 128), F32), pltpu.VMEM((n, 128), F32), pltpu.VMEM((1, 128), F32)],
            compiler_params=pltpu.CompilerParams(dimension_semantics=("arbitrary", "arbitrary", "arbitrary"),
                                                 vmem_limit_bytes=VMEM_LIMIT),
            name="sb_prompt_window" if windowed else "sb_prompt",
        )(sq, sk, sv)

    if n <= SB_WINDOW:
        return call(n)[0]
    o_near, more = call(SB_WINDOW)
    return lax.cond(jnp.max(more) > 0.0, lambda: call(n)[0], lambda: o_near)


GDN_CHUNK = 128
GDN_BASE = 16


def _unit_lower_inverses(lmats, ii, jj):
    ns = [jnp.where(ii // GDN_BASE == jj // GDN_BASE, -lm, 0.0) for lm in lmats]
    eye = jnp.where(ii == jj, 1.0, 0.0)
    ts = [eye + n for n in ns]
    ps = ns
    for _ in range(GDN_BASE.bit_length() - 2):
        ps = [_dot_f32(p, p) for p in ps]
        ts = [t + _dot_f32(t, p) for t, p in zip(ts, ps)]
    b = GDN_BASE
    while b < GDN_CHUNK:
        lower_left = (ii // (2 * b) == jj // (2 * b)) & (ii // b != jj // b)
        mids = [_dot_f32(t, jnp.where(lower_left, lm, 0.0)) for t, lm in zip(ts, lmats)]
        ts = [t - _dot_f32(mid, t) for t, mid in zip(ts, mids)]
        b *= 2
    return ts


def _gdn_prep_kernel(x_ref, w_ref, sm_ref, smt_ref, u_ref, wk_ref, qe_ref, attn_ref, kdt_ref, gct_ref, ext_ref, *, tt):
    t = pl.program_id(1)

    @pl.when(t == 0)
    def _():
        ext_ref[0:8, :] = jnp.zeros((8, 3 * GROUP_WIDTH), F32)

    @pl.when(t > 0)
    def _():
        ext_ref[0:8, :] = ext_ref[tt:tt + 8, :]

    ext_ref[8:8 + tt, :] = x_ref[0]
    y = ext_ref[pl.ds(8 - (CONV_WIDTH - 1), tt), :] * w_ref[0:1, :]
    for jw in range(1, CONV_WIDTH):
        y = y + ext_ref[pl.ds(8 - (CONV_WIDTH - 1) + jw, tt), :] * w_ref[jw:jw + 1, :]
    y = _silu(y)
    ones = _head_ones()
    q = y[:, :GROUP_WIDTH]
    k = y[:, GROUP_WIDTH:2 * GROUP_WIDTH]
    v = y[:, 2 * GROUP_WIDTH:]
    q = q * lax.rsqrt(_head_sum(q * q, ones) + NORM_EPS) * QK_SCALE
    k = k * lax.rsqrt(_head_sum(k * k, ones) + NORM_EPS)
    sm = sm_ref[0]
    smt = smt_ref[0]
    r = _iota((tt, tt), 0)
    c = _iota((tt, tt), 1)
    same = (r // GDN_CHUNK) == (c // GDN_CHUNK)
    gc_col = _dot_cx(jnp.where(same & (c <= r), 1.0, 0.0).astype(BF16), sm)
    gc_row = _dot_xc(smt, jnp.where(same & (r <= c), 1.0, 0.0).astype(BF16))
    gct_ref[0] = gc_row
    k_t = k.T
    ii = _iota((GDN_CHUNK, GDN_CHUNK), 0)
    jj = _iota((GDN_CHUNK, GDN_CHUNK), 1)
    n_chunks = tt // GDN_CHUNK
    lmats, rhss = [], []
    for ci in range(n_chunks):
        rs = slice(ci * GDN_CHUNK, (ci + 1) * GDN_CHUNK)
        qes, attns, kdts = [], [], []
        for h in range(N_HEADS):
            cs = slice(h * HEAD_DIM, (h + 1) * HEAD_DIM)
            gcol = gc_col[rs, LANE_GDEC + h:LANE_GDEC + h + 1]
            grow = gc_row[LANE_GDEC + h:LANE_GDEC + h + 1, rs]
            beta = sm[rs, LANE_BETA + h:LANE_BETA + h + 1]
            dec = jnp.exp(jnp.where(ii >= jj, gcol - grow, NEG_BIG))
            qh = q[rs, cs].astype(BF16)
            kth = k_t[cs, rs]
            kb = k[rs, cs] * beta
            eg = jnp.exp(gcol)
            lmats.append(jnp.where(ii > jj, _dot(kb.astype(BF16), kth.astype(BF16)) * dec, 0.0))
            rhss.append(jnp.concatenate([v[rs, cs] * beta, kb * eg], axis=1))
            qes.append(q[rs, cs] * eg)
            attns.append(_dot(qh, kth.astype(BF16)) * dec)
            kdts.append(kth * jnp.exp(grow[:, GDN_CHUNK - 1:GDN_CHUNK] - grow))
        qe_ref[0, rs, :] = jnp.concatenate(qes, axis=1).astype(BF16)
        attn_ref[0, rs, :] = jnp.concatenate(attns, axis=1).astype(BF16)
        kdt_ref[0, :, rs] = jnp.concatenate(kdts, axis=0).astype(BF16)
    tinvs = _unit_lower_inverses(lmats, ii, jj)
    sols = [_dot_f32(ti, rhs) for ti, rhs in zip(tinvs, rhss)]
    for ci in range(n_chunks):
        rs = slice(ci * GDN_CHUNK, (ci + 1) * GDN_CHUNK)
        chunk = sols[ci * N_HEADS:(ci + 1) * N_HEADS]
        u_ref[0, rs, :] = jnp.concatenate([sol[:, :HEAD_DIM] for sol in chunk], axis=1)
        wk_ref[0, rs, :] = jnp.concatenate([sol[:, HEAD_DIM:] for sol in chunk], axis=1).astype(BF16)


def _gdn_scan_kernel(u_ref, wk_ref, qe_ref, attn_ref, kdt_ref, gct_ref, o_ref, st_ref, s_ref, *, tt, nb):
    t = pl.program_id(0)

    @pl.when(t == 0)
    def _():
        s_ref[...] = jnp.zeros_like(s_ref)

    for ci in range(tt // GDN_CHUNK):
        rs = slice(ci * GDN_CHUNK, (ci + 1) * GDN_CHUNK)
        items = [(b, h) for b in range(nb) for h in range(N_HEADS)]
        cs = lambda h: slice(h * HEAD_DIM, (h + 1) * HEAD_DIM)
        us = [u_ref[b, rs, :] for b in range(nb)]
        wks = [wk_ref[b, rs, :] for b in range(nb)]
        qes = [qe_ref[b, rs, :] for b in range(nb)]
        attns = [attn_ref[b, rs, :] for b in range(nb)]
        kdts = [kdt_ref[b, :, rs] for b in range(nb)]
        decays = [jnp.exp(gct_ref[b, :, rs][:, GDN_CHUNK - 1:GDN_CHUNK]) for b in range(nb)]
        ss = [s_ref[b, h] for b, h in items]
        sbs = [s.astype(BF16) for s in ss]
        v_news = [us[b][:, cs(h)] - _dot(wks[b][:, cs(h)], sb) for (b, h), sb in zip(items, sbs)]
        vbs = [vn.astype(BF16) for vn in v_news]
        for (b, h), s, vb in zip(items, ss, vbs):
            a = decays[b][LANE_GDEC + h:LANE_GDEC + h + 1, :]
            s_ref[b, h] = a * s + _dot(kdts[b][cs(h), :], vb)
        for b in range(nb):
            outs = [_dot(qes[b][:, cs(h)], sbs[b * N_HEADS + h])
                    + _dot(attns[b][:, h * GDN_CHUNK:(h + 1) * GDN_CHUNK], vbs[b * N_HEADS + h])
                    for h in range(N_HEADS)]
            o_ref[b, rs, :] = jnp.concatenate(outs, axis=1)

    @pl.when(t == pl.num_programs(0) - 1)
    def _():
        st_ref[...] = s_ref[...]


def _gdn_prompt(dqkv, conv_w, small, small_t, *, tt):
    b, t, _ = dqkv.shape
    n_attn = N_HEADS * GDN_CHUNK
    by_rows = lambda w: pl.BlockSpec((1, tt, w), lambda i, j: (i, j, 0))
    by_cols = lambda r: pl.BlockSpec((1, r, tt), lambda i, j: (i, 0, j))
    u, wk, qe, attn, kdt, gct = pl.pallas_call(
        functools.partial(_gdn_prep_kernel, tt=tt), grid=(b, t // tt),
        in_specs=[by_rows(3 * GROUP_WIDTH), pl.BlockSpec(conv_w.shape, lambda i, j: (0, 0)),
                  by_rows(SMALL_WIDTH), by_cols(SMALL_ROWS_T)],
        out_specs=[by_rows(GROUP_WIDTH), by_rows(GROUP_WIDTH), by_rows(GROUP_WIDTH), by_rows(n_attn),
                   by_cols(GROUP_WIDTH), by_cols(SMALL_ROWS_T)],
        out_shape=[jax.ShapeDtypeStruct((b, t, GROUP_WIDTH), F32), jax.ShapeDtypeStruct((b, t, GROUP_WIDTH), BF16),
                   jax.ShapeDtypeStruct((b, t, GROUP_WIDTH), BF16), jax.ShapeDtypeStruct((b, t, n_attn), BF16),
                   jax.ShapeDtypeStruct((b, GROUP_WIDTH, t), BF16), jax.ShapeDtypeStruct((b, SMALL_ROWS_T, t), F32)],
        scratch_shapes=[pltpu.VMEM((tt + 8, 3 * GROUP_WIDTH), F32)],
        compiler_params=pltpu.CompilerParams(dimension_semantics=("arbitrary", "arbitrary"),
                                             vmem_limit_bytes=VMEM_LIMIT),
        name="gdn_prep",
    )(dqkv, conv_w, small, small_t)
    all_rows = lambda w: pl.BlockSpec((b, tt, w), lambda j: (0, j, 0))
    all_cols = lambda r: pl.BlockSpec((b, r, tt), lambda j: (0, 0, j))
    st_spec = pl.BlockSpec((b, N_HEADS, HEAD_DIM, HEAD_DIM), lambda j: (0, 0, 0, 0))
    return pl.pallas_call(
        functools.partial(_gdn_scan_kernel, tt=tt, nb=b), grid=(t // tt,),
        in_specs=[all_rows(GROUP_WIDTH), all_rows(GROUP_WIDTH), all_rows(GROUP_WIDTH), all_rows(n_attn),
                  all_cols(GROUP_WIDTH), all_cols(SMALL_ROWS_T)],
        out_specs=[all_rows(GROUP_WIDTH), st_spec],
        out_shape=[jax.ShapeDtypeStruct((b, t, GROUP_WIDTH), F32),
                   jax.ShapeDtypeStruct((b, N_HEADS, HEAD_DIM, HEAD_DIM), F32)],
        scratch_shapes=[pltpu.VMEM((b, N_HEADS, HEAD_DIM, HEAD_DIM), F32)],
        compiler_params=pltpu.CompilerParams(dimension_semantics=("arbitrary",), vmem_limit_bytes=VMEM_LIMIT),
        name="gdn_scan",
    )(u, wk, qe, attn, kdt, gct)


def _outproj_kernel(x_ref, ohg_ref, hg_ref, ofx_ref, osb_ref, odn_ref, dz_ref, gains_ref, w_ref, y_ref):
    ones = _head_ones()
    parts = [
        _head_rms(ohg_ref[...], ones, gains_ref[0:1, :]) * hg_ref[...],
        _head_rms(ofx_ref[...], ones, gains_ref[1:2, :]),
        _head_rms(osb_ref[...], ones, gains_ref[2:3, :]),
        _head_rms(odn_ref[...], ones, gains_ref[3:4, :]) * dz_ref[...],
    ]
    y = x_ref[...]
    for gidx, p in enumerate(parts):
        y = y + _dot(p.astype(BF16), w_ref[gidx * GROUP_WIDTH:(gidx + 1) * GROUP_WIDTH, :])
    y_ref[...] = y


def _outproj(x2, ohg, hgate, ofx, osb, odn, dz, gains, w_out, *, tm):
    m, d = x2.shape
    row = lambda i: (i, 0)
    const = lambda i: (0, 0)
    seg = pl.BlockSpec((tm, GROUP_WIDTH), row)
    return pl.pallas_call(
        _outproj_kernel, grid=(m // tm,),
        in_specs=[pl.BlockSpec((tm, d), row)] + [seg] * 6 + [pl.BlockSpec(gains.shape, const),
                                                            pl.BlockSpec(w_out.shape, const)],
        out_specs=pl.BlockSpec((tm, d), row),
        out_shape=jax.ShapeDtypeStruct((m, d), F32),
        compiler_params=pltpu.CompilerParams(dimension_semantics=("arbitrary",), vmem_limit_bytes=VMEM_LIMIT),
        name="outproj",
    )(x2, ohg, hgate, ofx, osb, odn, dz, gains, w_out)


def _mlp_kernel(x_ref, g2_ref, wu_ref, wd_ref, y_ref, h_ref, acc_ref):
    f = pl.program_id(1)

    @pl.when(f == 0)
    def _():
        x = x_ref[...]
        h_ref[...] = (x * lax.rsqrt(jnp.mean(x * x, axis=-1, keepdims=True) + NORM_EPS) * g2_ref[...]).astype(BF16)
        acc_ref[...] = x

    u = jnp.maximum(_dot(h_ref[...], wu_ref[...]), 0.0)
    acc_ref[...] += _dot((u * u).astype(BF16), wd_ref[...])

    @pl.when(f == pl.num_programs(1) - 1)
    def _():
        y_ref[...] = acc_ref[...]


def _mlp(x2, g2, w_up, w_down, *, tm, tf):
    m, d = x2.shape
    dff = w_up.shape[1]
    return pl.pallas_call(
        _mlp_kernel, grid=(m // tm, dff // tf),
        in_specs=[pl.BlockSpec((tm, d), lambda i, f: (i, 0)), pl.BlockSpec((1, d), lambda i, f: (0, 0)),
                  pl.BlockSpec((d, tf), lambda i, f: (0, f)), pl.BlockSpec((tf, d), lambda i, f: (f, 0))],
        out_specs=pl.BlockSpec((tm, d), lambda i, f: (i, 0)),
        out_shape=jax.ShapeDtypeStruct((m, d), F32),
        scratch_shapes=[pltpu.VMEM((tm, d), BF16), pltpu.VMEM((tm, d), F32)],
        compiler_params=pltpu.CompilerParams(dimension_semantics=("arbitrary", "arbitrary"),
                                             vmem_limit_bytes=VMEM_LIMIT),
        name="mlp",
    )(x2, g2, w_up, w_down)


PAGES_PER_STEP = 16
SEQS_PER_STEP = 4


def _head_rows(row):
    x = jnp.broadcast_to(row, (8, GROUP_WIDTH))
    return jnp.where(_iota((8, GROUP_WIDTH), 1) // HEAD_DIM == _iota((8, GROUP_WIDTH), 0), x, 0.0)


def _dec_attn_kernel(pt_ref, q_ref, kn_ref, vn_ref, sn_ref, *refs, fox, ns, pp, page):
    n_refs = ns * pp
    k_refs = refs[:n_refs]
    v_refs = refs[n_refs:2 * n_refs]
    rest = refs[2 * n_refs:]
    if fox:
        lf_refs = rest[:n_refs]
        rest = rest[n_refs:]
    o_ref, m_ref, l_ref, acc_ref, carry_ref = rest
    j = pl.program_id(1)
    seqs = range(ns)
    qbs = [_head_rows(q_ref[s]).astype(BF16) for s in seqs]
    r = _iota((page, page), 0)
    c = _iota((page, page), 1)
    later = jnp.where(r > c, 1.0, 0.0).astype(BF16)

    @pl.when(j == 0)
    def _():
        if fox:
            l_ref[...] = jnp.ones_like(l_ref)
            lane = _iota((8, SMALL_WIDTH), 1)
            row = _iota((8, SMALL_WIDTH), 0)
            for s in seqs:
                kn = jnp.broadcast_to(kn_ref[s], (8, GROUP_WIDTH)).astype(BF16)
                m_ref[s] = jnp.broadcast_to(_dot_nt(qbs[s], kn)[:, 0:1], m_ref.shape[1:])
                acc_ref[s] = jnp.broadcast_to(vn_ref[s], acc_ref.shape[1:]).astype(BF16).astype(F32)
                sn = jnp.broadcast_to(sn_ref[s], (8, SMALL_WIDTH))
                lf_new = jnp.sum(jnp.where(lane == row + LANE_FLOG, sn, 0.0), axis=-1, keepdims=True)
                carry_ref[s] = jnp.broadcast_to(jnp.where(_iota((8, 1), 0) < N_HEADS, lf_new, 0.0),
                                                carry_ref.shape[1:])
        else:
            acc_ref[...] = jnp.zeros_like(acc_ref)
            carry_ref[...] = jnp.zeros_like(carry_ref)

    g8 = pp * 8
    n8 = ns * g8
    per_page = lambda vals: jnp.concatenate([v for s in seqs for v in [vals[s]] * pp], axis=0)
    z = jnp.concatenate([_dot(qbs[s], k_refs[s * pp + i][0].astype(BF16)) for s in seqs for i in range(pp)], axis=0)
    if fox:
        pad = jnp.zeros((8 - N_HEADS, page), F32)
        x = jnp.concatenate([a for i in range(n_refs) for a in (lf_refs[i][0], pad)], axis=0)
    else:
        x = _log_sigmoid(-z)
    ri = _iota((n8, n8), 0)
    ci = _iota((n8, n8), 1)
    before = jnp.where((ri % 8 == ci % 8) & (ri // g8 == ci // g8) & (ci // 8 < ri // 8), 1.0, 0.0).astype(BF16)
    tot = _dot_xc(x, jnp.ones((page, page), BF16))
    upto = _dot_cx(before, tot) + per_page([carry_ref[s] for s in seqs])
    bias = _dot_xc(x, later) + upto
    new_carry = upto + tot
    for s in seqs:
        carry_ref[s] = new_carry[(s + 1) * g8 - 8:(s + 1) * g8, :]
    if fox:
        sc = z + bias
        m_prev = m_ref[...]
        m_new = jnp.maximum(m_prev, jnp.max(jnp.max(sc.reshape(ns, pp, 8, page), axis=1), axis=-1, keepdims=True))
        alpha = jnp.exp(m_prev - m_new)
        p = jnp.exp(sc - per_page([m_new[s] for s in seqs]))
        l_ref[...] = alpha * l_ref[...] + jnp.sum(jnp.sum(p.reshape(ns, pp, 8, page), axis=1), axis=-1, keepdims=True)
        m_ref[...] = m_new
    else:
        p = jnp.exp(z + x + bias)
    for s in seqs:
        pv = None
        for i in range(pp):
            row0 = (s * pp + i) * 8
            term = _dot_nt(p[row0:row0 + 8].astype(BF16), v_refs[s * pp + i][0].astype(BF16))
            pv = term if pv is None else pv + term
        if fox:
            acc_ref[s] = alpha[s][:, 0:1] * acc_ref[s] + pv
        else:
            acc_ref[s] = acc_ref[s] + pv

    @pl.when(j == pl.num_programs(1) - 1)
    def _():
        own = _iota((8, GROUP_WIDTH), 1) // HEAD_DIM == _iota((8, GROUP_WIDTH), 0)
        for s in seqs:
            acc = acc_ref[s]
            if fox:
                acc = acc / l_ref[s][:, 0:1]
            o_ref[s] = jnp.sum(jnp.where(own, acc, 0.0), axis=0, keepdims=True)


def _dec_attn(page_table, q, k_new, v_new, small, cache_k, cache_v, cache_lf_t, *, layer, fox):
    nb = q.shape[0]
    n_pages = page_table.shape[1]
    page = cache_k.shape[3]
    assert page == 128, "per-page statistics are kept one page per vreg row group"
    pp = _pick(n_pages, (PAGES_PER_STEP, 8, 4, 2, 1))
    ns = _pick(nb, (SEQS_PER_STEP, 1))
    row3 = lambda a: a.reshape(nb, 1, a.shape[-1])
    rspec = lambda w: pl.BlockSpec((ns, 1, w), lambda b, j, pt: (b, 0, 0))

    def page_map(s, i):
        return lambda b, j, pt: (layer, pt[b * ns + s, n_pages - 1 - (j * pp + i)], 0, 0)

    slots = [(s, i) for s in range(ns) for i in range(pp)]
    in_specs = [rspec(GROUP_WIDTH)] * 3 + [rspec(SMALL_WIDTH)]
    in_specs += [pl.BlockSpec((None, 1, GROUP_WIDTH, page), page_map(s, i)) for s, i in slots] * 2
    args = [row3(q), row3(k_new), row3(v_new), row3(small)] + [cache_k] * len(slots) + [cache_v] * len(slots)
    if fox:
        in_specs += [pl.BlockSpec((None, 1, N_HEADS, page), page_map(s, i)) for s, i in slots]
        args += [cache_lf_t] * len(slots)
    grid_spec = pltpu.PrefetchScalarGridSpec(
        num_scalar_prefetch=1, grid=(nb // ns, n_pages // pp), in_specs=in_specs,
        out_specs=pl.BlockSpec((ns, 1, GROUP_WIDTH), lambda b, j, pt: (b, 0, 0)),
        scratch_shapes=[pltpu.VMEM((ns, 8, 128), F32), pltpu.VMEM((ns, 8, 128), F32),
                        pltpu.VMEM((ns, 8, GROUP_WIDTH), F32), pltpu.VMEM((ns, 8, 128), F32)])
    out = pl.pallas_call(
        functools.partial(_dec_attn_kernel, fox=fox, ns=ns, pp=pp, page=page), grid_spec=grid_spec,
        out_shape=jax.ShapeDtypeStruct((nb, 1, GROUP_WIDTH), F32),
        compiler_params=pltpu.CompilerParams(dimension_semantics=("arbitrary", "arbitrary"),
                                             vmem_limit_bytes=VMEM_LIMIT),
        name="fox_step" if fox else "sb_step",
    )(page_table, *args)
    return out.reshape(nb, GROUP_WIDTH)


def _column(row, eye):
    return jnp.sum(eye * row, axis=1, keepdims=True)


def _rec_step_kernel(hq_ref, hlf_ref, hk_ref, hi_ref, dx_ref, sm_ref, w_ref, shg_ref, sdn_ref, buf_ref,
                     ohg_ref, odn_ref, shg_o_ref, sdn_o_ref, buf_o_ref):
    eye = jnp.where(_iota((HEAD_DIM, HEAD_DIM), 0) == _iota((HEAD_DIM, HEAD_DIM), 1), 1.0, 0.0)
    hq = hq_ref[0]
    hlf = hlf_ref[0]
    hk = hk_ref[0]
    hv = hi_ref[0]
    sm = sm_ref[0]
    buf = buf_ref[0]
    x_new = dx_ref[0]
    y = x_new * w_ref[CONV_WIDTH - 1:CONV_WIDTH, :]
    for jw in range(CONV_WIDTH - 1):
        y = y + buf[jw:jw + 1, :] * w_ref[jw:jw + 1, :]
    y = _silu(y)
    buf_o_ref[0] = jnp.concatenate([buf[1:CONV_WIDTH - 1, :], x_new], axis=0)
    for h in range(N_HEADS):
        cs = slice(h * HEAD_DIM, (h + 1) * HEAD_DIM)
        s = shg_ref[0, h]
        s = _column(jnp.exp(hlf[:, cs]), eye) * s + _column(hk[:, cs], eye) * hv[:, cs]
        shg_o_ref[0, h] = s
        ohg_ref[0, :, cs] = jnp.sum(_column(hq[:, cs], eye) * s, axis=0, keepdims=True)
        q = y[:, cs]
        k = y[:, GROUP_WIDTH + h * HEAD_DIM:GROUP_WIDTH + (h + 1) * HEAD_DIM]
        v = y[:, 2 * GROUP_WIDTH + h * HEAD_DIM:2 * GROUP_WIDTH + (h + 1) * HEAD_DIM]
        q = q * lax.rsqrt(jnp.sum(q * q, axis=-1, keepdims=True) + NORM_EPS) * QK_SCALE
        k = k * lax.rsqrt(jnp.sum(k * k, axis=-1, keepdims=True) + NORM_EPS)
        beta = sm[:, LANE_BETA + h:LANE_BETA + h + 1]
        a = jnp.exp(sm[:, LANE_GDEC + h:LANE_GDEC + h + 1])
        s = sdn_ref[0, h]
        kc = _column(k, eye)
        v_new = beta * (v - a * jnp.sum(kc * s, axis=0, keepdims=True))
        s = a * s + kc * v_new
        sdn_o_ref[0, h] = s
        odn_ref[0, :, cs] = jnp.sum(_column(q, eye) * s, axis=0, keepdims=True)


def _rec_step(hq, hlf, hk, hi, dqkv, small, conv_w, s_hg, s_dn, buf):
    nb = hq.shape[0]
    row3 = lambda a: a.reshape(nb, 1, a.shape[-1])
    rspec = lambda w: pl.BlockSpec((1, 1, w), lambda b: (b, 0, 0))
    st_spec = pl.BlockSpec((1, N_HEADS, HEAD_DIM, HEAD_DIM), lambda b: (b, 0, 0, 0))
    buf_spec = pl.BlockSpec((1, CONV_WIDTH - 1, 3 * GROUP_WIDTH), lambda b: (b, 0, 0))
    outs = pl.pallas_call(
        _rec_step_kernel, grid=(nb,),
        in_specs=[rspec(GROUP_WIDTH)] * 4 + [rspec(3 * GROUP_WIDTH), rspec(SMALL_WIDTH),
                                             pl.BlockSpec(conv_w.shape, lambda b: (0, 0)), st_spec, st_spec, buf_spec],
        out_specs=[rspec(GROUP_WIDTH), rspec(GROUP_WIDTH), st_spec, st_spec, buf_spec],
        out_shape=[jax.ShapeDtypeStruct((nb, 1, GROUP_WIDTH), F32)] * 2
        + [jax.ShapeDtypeStruct(s_hg.shape, F32), jax.ShapeDtypeStruct(s_dn.shape, F32),
           jax.ShapeDtypeStruct(buf.shape, F32)],
        compiler_params=pltpu.CompilerParams(dimension_semantics=("arbitrary",), vmem_limit_bytes=VMEM_LIMIT),
        name="recurrent_step",
    )(row3(hq), row3(hlf), row3(hk), row3(hi), row3(dqkv), row3(small), conv_w, s_hg, s_dn, buf)
    ohg, odn, s_hg_new, s_dn_new, buf_new = outs
    return ohg.reshape(nb, GROUP_WIDTH), odn.reshape(nb, GROUP_WIDTH), s_hg_new, s_dn_new, buf_new


def _tile_gain(g):
    return jnp.tile(g.astype(F32), N_HEADS)


def _relayout_w_in(w_in_t_l):
    gw = GROUP_WIDTH
    a = 7 * gw
    e = a + N_HEADS + 7 * gw
    pad = jnp.zeros((SMALL_WIDTH - 3 * N_HEADS, w_in_t_l.shape[1]), w_in_t_l.dtype)
    rows = [w_in_t_l[:a], w_in_t_l[a + N_HEADS:e], w_in_t_l[a:a + N_HEADS], w_in_t_l[e:e + 2 * N_HEADS], pad]
    return jnp.concatenate(rows, axis=0).astype(BF16)


def _small_params(f_bias, dt_bias, a_log):
    sp = jnp.zeros((8, SMALL_WIDTH), F32)
    sp = sp.at[0, LANE_FLOG:LANE_FLOG + N_HEADS].set(f_bias.astype(F32))
    sp = sp.at[1, LANE_GDEC:LANE_GDEC + N_HEADS].set(dt_bias.astype(F32))
    sp = sp.at[2, LANE_GDEC:LANE_GDEC + N_HEADS].set(a_log.astype(F32))
    return sp


def _pick(n, candidates):
    for c in candidates:
        if n % c == 0:
            return c
    return n


def kernel(x_prompt, x_sample, cache_fox_k, cache_fox_v, cache_fox_logf, cache_sb_k, cache_sb_v, state_hgrn, state_dn, state_dn_conv, page_table, hgrn_lb_param, w_in, w_out, ln1_g, ln2_g, fox_f_bias, fox_q_norm, fox_k_norm, sb_q_norm, sb_k_norm, hgrn_out_norm, fox_out_norm, sb_out_norm, dn_out_norm, dn_conv_w, dn_dt_bias, dn_a_log, w_up, w_down):
    depth = w_in.shape[0]
    bsz, seq, d = x_prompt.shape
    nb = x_sample.shape[0]
    n_phys, page = cache_fox_k.shape[1], cache_fox_k.shape[2]
    m = bsz * seq
    tq = _pick(seq, (512, 256, 128))
    tm = _pick(tq, (256, 128))
    tm_mlp = _pick(m, (512, 256, 128, 64, 32, 16, 8))
    tf = _pick(w_up.shape[2], (1024, 512, 256, 128))
    tt = _pick(seq, (256, 128, 64))

    yp = x_prompt.reshape(m, d)
    ys = x_sample.reshape(nb, d)
    lbp = hgrn_lb_param.astype(F32)
    w_in_t = jnp.transpose(w_in, (2, 0, 1))
    kv_t = lambda a: a.transpose(0, 1, 3, 4, 2).reshape(depth, n_phys, GROUP_WIDTH, page)
    lf_t = jnp.swapaxes(cache_fox_logf.astype(F32), 2, 3)
    p_out = [[] for _ in range(8)]
    s_out = [[] for _ in range(8)]
    for l in range(depth):
        w_re = _relayout_w_in(w_in_t[:, l, :])
        w_o = w_out[l].astype(BF16)
        w_u = w_up[l].astype(BF16)
        w_d = w_down[l].astype(BF16)
        g1 = ln1_g[l].reshape(1, d).astype(F32)
        g2 = ln2_g[l].reshape(1, d).astype(F32)
        qk_gains = jnp.stack([_tile_gain(fox_q_norm[l]), _tile_gain(fox_k_norm[l]),
                              _tile_gain(sb_q_norm[l]), _tile_gain(sb_k_norm[l])])
        out_gains = jnp.stack([_tile_gain(hgrn_out_norm[l]), _tile_gain(fox_out_norm[l]),
                               _tile_gain(sb_out_norm[l]), _tile_gain(dn_out_norm[l])])
        sp = _small_params(fox_f_bias[l], dn_dt_bias[l], dn_a_log[l])
        conv_w = dn_conv_w[l].astype(F32)

        (hq, hlf, hk, hi, hgate, fq, fk, fv, sq, sk, sv, dqkv, dz, small, cum, small_t, cum_t,
         fk_blk, fv_blk, sk_blk, sv_blk) = _inproj(
            yp, g1, w_re, lbp, qk_gains, sp, layer=l, tm=tm, rows_per_seq=seq, with_time=True, key_block=tq)
        b3 = lambda a: a.reshape(bsz, seq, a.shape[-1])
        o_hg, st_hg = _hgrn_prompt(b3(hq), b3(hlf), b3(hk), b3(hi), tt=tt)
        o_fx = _fox_prompt(b3(fq), fk_blk, fv_blk, b3(cum), cum_t, tq=tq)
        o_sb = _sb_prompt(b3(sq), sk_blk, sv_blk, tq=tq)
        o_dn, st_dn = _gdn_prompt(b3(dqkv), conv_w, b3(small), small_t, tt=tt)
        yp = _outproj(yp, o_hg.reshape(m, -1), hgate, o_fx.reshape(m, -1), o_sb.reshape(m, -1),
                      o_dn.reshape(m, -1), dz, out_gains, w_o, tm=tm)
        yp = _mlp(yp, g2, w_u, w_d, tm=tm_mlp, tf=tf)
        by_head = lambda a: a.reshape(bsz, N_HEADS, HEAD_DIM, seq).transpose(0, 3, 1, 2)
        p_out[0].append(by_head(fk))
        p_out[1].append(by_head(fv))
        p_out[2].append(b3(small)[:, :, LANE_FLOG:LANE_FLOG + N_HEADS])
        p_out[3].append(by_head(sk))
        p_out[4].append(by_head(sv))
        p_out[5].append(jnp.swapaxes(st_hg, -1, -2))
        p_out[6].append(st_dn)
        p_out[7].append(b3(dqkv)[:, seq - (CONV_WIDTH - 1):, :])

        (hq, hlf, hk, hi, hgate, fq, fk, fv, sq, sk, sv, dqkv, dz, small) = _inproj(
            ys, g1, w_re, lbp, qk_gains, sp, layer=l, tm=nb, rows_per_seq=nb, with_time=False)
        o_fx = _dec_attn(page_table, fq, fk, fv, small, kv_t(cache_fox_k), kv_t(cache_fox_v), lf_t,
                         layer=l, fox=True)
        o_sb = _dec_attn(page_table, sq, sk, sv, small, kv_t(cache_sb_k), kv_t(cache_sb_v), None,
                         layer=l, fox=False)
        o_hg, o_dn, s_hg_new, s_dn_new, buf_new = _rec_step(
            hq, hlf, hk, hi, dqkv, small, conv_w, state_hgrn[l].astype(F32), state_dn[l].astype(F32),
            state_dn_conv[l].astype(F32))
        ys = _outproj(ys, o_hg, hgate, o_fx, o_sb, o_dn, dz, out_gains, w_o, tm=nb)
        ys = _mlp(ys, g2, w_u, w_d, tm=nb, tf=tf)
        sshape = (nb, 1, N_HEADS, HEAD_DIM)
        s_out[0].append(fk.reshape(sshape))
        s_out[1].append(fv.reshape(sshape))
        s_out[2].append(small[:, LANE_FLOG:LANE_FLOG + N_HEADS].reshape(nb, 1, N_HEADS))
        s_out[3].append(sk.reshape(sshape))
        s_out[4].append(sv.reshape(sshape))
        s_out[5].append(s_hg_new)
        s_out[6].append(s_dn_new)
        s_out[7].append(buf_new)

    p = [jnp.stack(v) for v in p_out]
    s = [jnp.stack(v) for v in s_out]
    return (yp.reshape(bsz, seq, d), ys.reshape(nb, 1, d), *p, *s)
```

```python
import functools

import jax
import jax.numpy as jnp
from jax import lax
from jax.experimental import pallas as pl
from jax.experimental.pallas import tpu as pltpu

F32 = jnp.float32
BF16 = jnp.bfloat16

HEAD_DIM = 64
N_HEADS = 4
GROUP_WIDTH = N_HEADS * HEAD_DIM
N_SEGMENTS = 14
SMALL_WIDTH = 128
CONV_WIDTH = 4
NORM_EPS = 1e-6
NEG_BIG = -1e30
QK_SCALE = HEAD_DIM ** -0.5
LOG2E = 1.4426950408889634
INV_LN2 = LOG2E
PRUNE_LOG2 = 160.0
BOUND_SLACK = 1.001
VMEM_LIMIT = 56 * 1024 * 1024

LANE_FLOG = 0
LANE_BETA = 4
LANE_GDEC = 8
SMALL_ROWS_T = 16


def _iota(shape, dim):
    return lax.broadcasted_iota(jnp.int32, shape, dim)


def _dot(a, b):
    return jnp.dot(a, b, preferred_element_type=F32)


def _dot_nt(a, b):
    return lax.dot_general(a, b, (((1,), (1,)), ((), ())), preferred_element_type=F32)


def _split3(x):
    hi = x.astype(BF16)
    r = x - hi.astype(F32)
    mid = r.astype(BF16)
    lo = (r - mid.astype(F32)).astype(BF16)
    return hi, mid, lo


def _dot_xc(x, c, parts=3):
    ps = _split3(x)[:parts]
    out = _dot(ps[0], c)
    for p in ps[1:]:
        out = out + _dot(p, c)
    return out


def _dot_cx(c, x, parts=3):
    ps = _split3(x)[:parts]
    out = _dot(c, ps[0])
    for p in ps[1:]:
        out = out + _dot(c, p)
    return out


def _dot_f32(a, b):
    ah = a.astype(BF16)
    al = (a - ah.astype(F32)).astype(BF16)
    bh = b.astype(BF16)
    bl = (b - bh.astype(F32)).astype(BF16)
    return _dot(ah, bh) + _dot(ah, bl) + _dot(al, bh)


def _head_ones(n=GROUP_WIDTH):
    return (_iota((n, n), 0) // HEAD_DIM == _iota((n, n), 1) // HEAD_DIM).astype(BF16)


def _head_sum(x, ones):
    return _dot_xc(x, ones)


def _head_rms(x, ones, gain):
    ms = _head_sum(x * x, ones) * (1.0 / HEAD_DIM)
    return x * lax.rsqrt(ms + NORM_EPS) * gain


def _log_sigmoid(x):
    return jnp.minimum(x, 0.0) - jnp.log1p(jnp.exp(-jnp.abs(x)))


def _softplus(x):
    return jnp.maximum(x, 0.0) + jnp.log1p(jnp.exp(-jnp.abs(x)))


def _sigmoid(x):
    return 1.0 / (1.0 + jnp.exp(-x))


def _silu(x):
    return x * _sigmoid(x)


def _max_all(x):
    return jnp.max(jnp.max(x, axis=0, keepdims=True), axis=1, keepdims=True)


def _lane_pack(vals):
    lane = _iota((1, 128), 1)
    out = jnp.zeros((1, 128), F32)
    for i, v in enumerate(vals):
        out = jnp.where(lane == i, v, out)
    return out


def _any_head(mask):
    lane = _iota(mask.shape, 1)
    return jnp.max(jnp.where(mask & (lane < N_HEADS), 1.0, 0.0)) > 0.0


def _inproj_kernel(x_ref, g1_ref, w_ref, lbp_ref, gains_ref, sp_ref, *refs, layer, depth, tiles_per_seq,
                   with_time):
    (hq_ref, hlf_ref, hk_ref, hi_ref, hg_ref, fq_ref, fk_ref, fv_ref, sq_ref, sk_ref, sv_ref,
     dqkv_ref, dz_ref, small_ref) = refs[:14]
    x = x_ref[...]
    h = (x * lax.rsqrt(jnp.mean(x * x, axis=-1, keepdims=True) + NORM_EPS) * g1_ref[...]).astype(BF16)

    def seg(j, width=GROUP_WIDTH):
        return _dot_nt(h, w_ref[j * GROUP_WIDTH:j * GROUP_WIDTH + width, :])

    ones = _head_ones()

    rows = [lbp_ref[i:i + 1, :] for i in range(depth)]
    mx = functools.reduce(jnp.maximum, rows)
    es = [jnp.exp(r - mx) for r in rows]
    lb = sum(es[1:layer + 1], jnp.zeros_like(mx)) / sum(es)
    hq_ref[...] = seg(0)
    hf = seg(1)
    a = jnp.log(lb)
    b = jnp.log1p(-lb) + _log_sigmoid(hf)
    hi = jnp.maximum(a, b)
    lo = jnp.minimum(a, b)
    hlf_ref[...] = hi + jnp.log1p(jnp.exp(lo - hi))
    hk_ref[...] = (1.0 - lb) * _sigmoid(-hf)
    hi_ref[...] = seg(2)
    hg_ref[...] = _silu(seg(3))

    fq_ref[...] = _head_rms(seg(4), ones, gains_ref[0:1, :]) * QK_SCALE
    sq_ref[...] = _head_rms(seg(7), ones, gains_ref[2:3, :]) * QK_SCALE
    kv = (_head_rms(seg(5), ones, gains_ref[1:2, :]), seg(6), _head_rms(seg(8), ones, gains_ref[3:4, :]), seg(9))
    for idx, (ref, val) in enumerate(zip((fk_ref, fv_ref, sk_ref, sv_ref), kv)):
        if with_time:
            val_t = val.T
            ref[0] = val_t
            refs[17 + idx][0, 0] = val_t.astype(BF16)
        else:
            ref[...] = val

    dqkv_ref[...] = seg(10, 3 * GROUP_WIDTH)
    dz_ref[...] = _silu(seg(13))

    s = _dot_nt(h, w_ref[N_SEGMENTS * GROUP_WIDTH:, :])
    lane = _iota(s.shape, 1)
    f_log = _log_sigmoid(s + sp_ref[0:1, :])
    beta = _sigmoid(s)
    g_dec = -jnp.exp(sp_ref[2:3, :]) * _softplus(s + sp_ref[1:2, :])
    small = jnp.where(lane < LANE_BETA, f_log, jnp.where(lane < LANE_GDEC, beta, g_dec))
    small_ref[...] = small

    if with_time:
        cum_ref, small_t_ref, cum_t_ref = refs[14:17]
        carry_ref = refs[21]
        tm = s.shape[0]

        @pl.when(pl.program_id(0) % tiles_per_seq == 0)
        def _():
            carry_ref[...] = jnp.zeros_like(carry_ref)

        tril = (_iota((tm, tm), 1) <= _iota((tm, tm), 0)).astype(BF16)
        cum = _dot_cx(tril, small) + carry_ref[0:1, :]
        cum_ref[...] = cum
        carry_ref[...] = jnp.broadcast_to(cum[tm - 1:tm, :], carry_ref.shape)
        small_t_ref[0] = small.T[:SMALL_ROWS_T, :]
        cum_t_ref[0] = cum.T[:SMALL_ROWS_T, :]


def _inproj(x2, g1, w_re, lbp, gains, sp, *, layer, tm, rows_per_seq, with_time, key_block=None):
    m, d = x2.shape
    depth = lbp.shape[0]
    grid = (m // tm,)
    row = lambda i: (i, 0)
    const = lambda i: (0, 0)
    tps = rows_per_seq // tm
    nseq = m // rows_per_seq
    seg_shape = jax.ShapeDtypeStruct((m, GROUP_WIDTH), F32)
    seg_spec = pl.BlockSpec((tm, GROUP_WIDTH), row)
    out_shape = [seg_shape] * 11 + [jax.ShapeDtypeStruct((m, 3 * GROUP_WIDTH), F32), seg_shape,
                                    jax.ShapeDtypeStruct((m, SMALL_WIDTH), F32)]
    out_specs = [seg_spec] * 11 + [pl.BlockSpec((tm, 3 * GROUP_WIDTH), row), seg_spec,
                                   pl.BlockSpec((tm, SMALL_WIDTH), row)]
    scratch = []
    if with_time:
        def by_time(rows):
            return (jax.ShapeDtypeStruct((nseq, rows, rows_per_seq), F32),
                    pl.BlockSpec((1, rows, tm), lambda i: (i // tps, 0, i % tps)))
        for idx in (6, 7, 9, 10):
            out_shape[idx], out_specs[idx] = by_time(GROUP_WIDTH)
        out_shape += [jax.ShapeDtypeStruct((m, SMALL_WIDTH), F32)]
        out_specs += [pl.BlockSpec((tm, SMALL_WIDTH), row)]
        for _ in range(2):
            sh, sp_ = by_time(SMALL_ROWS_T)
            out_shape.append(sh)
            out_specs.append(sp_)
        per_blk = key_block // tm
        for _ in range(4):
            out_shape.append(jax.ShapeDtypeStruct((nseq, rows_per_seq // key_block, GROUP_WIDTH, key_block), BF16))
            out_specs.append(pl.BlockSpec((1, 1, GROUP_WIDTH, tm),
                                          lambda i: (i // tps, (i % tps) // per_blk, 0, (i % tps) % per_blk)))
        scratch = [pltpu.VMEM((8, SMALL_WIDTH), F32)]
    kern = functools.partial(_inproj_kernel, layer=layer, depth=depth, tiles_per_seq=tps,
                             with_time=with_time)
    return pl.pallas_call(
        kern, grid=grid,
        in_specs=[pl.BlockSpec((tm, d), row), pl.BlockSpec((1, d), const), pl.BlockSpec(w_re.shape, const),
                  pl.BlockSpec(lbp.shape, const), pl.BlockSpec(gains.shape, const), pl.BlockSpec(sp.shape, const)],
        out_specs=out_specs, out_shape=out_shape, scratch_shapes=scratch,
        compiler_params=pltpu.CompilerParams(dimension_semantics=("arbitrary",), vmem_limit_bytes=VMEM_LIMIT),
        name="inproj_time" if with_time else "inproj_step",
    )(x2, g1, w_re, lbp, gains, sp)


HGRN_SUB = 16


def _hgrn_kernel(q_ref, lf_ref, k_ref, v_ref, o_ref, st_ref, s_ref, oi_ref, *, tt):
    t = pl.program_id(1)

    @pl.when(t == 0)
    def _():
        s_ref[...] = jnp.zeros_like(s_ref)

    q = q_ref[0]
    lf = lf_ref[0]
    kin = k_ref[0]
    v = v_ref[0]
    r = _iota((tt, tt), 0)
    c = _iota((tt, tt), 1)
    same = (r // HGRN_SUB) == (c // HGRN_SUB)
    g = _dot_cx(jnp.where(same & (c <= r), 1.0, 0.0).astype(BF16), lf)
    gl = _dot_cx(jnp.where(same, 1.0, 0.0).astype(BF16), lf)
    qg = q * jnp.exp(g)
    kg = kin * jnp.exp(gl - g)

    ones = _head_ones()
    rowmod = _iota((tt, GROUP_WIDTH), 0) % HGRN_SUB
    o = jnp.zeros((tt, GROUP_WIDTH), F32)
    for d in range(HGRN_SUB):
        if d == 0:
            kd, gd, vd = kin, g, v
        else:
            kd = pltpu.roll(kin, d, 0)
            gd = pltpu.roll(g, d, 0)
            vd = pltpu.roll(v, d, 0)
        e = jnp.where(rowmod >= d, g - gd, NEG_BIG)
        p = q * kd * jnp.exp(e)
        o = o + _dot_xc(p, ones, parts=1) * vd

    v_t = v.T
    for i in range(tt // HGRN_SUB):
        rs = slice(i * HGRN_SUB, (i + 1) * HGRN_SUB)
        for h in range(N_HEADS):
            cs = slice(h * HEAD_DIM, (h + 1) * HEAD_DIM)
            s = s_ref[h]
            oi_ref[rs, cs] = _dot_nt(qg[rs, cs].astype(BF16), s.astype(BF16))
            dec = jnp.exp(gl[i * HGRN_SUB:i * HGRN_SUB + 1, cs])
            s_ref[h] = dec * s + _dot(v_t[cs, rs].astype(BF16), kg[rs, cs].astype(BF16))
    o_ref[0] = o + oi_ref[...]

    @pl.when(t == pl.num_programs(1) - 1)
    def _():
        st_ref[0] = s_ref[...]


def _hgrn_prompt(hq, hlf, hk, hi, *, tt):
    b, t, _ = hq.shape
    blk = pl.BlockSpec((1, tt, GROUP_WIDTH), lambda i, j: (i, j, 0))
    return pl.pallas_call(
        functools.partial(_hgrn_kernel, tt=tt), grid=(b, t // tt),
        in_specs=[blk] * 4,
        out_specs=[blk, pl.BlockSpec((1, N_HEADS, HEAD_DIM, HEAD_DIM), lambda i, j: (i, 0, 0, 0))],
        out_shape=[jax.ShapeDtypeStruct((b, t, GROUP_WIDTH), F32),
                   jax.ShapeDtypeStruct((b, N_HEADS, HEAD_DIM, HEAD_DIM), F32)],
        scratch_shapes=[pltpu.VMEM((N_HEADS, HEAD_DIM, HEAD_DIM), F32), pltpu.VMEM((tt, GROUP_WIDTH), F32)],
        compiler_params=pltpu.CompilerParams(dimension_semantics=("arbitrary", "arbitrary"),
                                             vmem_limit_bytes=VMEM_LIMIT),
        name="hgrn_prompt",
    )(hq, hlf, hk, hi)


FOX_WINDOW = 8


def _fox_kernel(q_ref, k_ref, v_ref, fq_ref, fk_ref, o_ref, *refs, tq, windowed):
    if windowed:
        more_ref, *refs = refs
    qa_ref, m_ref, acc_ref, qn_ref, fq_max_ref, m_min_ref, kn_ref, fk_min_ref = refs
    qi = pl.program_id(1)
    j = pl.program_id(2)
    last_j = jnp.minimum(qi, pl.num_programs(2) - 1)

    @pl.when((j == 0) & (qi == 0))
    def _():
        kn_ref[...] = jnp.zeros_like(kn_ref)
        fk_min_ref[...] = jnp.zeros_like(fk_min_ref)

    @pl.when(j == 0)
    def _():
        m_ref[...] = jnp.full_like(m_ref, NEG_BIG)
        acc_ref[...] = jnp.zeros_like(acc_ref)
        q = q_ref[0] * LOG2E
        f = fq_ref[0] * LOG2E
        lane = _iota((tq, HEAD_DIM), 1)
        for h in range(N_HEADS):
            hi, mid, lo = [p.astype(F32) for p in _split3(f[:, LANE_FLOG + h:LANE_FLOG + h + 1])]
            ext = jnp.where(lane == 0, hi, jnp.where(lane == 1, mid, jnp.where(lane == 2, lo,
                                                                               jnp.where(lane < 6, 1.0, 0.0))))
            qa_ref[h] = jnp.concatenate([q[:, h * HEAD_DIM:(h + 1) * HEAD_DIM], ext], axis=1).astype(BF16)

        heads = range(N_HEADS)
        rows = lambda h: slice(h * HEAD_DIM, (h + 1) * HEAD_DIM)
        qb = q.astype(BF16).astype(F32)
        qn2 = _head_sum(qb * qb, _head_ones())
        kf = k_ref[0, 0].astype(F32)
        k2 = kf * kf
        fk = fk_ref[0] * LOG2E
        qn_ref[...] = _lane_pack([_max_all(qn2[:, rows(h)]) for h in heads])
        fq_max_ref[...] = _lane_pack([_max_all(f[:, LANE_FLOG + h:LANE_FLOG + h + 1]) for h in heads])
        kn_ref[pl.ds(qi, 1), :] = _lane_pack([_max_all(jnp.sum(k2[rows(h), :], axis=0, keepdims=True))
                                              for h in heads])
        fk_min_ref[pl.ds(qi, 1), :] = _lane_pack([-_max_all(-fk[LANE_FLOG + h:LANE_FLOG + h + 1, :])
                                                  for h in heads])

    def may_matter(kn_rows, fk_min_rows):
        zcap2 = qn_ref[...] * kn_rows * (BOUND_SLACK * BOUND_SLACK)
        bias_cap = fq_max_ref[...] - fk_min_rows
        room = m_min_ref[...] - bias_cap - PRUNE_LOG2
        return (room < 0.0) | (zcap2 > room * room)

    def block_matters():
        return _any_head(may_matter(kn_ref[pl.ds(qi - j, 1), :], fk_min_ref[pl.ds(qi - j, 1), :]))

    def step(masked):
        kf = k_ref[0, 0]
        vf = v_ref[0, 0]
        fk = fk_ref[0] * LOG2E
        row = _iota((8, tq), 0)
        pad = jnp.zeros((HEAD_DIM - 8, tq), F32)
        v_ext = jnp.concatenate([jnp.where(row == 0, 1.0, 0.0), pad], axis=0).astype(BF16)
        if masked:
            keep = _iota((tq, tq), 1) <= _iota((tq, tq), 0)
        heads = range(N_HEADS)
        ss = []
        for h in heads:
            cs = slice(h * HEAD_DIM, (h + 1) * HEAD_DIM)
            hi, mid, lo = [p.astype(F32) for p in _split3(fk[LANE_FLOG + h:LANE_FLOG + h + 1, :])]
            k_ext = jnp.where(row < 3, 1.0, jnp.where(row == 3, -hi, jnp.where(row == 4, -mid,
                                                                               jnp.where(row == 5, -lo, 0.0))))
            ka = jnp.concatenate([kf[cs, :], jnp.concatenate([k_ext, pad], axis=0).astype(BF16)], axis=0)
            ss.append(_dot(qa_ref[h], ka))
        if masked:
            ss = [jnp.where(keep, s, NEG_BIG) for s in ss]
        m_prevs = [m_ref[h] for h in heads]
        m_news = [jnp.maximum(m_prevs[h], jnp.max(ss[h], axis=-1, keepdims=True)) for h in heads]
        ps = [jnp.exp2(ss[h] - jnp.concatenate([m_news[h]] * (tq // 128), axis=1)) for h in heads]
        for h in heads:
            va = jnp.concatenate([vf[h * HEAD_DIM:(h + 1) * HEAD_DIM, :], v_ext], axis=0)
            m_ref[h] = m_news[h]
            acc_ref[h] = jnp.exp2(m_prevs[h] - m_news[h]) * acc_ref[h] + _dot_nt(ps[h].astype(BF16), va)
        m_min_ref[...] = _lane_pack([-_max_all(-m_news[h]) for h in heads])

    @pl.when(j == 0)
    def _():
        step(True)

    @pl.when((j > 0) & (j <= qi))
    def _():
        @pl.when(block_matters())
        def _():
            step(False)

    @pl.when(j == last_j)
    def _():
        for h in range(N_HEADS):
            acc = acc_ref[h]
            o_ref[0, :, h * HEAD_DIM:(h + 1) * HEAD_DIM] = acc[:, :HEAD_DIM] / acc[:, HEAD_DIM:HEAD_DIM + 1]
        if windowed:
            older = _iota(kn_ref.shape, 0) <= qi - pl.num_programs(2)
            more = _any_head(may_matter(kn_ref[...], fk_min_ref[...]) & older)
            more_ref[0, 0] = jnp.where(more, jnp.ones((8, 128), F32), jnp.zeros((8, 128), F32))


def _fox_prompt(fq, fk, fv, cum, cum_t, *, tq):
    b, t, _ = fq.shape
    n = t // tq
    qspec = pl.BlockSpec((1, tq, GROUP_WIDTH), lambda i, qi, j: (i, qi, 0))
    kspec = pl.BlockSpec((1, 1, GROUP_WIDTH, tq), lambda i, qi, j: (i, jnp.maximum(qi - j, 0), 0, 0))

    def call(n_keys):
        windowed = n_keys < n
        out_specs = [qspec]
        out_shape = [jax.ShapeDtypeStruct((b, t, GROUP_WIDTH), F32)]
        if windowed:
            out_specs.append(pl.BlockSpec((1, 1, 8, 128), lambda i, qi, j: (i, qi, 0, 0)))
            out_shape.append(jax.ShapeDtypeStruct((b, n, 8, 128), F32))
        return pl.pallas_call(
            functools.partial(_fox_kernel, tq=tq, windowed=windowed), grid=(b, n, n_keys),
            in_specs=[qspec, kspec, kspec,
                      pl.BlockSpec((1, tq, SMALL_WIDTH), lambda i, qi, j: (i, qi, 0)),
                      pl.BlockSpec((1, SMALL_ROWS_T, tq), lambda i, qi, j: (i, 0, jnp.maximum(qi - j, 0)))],
            out_specs=out_specs, out_shape=out_shape,
            scratch_shapes=[pltpu.VMEM((N_HEADS, tq, 2 * HEAD_DIM), BF16),
                            pltpu.VMEM((N_HEADS, tq, 128), F32), pltpu.VMEM((N_HEADS, tq, 2 * HEAD_DIM), F32),
                            pltpu.VMEM((1, 128), F32), pltpu.VMEM((1, 128), F32), pltpu.VMEM((1, 128), F32),
                            pltpu.VMEM((n, 128), F32), pltpu.VMEM((n, 128), F32)],
            compiler_params=pltpu.CompilerParams(dimension_semantics=("arbitrary", "arbitrary", "arbitrary"),
                                                 vmem_limit_bytes=VMEM_LIMIT),
            name="fox_prompt_window" if windowed else "fox_prompt",
        )(fq, fk, fv, cum, cum_t)

    if n <= FOX_WINDOW:
        return call(n)[0]
    o_near, more = call(FOX_WINDOW)
    return lax.cond(jnp.max(more) > 0.0, lambda: call(n)[0], lambda: o_near)


SB_SUB = 256


def _sb_kernel(q_ref, k_ref, v_ref, o_ref, *refs, tq, windowed):
    if windowed:
        more_ref, *refs = refs
    qb_ref, carry_ref, acc_ref, qn_ref, kn_ref, cmin_ref = refs
    qi = pl.program_id(1)
    j = pl.program_id(2)
    last_j = jnp.minimum(qi, pl.num_programs(2) - 1)

    heads = range(N_HEADS)
    rows = lambda h: slice(h * HEAD_DIM, (h + 1) * HEAD_DIM)

    @pl.when((j == 0) & (qi == 0))
    def _():
        kn_ref[...] = jnp.zeros_like(kn_ref)

    @pl.when(j == 0)
    def _():
        carry_ref[...] = jnp.zeros_like(carry_ref)
        acc_ref[...] = jnp.zeros_like(acc_ref)
        qb = (q_ref[0] * LOG2E).astype(BF16)
        qb_ref[...] = qb
        qn2 = _head_sum(qb.astype(F32) * qb.astype(F32), _head_ones())
        kf = k_ref[0, 0].astype(F32)
        k2 = kf * kf
        qn_ref[...] = _lane_pack([_max_all(qn2[:, rows(h)]) for h in heads])
        kn_ref[pl.ds(qi, 1), :] = _lane_pack([_max_all(jnp.sum(k2[rows(h), :], axis=0, keepdims=True))
                                              for h in heads])
        cmin_ref[...] = jnp.zeros_like(cmin_ref)

    def may_matter(kn_rows):
        zcap2 = qn_ref[...] * kn_rows * (BOUND_SLACK * BOUND_SLACK)
        room = cmin_ref[...] - PRUNE_LOG2
        return (room < 0.0) | (zcap2 > room * room)

    def block_matters():
        return _any_head(may_matter(kn_ref[pl.ds(qi - j, 1), :]))

    def step(masked):
        k = k_ref[0, 0]
        v = v_ref[0, 0]
        r = _iota((tq, tq), 0)
        c = _iota((tq, tq), 1)
        sub = min(SB_SUB, tq)
        suffix = jnp.where(_iota((sub, sub), 1) <= _iota((sub, sub), 0), 1.0, 0.0).astype(BF16)
        z2s = [_dot(qb_ref[:, rows(h)], k[rows(h), :]) for h in heads]
        sps = [jnp.maximum(z2, 0.0) + jnp.log(1.0 + jnp.exp2(-jnp.abs(z2))) * INV_LN2 for z2 in z2s]
        if masked:
            sps = [jnp.where(c < r, sp, 0.0) for sp in sps]
        carries = [carry_ref[h] for h in heads]
        cum_parts = [[None] * (tq // sub) for _ in heads]
        for part in reversed(range(tq // sub)):
            ks = slice(part * sub, (part + 1) * sub)
            for h in heads:
                wide = jnp.concatenate([carries[h]] * (sub // 128), axis=1)
                cum = _dot_xc(sps[h][:, ks], suffix, parts=2) + wide
                cum_parts[h][part] = cum
                carries[h] = jnp.broadcast_to(cum[:, 0:1], carries[h].shape)
        es = [z2s[h] - jnp.concatenate(cum_parts[h], axis=1) for h in heads]
        if masked:
            es = [jnp.where(c < r, e, NEG_BIG) for e in es]
        pad = jnp.zeros((HEAD_DIM, tq), BF16)
        for h in heads:
            va = jnp.concatenate([v[rows(h), :], pad], axis=0)
            acc_ref[h] = acc_ref[h] + _dot_nt(jnp.exp2(es[h]).astype(BF16), va)
            carry_ref[h] = carries[h]
        cmin_ref[...] = _lane_pack([-_max_all(-carries[h]) for h in heads])

    @pl.when(j == 0)
    def _():
        step(True)

    @pl.when((j > 0) & (j <= qi))
    def _():
        @pl.when(block_matters())
        def _():
            step(False)

    @pl.when(j == last_j)
    def _():
        for h in range(N_HEADS):
            o_ref[0, :, h * HEAD_DIM:(h + 1) * HEAD_DIM] = acc_ref[h][:, :HEAD_DIM]
        if windowed:
            older = _iota(kn_ref.shape, 0) <= qi - pl.num_programs(2)
            more = _any_head(may_matter(kn_ref[...]) & older)
            more_ref[0, 0] = jnp.where(more, jnp.ones((8, 128), F32), jnp.zeros((8, 128), F32))


SB_WINDOW = 4


def _sb_prompt(sq, sk, sv, *, tq):
    b, t, _ = sq.shape
    n = t // tq
    qspec = pl.BlockSpec((1, tq, GROUP_WIDTH), lambda i, qi, j: (i, qi, 0))
    kspec = pl.BlockSpec((1, 1, GROUP_WIDTH, tq), lambda i, qi, j: (i, jnp.maximum(qi - j, 0), 0, 0))

    def call(n_keys):
        windowed = n_keys < n
        out_specs = [qspec]
        out_shape = [jax.ShapeDtypeStruct((b, t, GROUP_WIDTH), F32)]
        if windowed:
            out_specs.append(pl.BlockSpec((1, 1, 8, 128), lambda i, qi, j: (i, qi, 0, 0)))
            out_shape.append(jax.ShapeDtypeStruct((b, n, 8, 128), F32))
        return pl.pallas_call(
            functools.partial(_sb_kernel, tq=tq, windowed=windowed), grid=(b, n, n_keys),
            in_specs=[qspec, kspec, kspec],
            out_specs=out_specs, out_shape=out_shape,
            scratch_shapes=[pltpu.VMEM((tq, GROUP_WIDTH), BF16),
                            pltpu.VMEM((N_HEADS, tq, 128), F32), pltpu.VMEM((N_HEADS, tq, 2 * HEAD_DIM), F32),
                            pltpu.VMEM((1, 128), F32), pltpu.VMEM((n, 128), F32), pltpu.VMEM((1, 128), F32)],
            compiler_params=pltpu.CompilerParams(dimension_semantics=("arbitrary", "arbitrary", "arbitrary"),
                                                 vmem_limit_bytes=VMEM_LIMIT),
            name="sb_prompt_window" if windowed else "sb_prompt",
        )(sq, sk, sv)

    if n <= SB_WINDOW:
        return call(n)[0]
    o_near, more = call(SB_WINDOW)
    return lax.cond(jnp.max(more) > 0.0, lambda: call(n)[0], lambda: o_near)


GDN_CHUNK = 128
GDN_BASE = 16


def _unit_lower_inverses(lmats, ii, jj):
    ns = [jnp.where(ii // GDN_BASE == jj // GDN_BASE, -lm, 0.0) for lm in lmats]
    eye = jnp.where(ii == jj, 1.0, 0.0)
    ts = [eye + n for n in ns]
    ps = ns
    for _ in range(GDN_BASE.bit_length() - 2):
        ps = [_dot_f32(p, p) for p in ps]
        ts = [t + _dot_f32(t, p) for t, p in zip(ts, ps)]
    b = GDN_BASE
    while b < GDN_CHUNK:
        lower_left = (ii // (2 * b) == jj // (2 * b)) & (ii // b != jj // b)
        mids = [_dot_f32(t, jnp.where(lower_left, lm, 0.0)) for t, lm in zip(ts, lmats)]
        ts = [t - _dot_f32(mid, t) for t, mid in zip(ts, mids)]
        b *= 2
    return ts


def _gdn_prep_kernel(x_ref, w_ref, sm_ref, smt_ref, u_ref, wk_ref, qe_ref, attn_ref, kdt_ref, gct_ref, ext_ref, *, tt):
    t = pl.program_id(1)

    @pl.when(t == 0)
    def _():
        ext_ref[0:8, :] = jnp.zeros((8, 3 * GROUP_WIDTH), F32)

    @pl.when(t > 0)
    def _():
        ext_ref[0:8, :] = ext_ref[tt:tt + 8, :]

    ext_ref[8:8 + tt, :] = x_ref[0]
    y = ext_ref[pl.ds(8 - (CONV_WIDTH - 1), tt), :] * w_ref[0:1, :]
    for jw in range(1, CONV_WIDTH):
        y = y + ext_ref[pl.ds(8 - (CONV_WIDTH - 1) + jw, tt), :] * w_ref[jw:jw + 1, :]
    y = _silu(y)
    ones = _head_ones()
    q = y[:, :GROUP_WIDTH]
    k = y[:, GROUP_WIDTH:2 * GROUP_WIDTH]
    v = y[:, 2 * GROUP_WIDTH:]
    q = q * lax.rsqrt(_head_sum(q * q, ones) + NORM_EPS) * QK_SCALE
    k = k * lax.rsqrt(_head_sum(k * k, ones) + NORM_EPS)
    sm = sm_ref[0]
    smt = smt_ref[0]
    r = _iota((tt, tt), 0)
    c = _iota((tt, tt), 1)
    same = (r // GDN_CHUNK) == (c // GDN_CHUNK)
    gc_col = _dot_cx(jnp.where(same & (c <= r), 1.0, 0.0).astype(BF16), sm)
    gc_row = _dot_xc(smt, jnp.where(same & (r <= c), 1.0, 0.0).astype(BF16))
    gct_ref[0] = gc_row
    k_t = k.T
    ii = _iota((GDN_CHUNK, GDN_CHUNK), 0)
    jj = _iota((GDN_CHUNK, GDN_CHUNK), 1)
    n_chunks = tt // GDN_CHUNK
    lmats, rhss = [], []
    for ci in range(n_chunks):
        rs = slice(ci * GDN_CHUNK, (ci + 1) * GDN_CHUNK)
        qes, attns, kdts = [], [], []
        for h in range(N_HEADS):
            cs = slice(h * HEAD_DIM, (h + 1) * HEAD_DIM)
            gcol = gc_col[rs, LANE_GDEC + h:LANE_GDEC + h + 1]
            grow = gc_row[LANE_GDEC + h:LANE_GDEC + h + 1, rs]
            beta = sm[rs, LANE_BETA + h:LANE_BETA + h + 1]
            dec = jnp.exp(jnp.where(ii >= jj, gcol - grow, NEG_BIG))
            qh = q[rs, cs].astype(BF16)
            kth = k_t[cs, rs]
            kb = k[rs, cs] * beta
            eg = jnp.exp(gcol)
            lmats.append(jnp.where(ii > jj, _dot(kb.astype(BF16), kth.astype(BF16)) * dec, 0.0))
            rhss.append(jnp.concatenate([v[rs, cs] * beta, kb * eg], axis=1))
            qes.append(q[rs, cs] * eg)
            attns.append(_dot(qh, kth.astype(BF16)) * dec)
            kdts.append(kth * jnp.exp(grow[:, GDN_CHUNK - 1:GDN_CHUNK] - grow))
        qe_ref[0, rs, :] = jnp.concatenate(qes, axis=1).astype(BF16)
        attn_ref[0, rs, :] = jnp.concatenate(attns, axis=1).astype(BF16)
        kdt_ref[0, :, rs] = jnp.concatenate(kdts, axis=0).astype(BF16)
    tinvs = _unit_lower_inverses(lmats, ii, jj)
    sols = [_dot_f32(ti, rhs) for ti, rhs in zip(tinvs, rhss)]
    for ci in range(n_chunks):
        rs = slice(ci * GDN_CHUNK, (ci + 1) * GDN_CHUNK)
        chunk = sols[ci * N_HEADS:(ci + 1) * N_HEADS]
        u_ref[0, rs, :] = jnp.concatenate([sol[:, :HEAD_DIM] for sol in chunk], axis=1)
        wk_ref[0, rs, :] = jnp.concatenate([sol[:, HEAD_DIM:] for sol in chunk], axis=1).astype(BF16)


def _gdn_scan_kernel(u_ref, wk_ref, qe_ref, attn_ref, kdt_ref, gct_ref, o_ref, st_ref, s_ref, *, tt, nb):
    t = pl.program_id(0)

    @pl.when(t == 0)
    def _():
        s_ref[...] = jnp.zeros_like(s_ref)

    for ci in range(tt // GDN_CHUNK):
        rs = slice(ci * GDN_CHUNK, (ci + 1) * GDN_CHUNK)
        items = [(b, h) for b in range(nb) for h in range(N_HEADS)]
        cs = lambda h: slice(h * HEAD_DIM, (h + 1) * HEAD_DIM)
        us = [u_ref[b, rs, :] for b in range(nb)]
        wks = [wk_ref[b, rs, :] for b in range(nb)]
        qes = [qe_ref[b, rs, :] for b in range(nb)]
        attns = [attn_ref[b, rs, :] for b in range(nb)]
        kdts = [kdt_ref[b, :, rs] for b in range(nb)]
        decays = [jnp.exp(gct_ref[b, :, rs][:, GDN_CHUNK - 1:GDN_CHUNK]) for b in range(nb)]
        ss = [s_ref[b, h] for b, h in items]
        sbs = [s.astype(BF16) for s in ss]
        v_news = [us[b][:, cs(h)] - _dot(wks[b][:, cs(h)], sb) for (b, h), sb in zip(items, sbs)]
        vbs = [vn.astype(BF16) for vn in v_news]
        for (b, h), s, vb in zip(items, ss, vbs):
            a = decays[b][LANE_GDEC + h:LANE_GDEC + h + 1, :]
            s_ref[b, h] = a * s + _dot(kdts[b][cs(h), :], vb)
        for b in range(nb):
            outs = [_dot(qes[b][:, cs(h)], sbs[b * N_HEADS + h])
                    + _dot(attns[b][:, h * GDN_CHUNK:(h + 1) * GDN_CHUNK], vbs[b * N_HEADS + h])
                    for h in range(N_HEADS)]
            o_ref[b, rs, :] = jnp.concatenate(outs, axis=1)

    @pl.when(t == pl.num_programs(0) - 1)
    def _():
        st_ref[...] = s_ref[...]


def _gdn_prompt(dqkv, conv_w, small, small_t, *, tt):
    b, t, _ = dqkv.shape
    n_attn = N_HEADS * GDN_CHUNK
    by_rows = lambda w: pl.BlockSpec((1, tt, w), lambda i, j: (i, j, 0))
    by_cols = lambda r: pl.BlockSpec((1, r, tt), lambda i, j: (i, 0, j))
    u, wk, qe, attn, kdt, gct = pl.pallas_call(
        functools.partial(_gdn_prep_kernel, tt=tt), grid=(b, t // tt),
        in_specs=[by_rows(3 * GROUP_WIDTH), pl.BlockSpec(conv_w.shape, lambda i, j: (0, 0)),
                  by_rows(SMALL_WIDTH), by_cols(SMALL_ROWS_T)],
        out_specs=[by_rows(GROUP_WIDTH), by_rows(GROUP_WIDTH), by_rows(GROUP_WIDTH), by_rows(n_attn),
                   by_cols(GROUP_WIDTH), by_cols(SMALL_ROWS_T)],
        out_shape=[jax.ShapeDtypeStruct((b, t, GROUP_WIDTH), F32), jax.ShapeDtypeStruct((b, t, GROUP_WIDTH), BF16),
                   jax.ShapeDtypeStruct((b, t, GROUP_WIDTH), BF16), jax.ShapeDtypeStruct((b, t, n_attn), BF16),
                   jax.ShapeDtypeStruct((b, GROUP_WIDTH, t), BF16), jax.ShapeDtypeStruct((b, SMALL_ROWS_T, t), F32)],
        scratch_shapes=[pltpu.VMEM((tt + 8, 3 * GROUP_WIDTH), F32)],
        compiler_params=pltpu.CompilerParams(dimension_semantics=("arbitrary", "arbitrary"),
                                             vmem_limit_bytes=VMEM_LIMIT),
        name="gdn_prep",
    )(dqkv, conv_w, small, small_t)
    all_rows = lambda w: pl.BlockSpec((b, tt, w), lambda j: (0, j, 0))
    all_cols = lambda r: pl.BlockSpec((b, r, tt), lambda j: (0, 0, j))
    st_spec = pl.BlockSpec((b, N_HEADS, HEAD_DIM, HEAD_DIM), lambda j: (0, 0, 0, 0))
    return pl.pallas_call(
        functools.partial(_gdn_scan_kernel, tt=tt, nb=b), grid=(t // tt,),
        in_specs=[all_rows(GROUP_WIDTH), all_rows(GROUP_WIDTH), all_rows(GROUP_WIDTH), all_rows(n_attn),
                  all_cols(GROUP_WIDTH), all_cols(SMALL_ROWS_T)],
        out_specs=[all_rows(GROUP_WIDTH), st_spec],
        out_shape=[jax.ShapeDtypeStruct((b, t, GROUP_WIDTH), F32),
                   jax.ShapeDtypeStruct((b, N_HEADS, HEAD_DIM, HEAD_DIM), F32)],
        scratch_shapes=[pltpu.VMEM((b, N_HEADS, HEAD_DIM, HEAD_DIM), F32)],
        compiler_params=pltpu.CompilerParams(dimension_semantics=("arbitrary",), vmem_limit_bytes=VMEM_LIMIT),
        name="gdn_scan",
    )(u, wk, qe, attn, kdt, gct)


def _outproj_kernel(x_ref, ohg_ref, hg_ref, ofx_ref, osb_ref, odn_ref, dz_ref, gains_ref, w_ref, y_ref):
    ones = _head_ones()
    parts = [
        _head_rms(ohg_ref[...], ones, gains_ref[0:1, :]) * hg_ref[...],
        _head_rms(ofx_ref[...], ones, gains_ref[1:2, :]),
        _head_rms(osb_ref[...], ones, gains_ref[2:3, :]),
        _head_rms(odn_ref[...], ones, gains_ref[3:4, :]) * dz_ref[...],
    ]
    y = x_ref[...]
    for gidx, p in enumerate(parts):
        y = y + _dot(p.astype(BF16), w_ref[gidx * GROUP_WIDTH:(gidx + 1) * GROUP_WIDTH, :])
    y_ref[...] = y


def _outproj(x2, ohg, hgate, ofx, osb, odn, dz, gains, w_out, *, tm):
    m, d = x2.shape
    row = lambda i: (i, 0)
    const = lambda i: (0, 0)
    seg = pl.BlockSpec((tm, GROUP_WIDTH), row)
    return pl.pallas_call(
        _outproj_kernel, grid=(m // tm,),
        in_specs=[pl.BlockSpec((tm, d), row)] + [seg] * 6 + [pl.BlockSpec(gains.shape, const),
                                                            pl.BlockSpec(w_out.shape, const)],
        out_specs=pl.BlockSpec((tm, d), row),
        out_shape=jax.ShapeDtypeStruct((m, d), F32),
        compiler_params=pltpu.CompilerParams(dimension_semantics=("arbitrary",), vmem_limit_bytes=VMEM_LIMIT),
        name="outproj",
    )(x2, ohg, hgate, ofx, osb, odn, dz, gains, w_out)


def _mlp_kernel(x_ref, g2_ref, wu_ref, wd_ref, y_ref, h_ref, acc_ref):
    f = pl.program_id(1)

    @pl.when(f == 0)
    def _():
        x = x_ref[...]
        h_ref[...] = (x * lax.rsqrt(jnp.mean(x * x, axis=-1, keepdims=True) + NORM_EPS) * g2_ref[...]).astype(BF16)
        acc_ref[...] = x

    u = jnp.maximum(_dot(h_ref[...], wu_ref[...]), 0.0)
    acc_ref[...] += _dot((u * u).astype(BF16), wd_ref[...])

    @pl.when(f == pl.num_programs(1) - 1)
    def _():
        y_ref[...] = acc_ref[...]


def _mlp(x2, g2, w_up, w_down, *, tm, tf):
    m, d = x2.shape
    dff = w_up.shape[1]
    return pl.pallas_call(
        _mlp_kernel, grid=(m // tm, dff // tf),
        in_specs=[pl.BlockSpec((tm, d), lambda i, f: (i, 0)), pl.BlockSpec((1, d), lambda i, f: (0, 0)),
                  pl.BlockSpec((d, tf), lambda i, f: (0, f)), pl.BlockSpec((tf, d), lambda i, f: (f, 0))],
        out_specs=pl.BlockSpec((tm, d), lambda i, f: (i, 0)),
        out_shape=jax.ShapeDtypeStruct((m, d), F32),
        scratch_shapes=[pltpu.VMEM((tm, d), BF16), pltpu.VMEM((tm, d), F32)],
        compiler_params=pltpu.CompilerParams(dimension_semantics=("arbitrary", "arbitrary"),
                                             vmem_limit_bytes=VMEM_LIMIT),
        name="mlp",
    )(x2, g2, w_up, w_down)


PAGES_PER_STEP = 16
SEQS_PER_STEP = 4


def _head_rows(row):
    x = jnp.broadcast_to(row, (8, GROUP_WIDTH))
    return jnp.where(_iota((8, GROUP_WIDTH), 1) // HEAD_DIM == _iota((8, GROUP_WIDTH), 0), x, 0.0)


def _dec_attn_kernel(pt_ref, q_ref, kn_ref, vn_ref, sn_ref, *refs, fox, ns, pp, page):
    n_refs = ns * pp
    k_refs = refs[:n_refs]
    v_refs = refs[n_refs:2 * n_refs]
    rest = refs[2 * n_refs:]
    if fox:
        lf_refs = rest[:n_refs]
        rest = rest[n_refs:]
    o_ref, m_ref, l_ref, acc_ref, carry_ref = rest
    j = pl.program_id(1)
    seqs = range(ns)
    qbs = [_head_rows(q_ref[s]).astype(BF16) for s in seqs]
    r = _iota((page, page), 0)
    c = _iota((page, page), 1)
    later = jnp.where(r > c, 1.0, 0.0).astype(BF16)

    @pl.when(j == 0)
    def _():
        if fox:
            l_ref[...] = jnp.ones_like(l_ref)
            lane = _iota((8, SMALL_WIDTH), 1)
            row = _iota((8, SMALL_WIDTH), 0)
            for s in seqs:
                kn = jnp.broadcast_to(kn_ref[s], (8, GROUP_WIDTH)).astype(BF16)
                m_ref[s] = jnp.broadcast_to(_dot_nt(qbs[s], kn)[:, 0:1], m_ref.shape[1:])
                acc_ref[s] = jnp.broadcast_to(vn_ref[s], acc_ref.shape[1:]).astype(BF16).astype(F32)
                sn = jnp.broadcast_to(sn_ref[s], (8, SMALL_WIDTH))
                lf_new = jnp.sum(jnp.where(lane == row + LANE_FLOG, sn, 0.0), axis=-1, keepdims=True)
                carry_ref[s] = jnp.broadcast_to(jnp.where(_iota((8, 1), 0) < N_HEADS, lf_new, 0.0),
                                                carry_ref.shape[1:])
        else:
            acc_ref[...] = jnp.zeros_like(acc_ref)
            carry_ref[...] = jnp.zeros_like(carry_ref)

    g8 = pp * 8
    n8 = ns * g8
    per_page = lambda vals: jnp.concatenate([v for s in seqs for v in [vals[s]] * pp], axis=0)
    z = jnp.concatenate([_dot(qbs[s], k_refs[s * pp + i][0].astype(BF16)) for s in seqs for i in range(pp)], axis=0)
    if fox:
        pad = jnp.zeros((8 - N_HEADS, page), F32)
        x = jnp.concatenate([a for i in range(n_refs) for a in (lf_refs[i][0], pad)], axis=0)
    else:
        x = _log_sigmoid(-z)
    ri = _iota((n8, n8), 0)
    ci = _iota((n8, n8), 1)
    before = jnp.where((ri % 8 == ci % 8) & (ri // g8 == ci // g8) & (ci // 8 < ri // 8), 1.0, 0.0).astype(BF16)
    tot = _dot_xc(x, jnp.ones((page, page), BF16))
    upto = _dot_cx(before, tot) + per_page([carry_ref[s] for s in seqs])
    bias = _dot_xc(x, later) + upto
    new_carry = upto + tot
    for s in seqs:
        carry_ref[s] = new_carry[(s + 1) * g8 - 8:(s + 1) * g8, :]
    if fox:
        sc = z + bias
        m_prev = m_ref[...]
        m_new = jnp.maximum(m_prev, jnp.max(jnp.max(sc.reshape(ns, pp, 8, page), axis=1), axis=-1, keepdims=True))
        alpha = jnp.exp(m_prev - m_new)
        p = jnp.exp(sc - per_page([m_new[s] for s in seqs]))
        l_ref[...] = alpha * l_ref[...] + jnp.sum(jnp.sum(p.reshape(ns, pp, 8, page), axis=1), axis=-1, keepdims=True)
        m_ref[...] = m_new
    else:
        p = jnp.exp(z + x + bias)
    for s in seqs:
        pv = None
        for i in range(pp):
            row0 = (s * pp + i) * 8
            term = _dot_nt(p[row0:row0 + 8].astype(BF16), v_refs[s * pp + i][0].astype(BF16))
            pv = term if pv is None else pv + term
        if fox:
            acc_ref[s] = alpha[s][:, 0:1] * acc_ref[s] + pv
        else:
            acc_ref[s] = acc_ref[s] + pv

    @pl.when(j == pl.num_programs(1) - 1)
    def _():
        own = _iota((8, GROUP_WIDTH), 1) // HEAD_DIM == _iota((8, GROUP_WIDTH), 0)
        for s in seqs:
            acc = acc_ref[s]
            if fox:
                acc = acc / l_ref[s][:, 0:1]
            o_ref[s] = jnp.sum(jnp.where(own, acc, 0.0), axis=0, keepdims=True)


def _dec_attn(page_table, q, k_new, v_new, small, cache_k, cache_v, cache_lf_t, *, layer, fox):
    nb = q.shape[0]
    n_pages = page_table.shape[1]
    page = cache_k.shape[3]
    assert page == 128, "per-page statistics are kept one page per vreg row group"
    pp = _pick(n_pages, (PAGES_PER_STEP, 8, 4, 2, 1))
    ns = _pick(nb, (SEQS_PER_STEP, 1))
    row3 = lambda a: a.reshape(nb, 1, a.shape[-1])
    rspec = lambda w: pl.BlockSpec((ns, 1, w), lambda b, j, pt: (b, 0, 0))

    def page_map(s, i):
        return lambda b, j, pt: (layer, pt[b * ns + s, n_pages - 1 - (j * pp + i)], 0, 0)

    slots = [(s, i) for s in range(ns) for i in range(pp)]
    in_specs = [rspec(GROUP_WIDTH)] * 3 + [rspec(SMALL_WIDTH)]
    in_specs += [pl.BlockSpec((None, 1, GROUP_WIDTH, page), page_map(s, i)) for s, i in slots] * 2
    args = [row3(q), row3(k_new), row3(v_new), row3(small)] + [cache_k] * len(slots) + [cache_v] * len(slots)
    if fox:
        in_specs += [pl.BlockSpec((None, 1, N_HEADS, page), page_map(s, i)) for s, i in slots]
        args += [cache_lf_t] * len(slots)
    grid_spec = pltpu.PrefetchScalarGridSpec(
        num_scalar_prefetch=1, grid=(nb // ns, n_pages // pp), in_specs=in_specs,
        out_specs=pl.BlockSpec((ns, 1, GROUP_WIDTH), lambda b, j, pt: (b, 0, 0)),
        scratch_shapes=[pltpu.VMEM((ns, 8, 128), F32), pltpu.VMEM((ns, 8, 128), F32),
                        pltpu.VMEM((ns, 8, GROUP_WIDTH), F32), pltpu.VMEM((ns, 8, 128), F32)])
    out = pl.pallas_call(
        functools.partial(_dec_attn_kernel, fox=fox, ns=ns, pp=pp, page=page), grid_spec=grid_spec,
        out_shape=jax.ShapeDtypeStruct((nb, 1, GROUP_WIDTH), F32),
        compiler_params=pltpu.CompilerParams(dimension_semantics=("arbitrary", "arbitrary"),
                                             vmem_limit_bytes=VMEM_LIMIT),
        name="fox_step" if fox else "sb_step",
    )(page_table, *args)
    return out.reshape(nb, GROUP_WIDTH)


def _column(row, eye):
    return jnp.sum(eye * row, axis=1, keepdims=True)


def _rec_step_kernel(hq_ref, hlf_ref, hk_ref, hi_ref, dx_ref, sm_ref, w_ref, shg_ref, sdn_ref, buf_ref,
                     ohg_ref, odn_ref, shg_o_ref, sdn_o_ref, buf_o_ref):
    eye = jnp.where(_iota((HEAD_DIM, HEAD_DIM), 0) == _iota((HEAD_DIM, HEAD_DIM), 1), 1.0, 0.0)
    hq = hq_ref[0]
    hlf = hlf_ref[0]
    hk = hk_ref[0]
    hv = hi_ref[0]
    sm = sm_ref[0]
    buf = buf_ref[0]
    x_new = dx_ref[0]
    y = x_new * w_ref[CONV_WIDTH - 1:CONV_WIDTH, :]
    for jw in range(CONV_WIDTH - 1):
        y = y + buf[jw:jw + 1, :] * w_ref[jw:jw + 1, :]
    y = _silu(y)
    buf_o_ref[0] = jnp.concatenate([buf[1:CONV_WIDTH - 1, :], x_new], axis=0)
    for h in range(N_HEADS):
        cs = slice(h * HEAD_DIM, (h + 1) * HEAD_DIM)
        s = shg_ref[0, h]
        s = _column(jnp.exp(hlf[:, cs]), eye) * s + _column(hk[:, cs], eye) * hv[:, cs]
        shg_o_ref[0, h] = s
        ohg_ref[0, :, cs] = jnp.sum(_column(hq[:, cs], eye) * s, axis=0, keepdims=True)
        q = y[:, cs]
        k = y[:, GROUP_WIDTH + h * HEAD_DIM:GROUP_WIDTH + (h + 1) * HEAD_DIM]
        v = y[:, 2 * GROUP_WIDTH + h * HEAD_DIM:2 * GROUP_WIDTH + (h + 1) * HEAD_DIM]
        q = q * lax.rsqrt(jnp.sum(q * q, axis=-1, keepdims=True) + NORM_EPS) * QK_SCALE
        k = k * lax.rsqrt(jnp.sum(k * k, axis=-1, keepdims=True) + NORM_EPS)
        beta = sm[:, LANE_BETA + h:LANE_BETA + h + 1]
        a = jnp.exp(sm[:, LANE_GDEC + h:LANE_GDEC + h + 1])
        s = sdn_ref[0, h]
        kc = _column(k, eye)
        v_new = beta * (v - a * jnp.sum(kc * s, axis=0, keepdims=True))
        s = a * s + kc * v_new
        sdn_o_ref[0, h] = s
        odn_ref[0, :, cs] = jnp.sum(_column(q, eye) * s, axis=0, keepdims=True)


def _rec_step(hq, hlf, hk, hi, dqkv, small, conv_w, s_hg, s_dn, buf):
    nb = hq.shape[0]
    row3 = lambda a: a.reshape(nb, 1, a.shape[-1])
    rspec = lambda w: pl.BlockSpec((1, 1, w), lambda b: (b, 0, 0))
    st_spec = pl.BlockSpec((1, N_HEADS, HEAD_DIM, HEAD_DIM), lambda b: (b, 0, 0, 0))
    buf_spec = pl.BlockSpec((1, CONV_WIDTH - 1, 3 * GROUP_WIDTH), lambda b: (b, 0, 0))
    outs = pl.pallas_call(
        _rec_step_kernel, grid=(nb,),
        in_specs=[rspec(GROUP_WIDTH)] * 4 + [rspec(3 * GROUP_WIDTH), rspec(SMALL_WIDTH),
                                             pl.BlockSpec(conv_w.shape, lambda b: (0, 0)), st_spec, st_spec, buf_spec],
        out_specs=[rspec(GROUP_WIDTH), rspec(GROUP_WIDTH), st_spec, st_spec, buf_spec],
        out_shape=[jax.ShapeDtypeStruct((nb, 1, GROUP_WIDTH), F32)] * 2
        + [jax.ShapeDtypeStruct(s_hg.shape, F32), jax.ShapeDtypeStruct(s_dn.shape, F32),
           jax.ShapeDtypeStruct(buf.shape, F32)],
        compiler_params=pltpu.CompilerParams(dimension_semantics=("arbitrary",), vmem_limit_bytes=VMEM_LIMIT),
        name="recurrent_step",
    )(row3(hq), row3(hlf), row3(hk), row3(hi), row3(dqkv), row3(small), conv_w, s_hg, s_dn, buf)
    ohg, odn, s_hg_new, s_dn_new, buf_new = outs
    return ohg.reshape(nb, GROUP_WIDTH), odn.reshape(nb, GROUP_WIDTH), s_hg_new, s_dn_new, buf_new


def _tile_gain(g):
    return jnp.tile(g.astype(F32), N_HEADS)


def _relayout_w_in(w_in_t_l):
    gw = GROUP_WIDTH
    a = 7 * gw
    e = a + N_HEADS + 7 * gw
    pad = jnp.zeros((SMALL_WIDTH - 3 * N_HEADS, w_in_t_l.shape[1]), w_in_t_l.dtype)
    rows = [w_in_t_l[:a], w_in_t_l[a + N_HEADS:e], w_in_t_l[a:a + N_HEADS], w_in_t_l[e:e + 2 * N_HEADS], pad]
    return jnp.concatenate(rows, axis=0).astype(BF16)


def _small_params(f_bias, dt_bias, a_log):
    sp = jnp.zeros((8, SMALL_WIDTH), F32)
    sp = sp.at[0, LANE_FLOG:LANE_FLOG + N_HEADS].set(f_bias.astype(F32))
    sp = sp.at[1, LANE_GDEC:LANE_GDEC + N_HEADS].set(dt_bias.astype(F32))
    sp = sp.at[2, LANE_GDEC:LANE_GDEC + N_HEADS].set(a_log.astype(F32))
    return sp


def _pick(n, candidates):
    for c in candidates:
        if n % c == 0:
            return c
    return n


def kernel(x_prompt, x_sample, cache_fox_k, cache_fox_v, cache_fox_logf, cache_sb_k, cache_sb_v, state_hgrn, state_dn, state_dn_conv, page_table, hgrn_lb_param, w_in, w_out, ln1_g, ln2_g, fox_f_bias, fox_q_norm, fox_k_norm, sb_q_norm, sb_k_norm, hgrn_out_norm, fox_out_norm, sb_out_norm, dn_out_norm, dn_conv_w, dn_dt_bias, dn_a_log, w_up, w_down):
    depth = w_in.shape[0]
    bsz, seq, d = x_prompt.shape
    nb = x_sample.shape[0]
    n_phys, page = cache_fox_k.shape[1], cache_fox_k.shape[2]
    m = bsz * seq
    tq = _pick(seq, (512, 256, 128))
    tm = _pick(tq, (256, 128))
    tm_mlp = _pick(m, (1024, 512, 256, 128, 64, 32, 16, 8))
    tf = _pick(w_up.shape[2], (1024, 512, 256, 128))
    tt = _pick(seq, (256, 128, 64))

    yp = x_prompt.reshape(m, d)
    ys = x_sample.reshape(nb, d)
    lbp = hgrn_lb_param.astype(F32)
    w_in_t = jnp.transpose(w_in, (2, 0, 1))
    kv_t = lambda a: a.transpose(0, 1, 3, 4, 2).reshape(depth, n_phys, GROUP_WIDTH, page)
    lf_t = jnp.swapaxes(cache_fox_logf.astype(F32), 2, 3)
    p_out = [[] for _ in range(8)]
    s_out = [[] for _ in range(8)]
    for l in range(depth):
        w_re = _relayout_w_in(w_in_t[:, l, :])
        w_o = w_out[l].astype(BF16)
        w_u = w_up[l].astype(BF16)
        w_d = w_down[l].astype(BF16)
        g1 = ln1_g[l].reshape(1, d).astype(F32)
        g2 = ln2_g[l].reshape(1, d).astype(F32)
        qk_gains = jnp.stack([_tile_gain(fox_q_norm[l]), _tile_gain(fox_k_norm[l]),
                              _tile_gain(sb_q_norm[l]), _tile_gain(sb_k_norm[l])])
        out_gains = jnp.stack([_tile_gain(hgrn_out_norm[l]), _tile_gain(fox_out_norm[l]),
                               _tile_gain(sb_out_norm[l]), _tile_gain(dn_out_norm[l])])
        sp = _small_params(fox_f_bias[l], dn_dt_bias[l], dn_a_log[l])
        conv_w = dn_conv_w[l].astype(F32)

        (hq, hlf, hk, hi, hgate, fq, fk, fv, sq, sk, sv, dqkv, dz, small, cum, small_t, cum_t,
         fk_blk, fv_blk, sk_blk, sv_blk) = _inproj(
            yp, g1, w_re, lbp, qk_gains, sp, layer=l, tm=tm, rows_per_seq=seq, with_time=True, key_block=tq)
        b3 = lambda a: a.reshape(bsz, seq, a.shape[-1])
        o_hg, st_hg = _hgrn_prompt(b3(hq), b3(hlf), b3(hk), b3(hi), tt=tt)
        o_fx = _fox_prompt(b3(fq), fk_blk, fv_blk, b3(cum), cum_t, tq=tq)
        o_sb = _sb_prompt(b3(sq), sk_blk, sv_blk, tq=tq)
        o_dn, st_dn = _gdn_prompt(b3(dqkv), conv_w, b3(small), small_t, tt=tt)
        yp = _outproj(yp, o_hg.reshape(m, -1), hgate, o_fx.reshape(m, -1), o_sb.reshape(m, -1),
                      o_dn.reshape(m, -1), dz, out_gains, w_o, tm=tm)
        yp = _mlp(yp, g2, w_u, w_d, tm=tm_mlp, tf=tf)
        by_head = lambda a: a.reshape(bsz, N_HEADS, HEAD_DIM, seq).transpose(0, 3, 1, 2)
        p_out[0].append(by_head(fk))
        p_out[1].append(by_head(fv))
        p_out[2].append(b3(small)[:, :, LANE_FLOG:LANE_FLOG + N_HEADS])
        p_out[3].append(by_head(sk))
        p_out[4].append(by_head(sv))
        p_out[5].append(jnp.swapaxes(st_hg, -1, -2))
        p_out[6].append(st_dn)
        p_out[7].append(b3(dqkv)[:, seq - (CONV_WIDTH - 1):, :])

        (hq, hlf, hk, hi, hgate, fq, fk, fv, sq, sk, sv, dqkv, dz, small) = _inproj(
            ys, g1, w_re, lbp, qk_gains, sp, layer=l, tm=nb, rows_per_seq=nb, with_time=False)
        o_fx = _dec_attn(page_table, fq, fk, fv, small, kv_t(cache_fox_k), kv_t(cache_fox_v), lf_t,
                         layer=l, fox=True)
        o_sb = _dec_attn(page_table, sq, sk, sv, small, kv_t(cache_sb_k), kv_t(cache_sb_v), None,
                         layer=l, fox=False)
        o_hg, o_dn, s_hg_new, s_dn_new, buf_new = _rec_step(
            hq, hlf, hk, hi, dqkv, small, conv_w, state_hgrn[l].astype(F32), state_dn[l].astype(F32),
            state_dn_conv[l].astype(F32))
        ys = _outproj(ys, o_hg, hgate, o_fx, o_sb, o_dn, dz, out_gains, w_o, tm=nb)
        ys = _mlp(ys, g2, w_u, w_d, tm=nb, tf=tf)
        sshape = (nb, 1, N_HEADS, HEAD_DIM)
        s_out[0].append(fk.reshape(sshape))
        s_out[1].append(fv.reshape(sshape))
        s_out[2].append(small[:, LANE_FLOG:LANE_FLOG + N_HEADS].reshape(nb, 1, N_HEADS))
        s_out[3].append(sk.reshape(sshape))
        s_out[4].append(sv.reshape(sshape))
        s_out[5].append(s_hg_new)
        s_out[6].append(s_dn_new)
        s_out[7].append(buf_new)

    p = [jnp.stack(v) for v in p_out]
    s = [jnp.stack(v) for v in s_out]
    return (yp.reshape(bsz, seq, d), ys.reshape(nb, 1, d), *p, *s)
```
